```python
import math
import jax, jax.numpy as jnp
from jax import lax
import numpy as np

D_MODEL = 2048
BATCH = 8
SEQ = 2048
DEPTH = 2

GRID_W = 64
CTX_LEN = 256
Q_BLOCK = 128
ROPE_THETA = 10000.0
EPS = 1e-6
NEG_INF = -1e30

NA_HEADS = 4
NA_HEAD_DIM = 128
NA_KH_MAX = 8
NA_KW = 16
NA_QCB = 16
NA_KCB = 2 * NA_KW
MLA_HEADS = 4
MLA_NOPE = 128
MLA_ROPE = 64
MLA_V = 128
MLA_Q_RANK = 384
MLA_KV_RANK = 128
GQA_HEADS = 4
GQA_KV_HEADS = 2
GQA_HEAD_DIM = 128
DIFF_HEADS = 4
DIFF_QK_DIM = 64
DIFF_V_DIM = 128

NA_SCALE = NA_HEAD_DIM ** -0.5
MLA_SCALE = (MLA_NOPE + MLA_ROPE) ** -0.5
GQA_SCALE = GQA_HEAD_DIM ** -0.5
DIFF_SCALE = DIFF_QK_DIM ** -0.5

MIX_WIDTH = NA_HEADS * NA_HEAD_DIM + MLA_HEADS * MLA_V + GQA_HEADS * GQA_HEAD_DIM + DIFF_HEADS * DIFF_V_DIM
IN_SIZES = (NA_HEADS * NA_HEAD_DIM, NA_HEADS * NA_HEAD_DIM, NA_HEADS * NA_HEAD_DIM,
            MLA_Q_RANK, MLA_KV_RANK, MLA_ROPE,
            GQA_HEADS * GQA_HEAD_DIM, GQA_KV_HEADS * GQA_HEAD_DIM, GQA_KV_HEADS * GQA_HEAD_DIM,
            DIFF_HEADS * 2 * DIFF_QK_DIM, DIFF_HEADS * 2 * DIFF_QK_DIM, DIFF_HEADS * DIFF_V_DIM)
IN_COLS = sum(IN_SIZES)
IN_SPLIT_POINTS = tuple(int(v) for v in np.cumsum(IN_SIZES)[:-1])

D_FF = 5632
N_EXPERTS = 8
TOP_K = 2
D_FF_EXPERT = 7168

kernel_name = 'hybrid_parallel_heads_dit_block'


def rmsnorm(x, g):
    xf = x.astype(jnp.float32)
    y = xf * lax.rsqrt(jnp.mean(xf * xf, axis=-1, keepdims=True) + EPS)
    return y.astype(x.dtype) * g


def modulate(x, shift, scale):
    return x * (1.0 + scale) + shift


def split_heads(t, n_heads):
    return t.reshape(t.shape[:2] + (n_heads, t.shape[-1] // n_heads))


def axial_angles(n_tokens, dim):
    half = dim // 2
    freqs = ROPE_THETA ** (-jnp.arange(0, half, 2, dtype=jnp.float32) / half)
    t = jnp.arange(n_tokens, dtype=jnp.int32)
    row = (t // GRID_W).astype(jnp.float32)
    col = (t % GRID_W).astype(jnp.float32)
    ang = jnp.stack([row[:, None] * freqs, col[:, None] * freqs], axis=1)
    return jnp.cos(ang), jnp.sin(ang)


def rope_2d(x, cos, sin):
    q = x.shape[-1] // 4
    xr = x.reshape(x.shape[:-1] + (2, 2, q))
    shape = (1, x.shape[1]) + (1,) * (x.ndim - 3) + (2, q)
    c = cos.reshape(shape).astype(x.dtype)
    s = sin.reshape(shape).astype(x.dtype)
    x1, x2 = xr[..., 0, :], xr[..., 1, :]
    out = jnp.stack([x1 * c - x2 * s, x2 * c + x1 * s], axis=-2)
    return out.reshape(x.shape)


def over_query_blocks(fn, *qs):
    b, n = qs[0].shape[:2]
    nb = n // Q_BLOCK
    blocks = tuple(jnp.moveaxis(q.reshape((b, nb, Q_BLOCK) + q.shape[2:]), 1, 0) for q in qs)
    out = lax.map(lambda qb: fn(*qb), blocks)
    out = jnp.moveaxis(out, 0, 1)
    return out.reshape((b, n) + out.shape[3:])


def softmax_attend(q, k, v, scale):
    b, nq, h, dk = q.shape
    g = k.shape[2]
    qg = q.reshape(b, nq, g, h // g, dk)
    s = jnp.einsum('bqgrd,bkgd->bgrqk', qg, k) * scale
    p = jax.nn.softmax(s.astype(jnp.float32), axis=-1).astype(v.dtype)
    o = jnp.einsum('bgrqk,bkgv->bqgrv', p, v)
    return o.reshape(b, nq, h, v.shape[-1])


def diff_attend(q1, q2, k1, k2, v, lam, scale):
    s1 = jnp.einsum('bqhd,bkhd->bhqk', q1, k1) * scale
    s2 = jnp.einsum('bqhd,bkhd->bhqk', q2, k2) * scale
    p = jax.nn.softmax(s1.astype(jnp.float32), axis=-1) - lam * jax.nn.softmax(s2.astype(jnp.float32), axis=-1)
    return jnp.einsum('bhqk,bkhv->bqhv', p.astype(v.dtype), v)


def natten_attend(q, k, v, k_ctx, v_ctx, rel_bias):
    b, n, h, d = q.shape
    rows = n // GRID_W
    kh = min(NA_KH_MAX, rows)
    ncb = GRID_W // NA_QCB
    qc = np.arange(GRID_W).reshape(ncb, NA_QCB)
    kb = np.clip(np.arange(ncb) * NA_QCB - NA_KW // 2, 0, GRID_W - NA_KCB)
    kc = kb[:, None] + np.arange(NA_KCB)
    cs = np.clip(qc - NA_KW // 2, 0, GRID_W - NA_KW)
    col_ok = (kc[:, None, :] >= cs[:, :, None]) & (kc[:, None, :] < cs[:, :, None] + NA_KW)
    col_off = np.clip(kc[:, None, :] - qc[:, :, None] + NA_KW - 1, 0, 2 * NA_KW - 2)
    col_bias = rel_bias[:, :, col_off]
    mask = jnp.asarray(np.broadcast_to(col_ok[:, :, None, :], (ncb, NA_QCB, kh, NA_KCB)).reshape(ncb, NA_QCB, kh * NA_KCB))
    nk = kh * NA_KCB

    def one_row(r):
        rs = jnp.clip(r - kh // 2, 0, rows - kh)
        qr = lax.dynamic_slice_in_dim(q, r * GRID_W, GRID_W, axis=1).reshape(b, ncb, NA_QCB, h, d)

        def band(t):
            t = lax.dynamic_slice_in_dim(t, rs * GRID_W, kh * GRID_W, axis=1).reshape(b, kh, GRID_W, h, d)
            t = t[:, :, kc]
            return jnp.moveaxis(t, 2, 1).reshape(b, ncb, nk, h, d)

        kr, vr = band(k), band(v)
        row_idx = rs - r + jnp.arange(kh) + NA_KH_MAX - 1
        bias = jnp.take(col_bias, row_idx, axis=1)
        bias = jnp.transpose(bias, (0, 2, 3, 1, 4)).reshape(h, ncb, NA_QCB, nk).astype(jnp.float32)
        s_lat = jnp.einsum('bnqhd,bnkhd->bhnqk', qr, kr).astype(jnp.float32) * NA_SCALE + bias
        s_lat = jnp.where(mask, s_lat, NEG_INF)
        s_ctx = jnp.einsum('bnqhd,bkhd->bhnqk', qr, k_ctx).astype(jnp.float32) * NA_SCALE
        p = jax.nn.softmax(jnp.concatenate([s_lat, s_ctx], axis=-1), axis=-1).astype(v.dtype)
        o = (jnp.einsum('bhnqk,bnkhd->bnqhd', p[..., :nk], vr)
             + jnp.einsum('bhnqk,bkhd->bnqhd', p[..., nk:], v_ctx))
        return o.reshape(b, GRID_W, h, d)

    out = lax.map(one_row, jnp.arange(rows, dtype=jnp.int32))
    return jnp.moveaxis(out, 0, 1).reshape(b, n, h, d)


def mla_queries(cq, g_q, w_qup, rope):
    q = split_heads(rmsnorm(cq, g_q) @ w_qup, MLA_HEADS)
    q_nope, q_pe = q[..., :MLA_NOPE], q[..., MLA_NOPE:]
    if rope is not None:
        q_pe = rope_2d(q_pe, *rope)
    return jnp.concatenate([q_nope, q_pe], axis=-1)


def mla_keys_values(ckv, k_pe, g_kv, w_kvup, rope):
    kv = split_heads(rmsnorm(ckv, g_kv) @ w_kvup, MLA_HEADS)
    k_nope, v = kv[..., :MLA_NOPE], kv[..., MLA_NOPE:]
    k_pe = k_pe[:, :, None, :]
    if rope is not None:
        k_pe = rope_2d(k_pe, *rope)
    k_pe = jnp.broadcast_to(k_pe, k_nope.shape[:-1] + (MLA_ROPE,))
    return jnp.concatenate([k_nope, k_pe], axis=-1), v


def gqa_qk(t, n_heads, g, rope):
    z = rmsnorm(split_heads(t, n_heads), g)
    if rope is not None:
        z = rope_2d(z, *rope)
    return z


def diff_qk(t, rope):
    z = t.reshape(t.shape[:2] + (DIFF_HEADS, 2, DIFF_QK_DIM))
    if rope is not None:
        z = rope_2d(z, *rope)
    return z[..., 0, :], z[..., 1, :]


def diff_lambda(lq1, lk1, lq2, lk2, lam_init):
    e = lambda a, b: jnp.exp(jnp.sum(a.astype(jnp.float32) * b.astype(jnp.float32)))
    return e(lq1, lk1) - e(lq2, lk2) + lam_init


def merge_heads(o_na, o_mla, o_gqa, o_diff, g_sub, lam_init, w_out):
    o_diff = rmsnorm(o_diff, g_sub) * (1.0 - lam_init)
    b, n = o_na.shape[:2]
    o = jnp.concatenate([o_na.reshape(b, n, -1), o_mla.reshape(b, n, -1),
                         o_gqa.reshape(b, n, -1), o_diff.reshape(b, n, -1)], axis=-1)
    return o @ w_out


def token_mixer(u, uc, lp, lam_init, rope64, rope128, need_ctx):
    (na_q, na_k, na_v, mla_cq, mla_ckv, mla_kpe, gqa_q, gqa_k, gqa_v,
     diff_q, diff_k, diff_v) = jnp.split(u @ lp['w_in'], IN_SPLIT_POINTS, axis=-1)
    (na_q_ctx, na_k_ctx, na_v_ctx, mla_cq_ctx, mla_ckv_ctx, mla_kpe_ctx, gqa_q_ctx, gqa_k_ctx, gqa_v_ctx,
     diff_q_ctx, diff_k_ctx, diff_v_ctx) = jnp.split(uc @ lp['w_in'], IN_SPLIT_POINTS, axis=-1)
    cat = lambda a, b: jnp.concatenate([a, b], axis=1)

    k_na_ctx, v_na_ctx = split_heads(na_k_ctx, NA_HEADS), split_heads(na_v_ctx, NA_HEADS)
    o_na = natten_attend(split_heads(na_q, NA_HEADS), split_heads(na_k, NA_HEADS), split_heads(na_v, NA_HEADS),
                         k_na_ctx, v_na_ctx, lp['na_rel_bias'])
    k_mla_ctx, v_mla_ctx = mla_keys_values(mla_ckv_ctx, mla_kpe_ctx, lp['mla_g_kv'], lp['mla_w_kvup'], None)
    k_mla, v_mla = mla_keys_values(mla_ckv, mla_kpe, lp['mla_g_kv'], lp['mla_w_kvup'], rope64)
    k_mla, v_mla = cat(k_mla_ctx, k_mla), cat(v_mla_ctx, v_mla)
    q_mla = mla_queries(mla_cq, lp['mla_g_q'], lp['mla_w_qup'], rope64)
    o_mla = over_query_blocks(lambda q: softmax_attend(q, k_mla, v_mla, MLA_SCALE), q_mla)
    k_gqa_ctx = gqa_qk(gqa_k_ctx, GQA_KV_HEADS, lp['gqa_g_k'], None)
    v_gqa_ctx = split_heads(gqa_v_ctx, GQA_KV_HEADS)
    k_gqa = cat(k_gqa_ctx, gqa_qk(gqa_k, GQA_KV_HEADS, lp['gqa_g_k'], rope128))
    v_gqa = cat(v_gqa_ctx, split_heads(gqa_v, GQA_KV_HEADS))
    q_gqa = gqa_qk(gqa_q, GQA_HEADS, lp['gqa_g_q'], rope128)
    o_gqa = over_query_blocks(lambda q: softmax_attend(q, k_gqa, v_gqa, GQA_SCALE), q_gqa)
    lam = diff_lambda(lp['diff_lq1'], lp['diff_lk1'], lp['diff_lq2'], lp['diff_lk2'], lam_init)
    k1_ctx, k2_ctx = diff_qk(diff_k_ctx, None)
    v_diff_ctx = split_heads(diff_v_ctx, DIFF_HEADS)
    k1, k2 = diff_qk(diff_k, rope64)
    k1, k2 = cat(k1_ctx, k1), cat(k2_ctx, k2)
    v_diff = cat(v_diff_ctx, split_heads(diff_v, DIFF_HEADS))
    q1, q2 = diff_qk(diff_q, rope64)
    o_diff = over_query_blocks(lambda a, b: diff_attend(a, b, k1, k2, v_diff, lam, DIFF_SCALE), q1, q2)
    out = merge_heads(o_na, o_mla, o_gqa, o_diff, lp['diff_g_sub'], lam_init, lp['w_out'])
    if not need_ctx:
        return out, None

    o_na_ctx = over_query_blocks(lambda q: softmax_attend(q, k_na_ctx, v_na_ctx, NA_SCALE), split_heads(na_q_ctx, NA_HEADS))
    q_mla_ctx = mla_queries(mla_cq_ctx, lp['mla_g_q'], lp['mla_w_qup'], None)
    o_mla_ctx = over_query_blocks(lambda q: softmax_attend(q, k_mla_ctx, v_mla_ctx, MLA_SCALE), q_mla_ctx)
    q_gqa_ctx = gqa_qk(gqa_q_ctx, GQA_HEADS, lp['gqa_g_q'], None)
    o_gqa_ctx = over_query_blocks(lambda q: softmax_attend(q, k_gqa_ctx, v_gqa_ctx, GQA_SCALE), q_gqa_ctx)
    q1_ctx, q2_ctx = diff_qk(diff_q_ctx, None)
    o_diff_ctx = over_query_blocks(lambda a, b: diff_attend(a, b, k1_ctx, k2_ctx, v_diff_ctx, lam, DIFF_SCALE), q1_ctx, q2_ctx)
    out_ctx = merge_heads(o_na_ctx, o_mla_ctx, o_gqa_ctx, o_diff_ctx, lp['diff_g_sub'], lam_init, lp['w_out'])
    return out, out_ctx


def swiglu(u, w_gate, w_up, w_down):
    return (jax.nn.silu(u @ w_gate) * (u @ w_up)) @ w_down


def moe_swiglu(u, w_router, b_router, w_gate, w_up, w_down):
    b, n, d = u.shape
    t = u.reshape(b * n, d)
    logits = (t @ w_router).astype(jnp.float32) + b_router.astype(jnp.float32)
    top_val, top_idx = lax.top_k(logits, TOP_K)
    gates = jax.nn.softmax(top_val, axis=-1)
    comb = jnp.sum(jax.nn.one_hot(top_idx, N_EXPERTS, dtype=jnp.float32) * gates[..., None], axis=1).astype(u.dtype)
    y = jnp.zeros_like(t)
    for e in range(N_EXPERTS):
        y = y + comb[:, e:e + 1] * swiglu(t, w_gate[e], w_up[e], w_down[e])
    return y.reshape(b, n, d)


def setup_inputs(seed: int = 0) -> dict:
    key = jax.random.key(seed)
    keys = jax.random.split(key, 32)
    n_dense = (DEPTH + 1) // 2
    n_moe = DEPTH // 2
    D = D_MODEL

    def nrm(i, shape, scale):
        return scale * jax.random.normal(keys[i], shape, jnp.float32)

    def gain(i, shape):
        return 1.0 + nrm(i, shape, 0.1)

    return {
        'x': nrm(0, (BATCH, SEQ, D), 1.0),
        'c': nrm(1, (BATCH, D), 1.0),
        'ctx': nrm(2, (BATCH, CTX_LEN, D), 1.0),
        'c_ctx': nrm(3, (D,), 1.0),
        'w_mod': nrm(4, (DEPTH, D, 6 * D), 0.3 * D ** -0.5),
        'b_mod': nrm(5, (DEPTH, 6 * D), 0.02),
        'g_mix': gain(6, (DEPTH, D)),
        'w_in': nrm(7, (DEPTH, D, IN_COLS), D ** -0.5),
        'na_rel_bias': nrm(8, (DEPTH, NA_HEADS, 2 * NA_KH_MAX - 1, 2 * NA_KW - 1), 0.3),
        'mla_g_q': gain(9, (DEPTH, MLA_Q_RANK)),
        'mla_w_qup': nrm(10, (DEPTH, MLA_Q_RANK, MLA_HEADS * (MLA_NOPE + MLA_ROPE)), MLA_Q_RANK ** -0.5),
        'mla_g_kv': gain(11, (DEPTH, MLA_KV_RANK)),
        'mla_w_kvup': nrm(12, (DEPTH, MLA_KV_RANK, MLA_HEADS * (MLA_NOPE + MLA_V)), MLA_KV_RANK ** -0.5),
        'gqa_g_q': gain(13, (DEPTH, GQA_HEAD_DIM)),
        'gqa_g_k': gain(14, (DEPTH, GQA_HEAD_DIM)),
        'diff_lq1': nrm(15, (DEPTH, DIFF_QK_DIM), 0.1),
        'diff_lk1': nrm(16, (DEPTH, DIFF_QK_DIM), 0.1),
        'diff_lq2': nrm(17, (DEPTH, DIFF_QK_DIM), 0.1),
        'diff_lk2': nrm(18, (DEPTH, DIFF_QK_DIM), 0.1),
        'diff_g_sub': gain(19, (DEPTH, DIFF_V_DIM)),
        'w_out': nrm(20, (DEPTH, MIX_WIDTH, D), MIX_WIDTH ** -0.5),
        'g_ffn': gain(21, (DEPTH, D)),
        'ffn_w_gate': nrm(22, (n_dense, D, D_FF), D ** -0.5),
        'ffn_w_up': nrm(23, (n_dense, D, D_FF), D ** -0.5),
        'ffn_w_down': nrm(24, (n_dense, D_FF, D), D_FF ** -0.5),
        'moe_w_router': nrm(25, (n_moe, D, N_EXPERTS), D ** -0.5),
        'moe_b_router': nrm(26, (n_moe, N_EXPERTS), 0.01),
        'moe_w_gate': nrm(27, (n_moe, N_EXPERTS, D, D_FF_EXPERT), D ** -0.5),
        'moe_w_up': nrm(28, (n_moe, N_EXPERTS, D, D_FF_EXPERT), D ** -0.5),
        'moe_w_down': nrm(29, (n_moe, N_EXPERTS, D_FF_EXPERT, D), D_FF_EXPERT ** -0.5),
        'g_final': gain(30, (D,)),
    }


def reference(x, c, ctx, c_ctx, w_mod, b_mod, g_mix, w_in, na_rel_bias, mla_g_q, mla_w_qup, mla_g_kv, mla_w_kvup,
              gqa_g_q, gqa_g_k, diff_lq1, diff_lk1, diff_lq2, diff_lk2, diff_g_sub, w_out, g_ffn,
              ffn_w_gate, ffn_w_up, ffn_w_down, moe_w_router, moe_b_router, moe_w_gate, moe_w_up, moe_w_down,
              g_final):
    n = x.shape[1]
    rope64 = axial_angles(n, MLA_ROPE)
    rope128 = axial_angles(n, GQA_HEAD_DIM)
    s_c = jax.nn.silu(c)
    s_cc = jax.nn.silu(c_ctx)
    h, hc = x, ctx
    for l in range(DEPTH):
        last = l == DEPTH - 1
        mod = (s_c @ w_mod[l] + b_mod[l])[:, None, :]
        sh_a, sc_a, gt_a, sh_f, sc_f, gt_f = jnp.split(mod, 6, axis=-1)
        csh_a, csc_a, cgt_a, csh_f, csc_f, cgt_f = jnp.split(s_cc @ w_mod[l] + b_mod[l], 6, axis=-1)
        lp = {'w_in': w_in[l], 'w_out': w_out[l], 'na_rel_bias': na_rel_bias[l],
              'mla_g_q': mla_g_q[l], 'mla_w_qup': mla_w_qup[l], 'mla_g_kv': mla_g_kv[l], 'mla_w_kvup': mla_w_kvup[l],
              'gqa_g_q': gqa_g_q[l], 'gqa_g_k': gqa_g_k[l],
              'diff_lq1': diff_lq1[l], 'diff_lk1': diff_lk1[l], 'diff_lq2': diff_lq2[l], 'diff_lk2': diff_lk2[l],
              'diff_g_sub': diff_g_sub[l]}
        lam_init = 0.8 - 0.6 * math.exp(-0.3 * l)
        u = modulate(rmsnorm(h, g_mix[l]), sh_a, sc_a)
        uc = modulate(rmsnorm(hc, g_mix[l]), csh_a, csc_a)
        a, a_ctx = token_mixer(u, uc, lp, lam_init, rope64, rope128, not last)
        h = h + gt_a * a
        u = modulate(rmsnorm(h, g_ffn[l]), sh_f, sc_f)
        if l % 2 == 0:
            ffn = lambda z: swiglu(z, ffn_w_gate[l // 2], ffn_w_up[l // 2], ffn_w_down[l // 2])
        else:
            ffn = lambda z: moe_swiglu(z, moe_w_router[l // 2], moe_b_router[l // 2], moe_w_gate[l // 2],
                                       moe_w_up[l // 2], moe_w_down[l // 2])
        h = h + gt_f * ffn(u)
        if not last:
            hc = hc + cgt_a * a_ctx
            uc = modulate(rmsnorm(hc, g_ffn[l]), csh_f, csc_f)
            hc = hc + cgt_f * ffn(uc)
    return rmsnorm(h, g_final)
```

```python
import functools
import math

import numpy as np
import jax
import jax.numpy as jnp
from jax import lax
from jax.experimental import pallas as pl
from jax.experimental.pallas import tpu as pltpu

F32 = jnp.float32
BF16 = jnp.bfloat16

D_MODEL = 2048
BATCH = 8
SEQ = 2048
DEPTH = 2
GRID_W = 64
GRID_H = SEQ // GRID_W
CTX_LEN = 256
ROPE_THETA = 10000.0
EPS = 1e-6
NEG_INF = -1e30

NA_HEADS = 4
NA_KH = 8
NA_KW = 16
MLA_HEADS = 4
MLA_NOPE = 128
MLA_ROPE = 64
MLA_V = 128
MLA_Q_RANK = 384
MLA_KV_RANK = 128
GQA_HEADS = 4
GQA_KV_HEADS = 2
DIFF_HEADS = 4
DIFF_QK = 64
HEAD = 128

NA_SCALE = HEAD ** -0.5
MLA_SCALE = (MLA_NOPE + MLA_ROPE) ** -0.5
GQA_SCALE = HEAD ** -0.5
DIFF_SCALE = DIFF_QK ** -0.5

D_FF = 5632
N_EXPERTS = 8
D_FF_EXPERT = 7168

T_LAT = BATCH * SEQ
T_CTX = BATCH * CTX_LEN
ROWS = T_LAT + T_CTX

COL_NA_Q, COL_NA_K, COL_NA_V = 0, 512, 1024
COL_MLA_C = 1536
COL_GQA_Q, COL_GQA_K, COL_GQA_V = 2048, 2560, 2816
COL_DIFF_Q, COL_DIFF_K, COL_DIFF_V = 3072, 3584, 4096
COL_KPE = 4608
IN_COLS_PAD = 5120

V7X_VMEM_LIMIT = 56 * 1024 * 1024

TM = 1024
N_LAT_TILES = T_LAT // TM
N_ALL_TILES = ROWS // TM
TILES_PER_BATCH = SEQ // TM
TQ = 256
NQ = SEQ // TQ
TP = 512
TN_IN = 512
TN_OUT = 512
TF = 256
MOE_TILES = 2 * T_LAT // TM + N_EXPERTS
MOE_ROWS = MOE_TILES * TM
TC = 256
DMA_LAG = 16


def _cparams(sem, vmem=V7X_VMEM_LIMIT):
    return pltpu.CompilerParams(dimension_semantics=sem, vmem_limit_bytes=vmem)


def _mod_row(i, tm):
    return jnp.where(i < T_LAT // tm, i // (SEQ // tm), BATCH)


def _dot(a, b):
    return jnp.dot(a, b, preferred_element_type=F32)


def _dot_t(a, b):
    return lax.dot_general(a, b, (((1,), (1,)), ((), ())), preferred_element_type=F32)


def _rms(x):
    return x * lax.rsqrt(jnp.mean(x * x, axis=-1, keepdims=True) + EPS)


def _silu(x):
    return x * jax.nn.sigmoid(x)


def _swiglu_accumulate(x, wg_ref, wu_ref, wd_ref, out_ref, first):
    a = (_silu(_dot(x, wg_ref[...].astype(BF16))) * _dot(x, wu_ref[...].astype(BF16))).astype(BF16)
    for c in range(0, D_MODEL, 512):
        part = _dot(a, wd_ref[:, c:c + 512].astype(BF16))

        @pl.when(first)
        def _():
            out_ref[:, c:c + 512] = part

        @pl.when(jnp.logical_not(first))
        def _():
            out_ref[:, c:c + 512] += part


def _mod_kernel(c_ref, w_ref, b_ref, o_ref):
    s = _silu(c_ref[...]).astype(BF16)
    o_ref[0] = _dot(s, w_ref[0].astype(BF16)) + b_ref[0]


def _modulation(cvec, w_mod, b_mod):
    tn = 1024
    return pl.pallas_call(
        _mod_kernel,
        grid=(DEPTH, 6 * D_MODEL // tn),
        in_specs=[pl.BlockSpec((16, D_MODEL), lambda l, j: (0, 0)),
                  pl.BlockSpec((1, D_MODEL, tn), lambda l, j: (l, 0, j)),
                  pl.BlockSpec((1, 1, tn), lambda l, j: (l, 0, j))],
        out_specs=pl.BlockSpec((1, 16, tn), lambda l, j: (l, 0, j)),
        out_shape=jax.ShapeDtypeStruct((DEPTH, 16, 6 * D_MODEL), F32),
        compiler_params=_cparams(("arbitrary", "arbitrary")),
        name="modulation",
    )(cvec, w_mod, b_mod.reshape(DEPTH, 1, 6 * D_MODEL))


def _in_proj_kernel(h_ref, g_ref, sh_ref, sc_ref, w_ref, o_ref, u_scr):
    @pl.when(pl.program_id(1) == 0)
    def _():
        y = _rms(h_ref[...]) * g_ref[...]
        u_scr[...] = (y * (1.0 + sc_ref[0]) + sh_ref[0]).astype(BF16)

    o_ref[...] = _dot(u_scr[...], w_ref[...]).astype(o_ref.dtype)


def _in_proj(h, g, mod, w_in):
    return pl.pallas_call(
        _in_proj_kernel,
        grid=(N_ALL_TILES, IN_COLS_PAD // TN_IN),
        in_specs=[pl.BlockSpec((TM, D_MODEL), lambda i, j: (i, 0)),
                  pl.BlockSpec((1, D_MODEL), lambda i, j: (0, 0)),
                  pl.BlockSpec((1, 1, D_MODEL), lambda i, j: (_mod_row(i, TM), 0, 0)),
                  pl.BlockSpec((1, 1, D_MODEL), lambda i, j: (_mod_row(i, TM), 0, 1)),
                  pl.BlockSpec((D_MODEL, TN_IN), lambda i, j: (0, j))],
        out_specs=pl.BlockSpec((TM, TN_IN), lambda i, j: (i, j)),
        out_shape=jax.ShapeDtypeStruct((ROWS, IN_COLS_PAD), BF16),
        scratch_shapes=[pltpu.VMEM((TM, D_MODEL), BF16)],
        compiler_params=_cparams(("arbitrary", "arbitrary")),
        name="in_proj",
    )(h, g, mod, mod, w_in)


def _rope(x, c, s, q):
    w = x.shape[-1]
    lane = lax.broadcasted_iota(jnp.int32, x.shape, 1)
    partner = jnp.where((lane & q) == 0, pltpu.roll(x, w - q, 1), pltpu.roll(x, q, 1))
    return x * c + partner * s


def _prep_kernel(mc_ref, gq_ref, gk_ref, dq_ref, dk_ref, kpe_ref,
                 g_cq_ref, w_qup_ref, g_ckv_ref, w_kvk_ref, w_kvv_ref, g_gq_ref, g_gk_ref,
                 c64_ref, s64_ref, c128_ref, s128_ref,
                 qm_ref, km_ref, vm_ref, qg_ref, kg_ref, qd_ref, kd_ref):
    c64, s64 = c64_ref[...], s64_ref[...]
    c128, s128 = c128_ref[...], s128_ref[...]
    q64, q128 = MLA_ROPE // 4, HEAD // 4

    mc = mc_ref[...].astype(F32)
    cq = (_rms(mc[:, :MLA_Q_RANK]) * g_cq_ref[...]).astype(BF16)
    ckv = (_rms(mc[:, MLA_Q_RANK:]) * g_ckv_ref[...]).astype(BF16)
    q = _dot(cq, w_qup_ref[...])
    kn = _dot(ckv, w_kvk_ref[...])
    vm_ref[...] = _dot(ckv, w_kvv_ref[...]).astype(BF16)
    kpe = _rope(kpe_ref[...].astype(F32), c64, s64, q64).astype(BF16)
    for h in range(MLA_HEADS):
        lo = 2 * HEAD * h
        qm_ref[:, lo:lo + HEAD] = (q[:, lo:lo + HEAD] * MLA_SCALE).astype(BF16)
        pe = _rope(q[:, lo + HEAD:lo + 2 * HEAD], c64, s64, q64)
        qm_ref[:, lo + HEAD:lo + 2 * HEAD] = (pe * MLA_SCALE).astype(BF16)
        km_ref[:, lo:lo + HEAD] = kn[:, h * HEAD:(h + 1) * HEAD].astype(BF16)
        km_ref[:, lo + HEAD:lo + 2 * HEAD] = kpe

    for h in range(GQA_HEADS):
        sl = slice(h * HEAD, (h + 1) * HEAD)
        z = _rms(gq_ref[:, sl].astype(F32)) * g_gq_ref[...]
        qg_ref[:, sl] = (_rope(z, c128, s128, q128) * GQA_SCALE).astype(BF16)
    for h in range(GQA_KV_HEADS):
        sl = slice(h * HEAD, (h + 1) * HEAD)
        z = _rms(gk_ref[:, sl].astype(F32)) * g_gk_ref[...]
        kg_ref[:, sl] = _rope(z, c128, s128, q128).astype(BF16)

    for h in range(DIFF_HEADS):
        sl = slice(h * HEAD, (h + 1) * HEAD)
        qd_ref[:, sl] = (_rope(dq_ref[:, sl].astype(F32), c64, s64, q64) * DIFF_SCALE).astype(BF16)
        kd_ref[:, sl] = _rope(dk_ref[:, sl].astype(F32), c64, s64, q64).astype(BF16)


def _prep(proj, lw, tabs):
    n_tiles = ROWS // TP
    lat_tiles = T_LAT // TP
    per_batch = SEQ // TP

    def col(width, start):
        return pl.BlockSpec((TP, width), lambda i: (i, start // width))

    def full(shape):
        return pl.BlockSpec(shape, lambda i: (0,) * len(shape))

    def tab():
        return pl.BlockSpec((TP, HEAD), lambda i: (jnp.where(i < lat_tiles, i % per_batch, per_batch), 0))

    def out(width):
        return pl.BlockSpec((TP, width), lambda i: (i, 0))

    widths = (1024, 1024, 512, 512, 256, 512, 512)
    return pl.pallas_call(
        _prep_kernel,
        grid=(n_tiles,),
        in_specs=[col(512, COL_MLA_C), col(512, COL_GQA_Q), col(256, COL_GQA_K),
                  col(512, COL_DIFF_Q), col(512, COL_DIFF_K), col(128, COL_KPE),
                  full((1, MLA_Q_RANK)), full((MLA_Q_RANK, 1024)), full((1, MLA_KV_RANK)),
                  full((MLA_KV_RANK, 512)), full((MLA_KV_RANK, 512)), full((1, HEAD)), full((1, HEAD)),
                  tab(), tab(), tab(), tab()],
        out_specs=[out(w) for w in widths],
        out_shape=[jax.ShapeDtypeStruct((ROWS, w), BF16) for w in widths],
        compiler_params=_cparams(("arbitrary",)),
        name="prep",
    )(proj, proj, proj, proj, proj, proj,
      lw["g_cq"], lw["w_qup"], lw["g_ckv"], lw["w_kvk"], lw["w_kvv"], lw["g_gq"], lw["g_gk"],
      tabs["c64"], tabs["s64"], tabs["c128"], tabs["s128"])


def _attend(s_c, vc, s_l=None, vl=None):
    m = jnp.max(s_c, axis=-1, keepdims=True)
    if s_l is not None:
        m = jnp.maximum(m, jnp.max(s_l, axis=-1, keepdims=True))
    p_c = jnp.exp(s_c - m)
    den = jnp.sum(p_c, axis=-1, keepdims=True)
    o = _dot(p_c.astype(BF16), vc)
    if s_l is not None:
        p_l = jnp.exp(s_l - m)
        den = den + jnp.sum(p_l, axis=-1, keepdims=True)
        o = o + _dot(p_l.astype(BF16), vl)
    return o / den


def _on_query_tiles(with_ctx, lat_fn, ctx_fn):
    if not with_ctx:
        lat_fn()
        return
    qt = pl.program_id(2)
    pl.when(qt < NQ)(lat_fn)
    pl.when(qt == NQ)(ctx_fn)


def _attn_kernel(q_ref, kc_ref, vc_ref, kl_ref, vl_ref, o_ref, *, with_ctx):
    q = q_ref[...]

    def lat():
        o_ref[...] = _attend(_dot_t(q, kc_ref[...]), vc_ref[...],
                             _dot_t(q, kl_ref[...]), vl_ref[...]).astype(o_ref.dtype)

    def ctx():
        o_ref[...] = _attend(_dot_t(q, kc_ref[...]), vc_ref[...]).astype(o_ref.dtype)

    _on_query_tiles(with_ctx, lat, ctx)


def _diff_kernel(q_ref, kc_ref, vc_ref, kl_ref, vl_ref, lam_ref, g_ref, o_ref, *, with_ctx, lam_init):
    q = q_ref[...]
    lane = lax.broadcasted_iota(jnp.int32, q.shape, 1)
    zero = jnp.zeros_like(q)
    q1 = jnp.where(lane < DIFF_QK, q, zero)
    q2 = jnp.where(lane >= DIFF_QK, q, zero)
    lv = lam_ref[...]
    lam = (jnp.exp(jnp.sum(lv[0:1] * lv[1:2], axis=-1, keepdims=True))
           - jnp.exp(jnp.sum(lv[2:3] * lv[3:4], axis=-1, keepdims=True)) + lam_init)

    def finish(o1, o2):
        o = o1 - lam * o2
        o_ref[...] = (_rms(o) * g_ref[...] * (1.0 - lam_init)).astype(o_ref.dtype)

    def lat():
        kc, vc, kl, vl = kc_ref[...], vc_ref[...], kl_ref[...], vl_ref[...]
        finish(_attend(_dot_t(q1, kc), vc, _dot_t(q1, kl), vl),
               _attend(_dot_t(q2, kc), vc, _dot_t(q2, kl), vl))

    def ctx():
        kc, vc = kc_ref[...], vc_ref[...]
        finish(_attend(_dot_t(q1, kc), vc), _attend(_dot_t(q2, kc), vc))

    _on_query_tiles(with_ctx, lat, ctx)


def _q_row_block(b, qt):
    return jnp.where(qt < NQ, b * NQ + qt, T_LAT // TQ + b)


def _global_attention(q, q_w, k, k_w, k_col, v, v_col, heads_per_kv, with_ctx, diff=None):
    n_heads = 4
    n_qt = NQ + 1 if with_ctx else NQ
    out_rows = ROWS if with_ctx else T_LAT
    ctx_blk = T_LAT // CTX_LEN

    in_specs = [
        pl.BlockSpec((TQ, q_w), lambda b, h, t: (_q_row_block(b, t), h)),
        pl.BlockSpec((CTX_LEN, k_w), lambda b, h, t: (ctx_blk + b, k_col + h // heads_per_kv)),
        pl.BlockSpec((CTX_LEN, HEAD), lambda b, h, t: (ctx_blk + b, v_col + h // heads_per_kv)),
        pl.BlockSpec((SEQ, k_w), lambda b, h, t: (b, k_col + h // heads_per_kv)),
        pl.BlockSpec((SEQ, HEAD), lambda b, h, t: (b, v_col + h // heads_per_kv)),
    ]
    args = [q, k, v, k, v]
    if diff is None:
        body = functools.partial(_attn_kernel, with_ctx=with_ctx)
    else:
        lam_vecs, g_sub, lam_init = diff
        body = functools.partial(_diff_kernel, with_ctx=with_ctx, lam_init=lam_init)
        in_specs += [pl.BlockSpec((4, DIFF_QK), lambda b, h, t: (0, 0)),
                     pl.BlockSpec((1, HEAD), lambda b, h, t: (0, 0))]
        args += [lam_vecs, g_sub]
    return pl.pallas_call(
        body,
        grid=(BATCH, n_heads, n_qt),
        in_specs=in_specs,
        out_specs=pl.BlockSpec((TQ, HEAD), lambda b, h, t: (_q_row_block(b, t), h)),
        out_shape=jax.ShapeDtypeStruct((out_rows, n_heads * HEAD), BF16),
        compiler_params=_cparams(("arbitrary",) * 3),
        name="diff_attention" if diff is not None else "global_attention",
    )(*args)


NA_QROWS = TQ // GRID_W
NA_WROWS = 12
NA_WIN = NA_WROWS * GRID_W
NA_MASKED = 2 * NA_KH - 1


def _na_kernel(q_ref, kc_ref, vc_ref, k_ref, v_ref, bias_ref, o_ref, bias_scr, *, with_ctx):
    q = q_ref[...]

    def lat():
        r0 = pl.program_id(2) * NA_QROWS
        ws = jnp.clip(r0 - NA_KH // 2, 0, GRID_H - NA_WROWS)
        for i in range(NA_QROWS):
            qr = r0 + i
            rs = jnp.clip(qr - NA_KH // 2, 0, GRID_H - NA_KH)
            for j in range(NA_WROWS):
                kr = ws + j
                a = jnp.where((kr >= rs) & (kr < rs + NA_KH), kr - qr + NA_KH - 1, NA_MASKED)
                half = (j % 2) * GRID_W
                bias_scr[i * GRID_W:(i + 1) * GRID_W, j * GRID_W:(j + 1) * GRID_W] = (
                    bias_ref[a][:, half:half + GRID_W])
        start = pl.multiple_of(ws * GRID_W, TQ)
        kw = k_ref[pl.ds(start, NA_WIN), :]
        vw = v_ref[pl.ds(start, NA_WIN), :]
        s_l = _dot_t(q, kw) * NA_SCALE + bias_scr[...]
        s_c = _dot_t(q, kc_ref[...]) * NA_SCALE
        o_ref[...] = _attend(s_c, vc_ref[...], s_l, vw).astype(o_ref.dtype)

    def ctx():
        s_c = _dot_t(q, kc_ref[...]) * NA_SCALE
        o_ref[...] = _attend(s_c, vc_ref[...]).astype(o_ref.dtype)

    _on_query_tiles(with_ctx, lat, ctx)


def _na_attention(proj, bias_tab, with_ctx):
    n_qt = NQ + 1 if with_ctx else NQ
    out_rows = ROWS if with_ctx else T_LAT
    ctx_blk = T_LAT // CTX_LEN
    kcol, vcol = COL_NA_K // HEAD, COL_NA_V // HEAD
    return pl.pallas_call(
        functools.partial(_na_kernel, with_ctx=with_ctx),
        grid=(BATCH, NA_HEADS, n_qt),
        in_specs=[pl.BlockSpec((TQ, HEAD), lambda b, h, t: (_q_row_block(b, t), h)),
                  pl.BlockSpec((CTX_LEN, HEAD), lambda b, h, t: (ctx_blk + b, kcol + h)),
                  pl.BlockSpec((CTX_LEN, HEAD), lambda b, h, t: (ctx_blk + b, vcol + h)),
                  pl.BlockSpec((SEQ, HEAD), lambda b, h, t: (b, kcol + h)),
                  pl.BlockSpec((SEQ, HEAD), lambda b, h, t: (b, vcol + h)),
                  pl.BlockSpec((None, 2 * NA_KH, GRID_W, HEAD), lambda b, h, t: (h, 0, 0, 0))],
        out_specs=pl.BlockSpec((TQ, HEAD), lambda b, h, t: (_q_row_block(b, t), h)),
        out_shape=jax.ShapeDtypeStruct((out_rows, NA_HEADS * HEAD), BF16),
        scratch_shapes=[pltpu.VMEM((TQ, NA_WIN), F32)],
        compiler_params=_cparams(("arbitrary",) * 3),
        name="na_attention",
    )(proj, proj, proj, proj, proj, bias_tab)


def _out_proj_kernel(o0_ref, o1_ref, o2_ref, o3_ref, w_ref, h_ref, gt_ref, out_ref):
    acc = None
    for t, o_ref in enumerate((o0_ref, o1_ref, o2_ref, o3_ref)):
        w = w_ref[t * 512:(t + 1) * 512, :].astype(BF16)
        part = _dot(o_ref[...], w)
        acc = part if acc is None else acc + part
    out_ref[...] = h_ref[...] + gt_ref[0] * acc


def _out_proj(o_parts, w_out, h, mod, n_tiles):
    tn = TN_OUT
    o_spec = pl.BlockSpec((TM, 512), lambda i, j: (i, 0))
    return pl.pallas_call(
        _out_proj_kernel,
        grid=(n_tiles, D_MODEL // tn),
        in_specs=[o_spec, o_spec, o_spec, o_spec,
                  pl.BlockSpec((D_MODEL, tn), lambda i, j: (0, j)),
                  pl.BlockSpec((TM, tn), lambda i, j: (i, j)),
                  pl.BlockSpec((1, 1, tn), lambda i, j: (_mod_row(i, TM), 0, 2 * (D_MODEL // tn) + j))],
        out_specs=pl.BlockSpec((TM, tn), lambda i, j: (i, j)),
        out_shape=jax.ShapeDtypeStruct((n_tiles * TM, D_MODEL), F32),
        compiler_params=_cparams(("arbitrary", "arbitrary")),
        name="out_proj",
    )(*o_parts, w_out, h, mod)


def _ffn_kernel(h_ref, g_ref, sh_ref, sc_ref, gt_ref, wg_ref, wu_ref, wd_ref, out_ref, u_scr):
    j = pl.program_id(1)

    @pl.when(j == 0)
    def _():
        y = _rms(h_ref[...]) * g_ref[...]
        u_scr[...] = (y * (1.0 + sc_ref[0]) + sh_ref[0]).astype(BF16)

    _swiglu_accumulate(u_scr[...], wg_ref, wu_ref, wd_ref, out_ref, j == 0)

    @pl.when(j == pl.num_programs(1) - 1)
    def _():
        out_ref[...] = h_ref[...] + gt_ref[0] * out_ref[...]


def _ffn_dense(h, g, mod, w_gate, w_up, w_down):
    def mod_spec(chunk):
        return pl.BlockSpec((1, 1, D_MODEL), lambda i, j: (_mod_row(i, TM), 0, chunk))

    return pl.pallas_call(
        _ffn_kernel,
        grid=(N_ALL_TILES, D_FF // TF),
        in_specs=[pl.BlockSpec((TM, D_MODEL), lambda i, j: (i, 0), pipeline_mode=pl.Buffered(1)),
                  pl.BlockSpec((1, D_MODEL), lambda i, j: (0, 0)),
                  mod_spec(3), mod_spec(4), mod_spec(5),
                  pl.BlockSpec((D_MODEL, TF), lambda i, j: (0, j)),
                  pl.BlockSpec((D_MODEL, TF), lambda i, j: (0, j)),
                  pl.BlockSpec((TF, D_MODEL), lambda i, j: (j, 0))],
        out_specs=pl.BlockSpec((TM, D_MODEL), lambda i, j: (i, 0)),
        out_shape=jax.ShapeDtypeStruct((ROWS, D_MODEL), F32),
        scratch_shapes=[pltpu.VMEM((TM, D_MODEL), BF16)],
        compiler_params=_cparams(("arbitrary", "arbitrary")),
        name="ffn_dense",
    )(h, g, mod, mod, mod, w_gate, w_up, w_down)


META_LANES = 128


def _route_kernel(h_ref, g_ref, sh_ref, sc_ref, wr_ref, br_ref, u_ref, meta_ref, cnt_ref, carry):
    i = pl.program_id(0)

    @pl.when(i == 0)
    def _():
        carry[...] = jnp.zeros_like(carry)

    y = _rms(h_ref[...]) * g_ref[...]
    u = y * (1.0 + sc_ref[0]) + sh_ref[0]
    u_ref[...] = u

    w = wr_ref[...]
    u_hi = u.astype(BF16)
    u_lo = (u - u_hi.astype(F32)).astype(BF16)
    w_hi = w.astype(BF16)
    w_lo = (w - w_hi.astype(F32)).astype(BF16)
    logits = _dot(u_hi, w_hi) + (_dot(u_hi, w_lo) + _dot(u_lo, w_hi)) + br_ref[...]

    lane = lax.broadcasted_iota(jnp.int32, logits.shape, 1)
    logits = jnp.where(lane < N_EXPERTS, logits, NEG_INF)
    lane_f = lane.astype(F32)
    m1 = jnp.max(logits, axis=-1, keepdims=True)
    i1 = jnp.min(jnp.where(logits == m1, lane_f, float(META_LANES)), axis=-1, keepdims=True)
    pick1 = lane_f == i1
    rest = jnp.where(pick1, NEG_INF, logits)
    m2 = jnp.max(rest, axis=-1, keepdims=True)
    i2 = jnp.min(jnp.where(rest == m2, lane_f, float(META_LANES)), axis=-1, keepdims=True)
    pick2 = lane_f == i2
    e = jnp.exp(m2 - m1)
    g1 = 1.0 / (1.0 + e)
    g2 = e / (1.0 + e)

    onehot = jnp.where(pick1 | pick2, 1.0, 0.0)
    rr = lax.broadcasted_iota(jnp.int32, (TM, TM), 0)
    cc = lax.broadcasted_iota(jnp.int32, (TM, TM), 1)
    earlier = jnp.where(cc < rr, 1.0, 0.0).astype(BF16)
    before = _dot(earlier, onehot.astype(BF16)) + carry[...]
    r1 = jnp.sum(jnp.where(pick1, before, 0.0), axis=-1, keepdims=True)
    r2 = jnp.sum(jnp.where(pick2, before, 0.0), axis=-1, keepdims=True)
    carry[...] += jnp.sum(onehot, axis=0, keepdims=True)

    meta = jnp.zeros(logits.shape, F32)
    for k, val in enumerate((i1, i2, r1, r2, g1, g2)):
        meta = jnp.where(lane == k, val, meta)
    meta_ref[...] = meta
    cnt_ref[...] = jnp.broadcast_to(carry[...], cnt_ref.shape)


def _moe_route(h, g, mod, w_router, b_router):
    def mod_spec(chunk):
        return pl.BlockSpec((1, 1, D_MODEL), lambda i: (_mod_row(i, TM), 0, chunk))

    return pl.pallas_call(
        _route_kernel,
        grid=(N_LAT_TILES,),
        in_specs=[pl.BlockSpec((TM, D_MODEL), lambda i: (i, 0)),
                  pl.BlockSpec((1, D_MODEL), lambda i: (0, 0)),
                  mod_spec(3), mod_spec(4),
                  pl.BlockSpec((D_MODEL, META_LANES), lambda i: (0, 0)),
                  pl.BlockSpec((1, META_LANES), lambda i: (0, 0))],
        out_specs=[pl.BlockSpec((TM, D_MODEL), lambda i: (i, 0)),
                   pl.BlockSpec((TM, META_LANES), lambda i: (i, 0)),
                   pl.BlockSpec((8, META_LANES), lambda i: (0, 0))],
        out_shape=[jax.ShapeDtypeStruct((T_LAT, D_MODEL), F32),
                   jax.ShapeDtypeStruct((T_LAT, META_LANES), F32),
                   jax.ShapeDtypeStruct((8, META_LANES), F32)],
        scratch_shapes=[pltpu.VMEM((1, META_LANES), F32)],
        compiler_params=_cparams(("arbitrary",)),
        name="moe_route",
    )(h, g, mod, mod, w_router, b_router)


def _row_copy(src, s, dst, d, sem):
    return pltpu.make_async_copy(src.at[pl.ds(s, 1)], dst.at[pl.ds(d, 1)], sem)


def _scatter_kernel(pos_ref, valid_ref, u_ref, z_ref, xs_ref, sem):
    tok_sem, pad_sem = sem.at[0], sem.at[1]

    def tok(t, c):
        _row_copy(u_ref, t, xs_ref, pos_ref[t], tok_sem).start()
        _row_copy(u_ref, t, xs_ref, pos_ref[T_LAT + t], tok_sem).start()

        @pl.when(t >= DMA_LAG)
        def _():
            _row_copy(u_ref, 0, xs_ref, 0, tok_sem).wait()
            _row_copy(u_ref, 0, xs_ref, 0, tok_sem).wait()
        return c

    lax.fori_loop(0, T_LAT, tok, 0)

    def drain_tok(t, c):
        _row_copy(u_ref, 0, xs_ref, 0, tok_sem).wait()
        return c

    lax.fori_loop(0, 2 * DMA_LAG, drain_tok, 0)

    def tile(i, c):
        first = valid_ref[i]

        def pad(r, c2):
            _row_copy(z_ref, 0, xs_ref, i * TM + r, pad_sem).start()

            @pl.when(r >= first + DMA_LAG)
            def _():
                _row_copy(z_ref, 0, xs_ref, 0, pad_sem).wait()
            return c2

        lax.fori_loop(first, TM, pad, 0)

        def drain_pad(r, c2):
            _row_copy(z_ref, 0, xs_ref, 0, pad_sem).wait()
            return c2

        lax.fori_loop(0, jnp.minimum(DMA_LAG, TM - first), drain_pad, 0)
        return c

    lax.fori_loop(0, MOE_TILES, tile, 0)


def _moe_scatter(u, pos, valid):
    zero_row = jnp.zeros((1, D_MODEL), F32)
    grid_spec = pltpu.PrefetchScalarGridSpec(
        num_scalar_prefetch=2,
        grid=(1,),
        in_specs=[pl.BlockSpec(memory_space=pl.ANY), pl.BlockSpec(memory_space=pl.ANY)],
        out_specs=pl.BlockSpec(memory_space=pl.ANY),
        scratch_shapes=[pltpu.SemaphoreType.DMA((2,))],
    )
    return pl.pallas_call(
        _scatter_kernel,
        grid_spec=grid_spec,
        out_shape=jax.ShapeDtypeStruct((MOE_ROWS, D_MODEL), F32),
        compiler_params=_cparams(("arbitrary",)),
        name="moe_scatter",
    )(pos, valid, u, zero_row)


def _moe_ffn_kernel(te_ref, nu_ref, x_ref, wg_ref, wu_ref, wd_ref, out_ref, xb_scr):
    i, j = pl.program_id(0), pl.program_id(1)
    used = i < nu_ref[0]

    @pl.when(used & (j == 0))
    def _():
        xb_scr[...] = x_ref[...].astype(BF16)

    @pl.when(used)
    def _():
        _swiglu_accumulate(xb_scr[...], wg_ref, wu_ref, wd_ref, out_ref, j == 0)

    @pl.when(jnp.logical_not(used) & (j == 0))
    def _():
        out_ref[...] = jnp.zeros_like(out_ref)


def _moe_ffn(xs, tile_expert, n_used, w_gate, w_up, w_down):
    nj = D_FF_EXPERT // TF

    def jj(i, j, nu):
        return jnp.where(i < nu[0], j, nj - 1)

    grid_spec = pltpu.PrefetchScalarGridSpec(
        num_scalar_prefetch=2,
        grid=(MOE_TILES, nj),
        in_specs=[pl.BlockSpec((TM, D_MODEL), lambda i, j, te, nu: (jnp.minimum(i, nu[0] - 1), 0),
                               pipeline_mode=pl.Buffered(1)),
                  pl.BlockSpec((None, D_MODEL, TF), lambda i, j, te, nu: (te[i], 0, jj(i, j, nu))),
                  pl.BlockSpec((None, D_MODEL, TF), lambda i, j, te, nu: (te[i], 0, jj(i, j, nu))),
                  pl.BlockSpec((None, TF, D_MODEL), lambda i, j, te, nu: (te[i], jj(i, j, nu), 0))],
        out_specs=pl.BlockSpec((TM, D_MODEL), lambda i, j, te, nu: (i, 0)),
        scratch_shapes=[pltpu.VMEM((TM, D_MODEL), BF16)],
    )
    return pl.pallas_call(
        _moe_ffn_kernel,
        grid_spec=grid_spec,
        out_shape=jax.ShapeDtypeStruct((MOE_ROWS, D_MODEL), F32),
        compiler_params=_cparams(("arbitrary", "arbitrary")),
        name="moe_ffn",
    )(tile_expert, n_used, xs, w_gate, w_up, w_down)


def _combine_kernel(pos_ref, h_ref, meta_ref, gt_ref, gf_ref, ys_ref, out_ref, y1, y2, sem):
    t0 = pl.program_id(0) * TC

    def start(r, c):
        _row_copy(ys_ref, pos_ref[t0 + r], y1, r, sem.at[0]).start()
        _row_copy(ys_ref, pos_ref[T_LAT + t0 + r], y2, r, sem.at[1]).start()
        return c

    lax.fori_loop(0, TC, start, 0)

    def wait(r, c):
        _row_copy(ys_ref, 0, y1, 0, sem.at[0]).wait()
        _row_copy(ys_ref, 0, y2, 0, sem.at[1]).wait()
        return c

    lax.fori_loop(0, TC, wait, 0)

    meta = meta_ref[...]
    y = meta[:, 4:5] * y1[...] + meta[:, 5:6] * y2[...]
    hn = h_ref[...] + gt_ref[0] * y
    out_ref[...] = _rms(hn) * gf_ref[...]


def _moe_combine(pos, h, meta, mod, g_final, ys):
    grid_spec = pltpu.PrefetchScalarGridSpec(
        num_scalar_prefetch=1,
        grid=(T_LAT // TC,),
        in_specs=[pl.BlockSpec((TC, D_MODEL), lambda i, p: (i, 0)),
                  pl.BlockSpec((TC, META_LANES), lambda i, p: (i, 0)),
                  pl.BlockSpec((1, 1, D_MODEL), lambda i, p: (i // (SEQ // TC), 0, 5)),
                  pl.BlockSpec((1, D_MODEL), lambda i, p: (0, 0)),
                  pl.BlockSpec(memory_space=pl.ANY)],
        out_specs=pl.BlockSpec((TC, D_MODEL), lambda i, p: (i, 0)),
        scratch_shapes=[pltpu.VMEM((TC, D_MODEL), F32), pltpu.VMEM((TC, D_MODEL), F32),
                        pltpu.SemaphoreType.DMA((2,))],
    )
    return pl.pallas_call(
        _combine_kernel,
        grid_spec=grid_spec,
        out_shape=jax.ShapeDtypeStruct((T_LAT, D_MODEL), F32),
        compiler_params=_cparams(("arbitrary",)),
        name="moe_combine",
    )(pos, h, meta, mod, g_final, ys)


def _moe_plan(meta, counts_f):
    i1 = meta[:, 0].astype(jnp.int32)
    i2 = meta[:, 1].astype(jnp.int32)
    r1 = meta[:, 2].astype(jnp.int32)
    r2 = meta[:, 3].astype(jnp.int32)
    counts = counts_f[0, :N_EXPERTS].astype(jnp.int32)
    padded = (counts + TM - 1) // TM * TM
    ends = jnp.cumsum(padded)
    starts = ends - padded
    pos = jnp.concatenate([starts[i1] + r1, starts[i2] + r2])
    n_used = ends[-1] // TM
    tile_start = jnp.arange(MOE_TILES, dtype=jnp.int32) * TM
    expert = jnp.minimum(jnp.sum(tile_start[:, None] >= ends[None, :], axis=1), N_EXPERTS - 1)
    valid = jnp.clip((starts + counts)[expert] - tile_start, 0, TM)
    valid = jnp.where(tile_start < ends[-1], valid, 0).astype(jnp.int32)
    expert = expert[jnp.minimum(jnp.arange(MOE_TILES), n_used - 1)].astype(jnp.int32)
    return pos.astype(jnp.int32), valid, expert, n_used.reshape(1).astype(jnp.int32)


def _rope_tables():
    t = jnp.arange(SEQ, dtype=jnp.int32)
    row = (t // GRID_W).astype(F32)
    col = (t % GRID_W).astype(F32)
    out = {}
    for dim, name in ((MLA_ROPE, "64"), (HEAD, "128")):
        half = dim // 2
        freqs = ROPE_THETA ** (-jnp.arange(0, half, 2, dtype=F32) / half)
        ar, ac = row[:, None] * freqs, col[:, None] * freqs
        c = jnp.concatenate([jnp.cos(ar), jnp.cos(ar), jnp.cos(ac), jnp.cos(ac)], axis=1)
        s = jnp.concatenate([-jnp.sin(ar), jnp.sin(ar), -jnp.sin(ac), jnp.sin(ac)], axis=1)
        c = jnp.tile(c, (1, HEAD // dim))
        s = jnp.tile(s, (1, HEAD // dim))
        out["c" + name] = jnp.concatenate([c, jnp.ones((TP, HEAD), F32)], axis=0)
        out["s" + name] = jnp.concatenate([s, jnp.zeros((TP, HEAD), F32)], axis=0)
    return out


def _na_bias_table(rel_bias):
    qc = np.arange(GRID_W)[:, None]
    kc = np.arange(GRID_W)[None, :]
    cs = np.clip(qc - NA_KW // 2, 0, GRID_W - NA_KW)
    col_ok = (kc >= cs) & (kc < cs + NA_KW)
    col_off = np.clip(kc - qc + NA_KW - 1, 0, 2 * NA_KW - 2)
    tab = jnp.where(jnp.asarray(col_ok), rel_bias[:, :, col_off], NEG_INF)
    tab = jnp.concatenate([tab, jnp.full((NA_HEADS, 1, GRID_W, GRID_W), NEG_INF, F32)], axis=1)
    return jnp.concatenate([tab, tab], axis=-1)


def _layer_weights(l, w_in, mla_g_q, mla_w_qup, mla_g_kv, mla_w_kvup, gqa_g_q, gqa_g_k):
    sizes = (512, 512, 512, 384, 128, 64, 512, 256, 256, 512, 512, 512)
    parts = jnp.split(w_in[l], np.cumsum(sizes)[:-1].tolist(), axis=1)
    order = (0, 1, 2, 3, 4, 6, 7, 8, 9, 10, 11, 5)
    pad = jnp.zeros((D_MODEL, IN_COLS_PAD - sum(sizes)), F32)
    w_in_p = jnp.concatenate([parts[k] for k in order] + [pad], axis=1).astype(BF16)

    qup = mla_w_qup[l].reshape(MLA_Q_RANK, MLA_HEADS, MLA_NOPE + MLA_ROPE)
    qup = jnp.pad(qup, ((0, 0), (0, 0), (0, 2 * HEAD - MLA_NOPE - MLA_ROPE)))
    kvup = mla_w_kvup[l].reshape(MLA_KV_RANK, MLA_HEADS, MLA_NOPE + MLA_V)
    return {
        "w_in": w_in_p,
        "g_cq": mla_g_q[l].reshape(1, -1),
        "w_qup": qup.reshape(MLA_Q_RANK, MLA_HEADS * 2 * HEAD).astype(BF16),
        "g_ckv": mla_g_kv[l].reshape(1, -1),
        "w_kvk": kvup[:, :, :MLA_NOPE].reshape(MLA_KV_RANK, -1).astype(BF16),
        "w_kvv": kvup[:, :, MLA_NOPE:].reshape(MLA_KV_RANK, -1).astype(BF16),
        "g_gq": gqa_g_q[l].reshape(1, -1),
        "g_gk": gqa_g_k[l].reshape(1, -1),
    }


def kernel(x, c, ctx, c_ctx, w_mod, b_mod, g_mix, w_in, na_rel_bias, mla_g_q, mla_w_qup, mla_g_kv, mla_w_kvup,
           gqa_g_q, gqa_g_k, diff_lq1, diff_lk1, diff_lq2, diff_lk2, diff_g_sub, w_out, g_ffn,
           ffn_w_gate, ffn_w_up, ffn_w_down, moe_w_router, moe_b_router, moe_w_gate, moe_w_up, moe_w_down,
           g_final):
    tabs = _rope_tables()
    cvec = jnp.concatenate([c, c_ctx[None, :], jnp.zeros((16 - BATCH - 1, D_MODEL), F32)], axis=0)
    mod_all = _modulation(cvec, w_mod, b_mod)
    h = jnp.concatenate([x.reshape(T_LAT, D_MODEL), ctx.reshape(T_CTX, D_MODEL)], axis=0)

    out = None
    for l in range(DEPTH):
        last = l == DEPTH - 1
        with_ctx = not last
        lam_init = 0.8 - 0.6 * math.exp(-0.3 * l)
        mod = mod_all[l].reshape(16, 1, 6 * D_MODEL)
        lw = _layer_weights(l, w_in, mla_g_q, mla_w_qup, mla_g_kv, mla_w_kvup, gqa_g_q, gqa_g_k)

        proj = _in_proj(h, g_mix[l].reshape(1, -1), mod, lw["w_in"])
        q_mla, k_mla, v_mla, q_gqa, k_gqa, q_diff, k_diff = _prep(proj, lw, tabs)

        o_na = _na_attention(proj, _na_bias_table(na_rel_bias[l]), with_ctx)
        o_mla = _global_attention(q_mla, 2 * HEAD, k_mla, 2 * HEAD, 0, v_mla, 0, 1, with_ctx)
        o_gqa = _global_attention(q_gqa, HEAD, k_gqa, HEAD, 0, proj, COL_GQA_V // HEAD,
                                  GQA_HEADS // GQA_KV_HEADS, with_ctx)
        lam_vecs = jnp.stack([diff_lq1[l], diff_lk1[l], diff_lq2[l], diff_lk2[l]])
        o_diff = _global_attention(q_diff, HEAD, k_diff, HEAD, 0, proj, COL_DIFF_V // HEAD, 1, with_ctx,
                                   diff=(lam_vecs, diff_g_sub[l].reshape(1, -1), lam_init))

        n_tiles = N_ALL_TILES if with_ctx else N_LAT_TILES
        h = _out_proj((o_na, o_mla, o_gqa, o_diff), w_out[l], h, mod, n_tiles)

        if l % 2 == 0:
            h = _ffn_dense(h, g_ffn[l].reshape(1, -1), mod, ffn_w_gate[l // 2], ffn_w_up[l // 2],
                           ffn_w_down[l // 2])
        else:
            m = l // 2
            w_r = jnp.pad(moe_w_router[m], ((0, 0), (0, META_LANES - N_EXPERTS)))
            b_r = jnp.pad(moe_b_router[m], (0, META_LANES - N_EXPERTS)).reshape(1, -1)
            u, meta, counts = _moe_route(h, g_ffn[l].reshape(1, -1), mod, w_r, b_r)
            pos, valid, tile_expert, n_used = _moe_plan(meta, counts)
            xs = _moe_scatter(u, pos, valid)
            ys = _moe_ffn(xs, tile_expert, n_used, moe_w_gate[m], moe_w_up[m], moe_w_down[m])
            out = _moe_combine(pos, h, meta, mod, g_final.reshape(1, -1), ys)
    return out.reshape(BATCH, SEQ, D_MODEL)
```

```python
import functools
import math

import numpy as np
import jax
import jax.numpy as jnp
from jax import lax
from jax.experimental import pallas as pl
from jax.experimental.pallas import tpu as pltpu

F32 = jnp.float32
BF16 = jnp.bfloat16

D_MODEL = 2048
BATCH = 8
SEQ = 2048
DEPTH = 2
GRID_W = 64
GRID_H = SEQ // GRID_W
CTX_LEN = 256
ROPE_THETA = 10000.0
EPS = 1e-6
NEG_INF = -1e30

NA_HEADS = 4
NA_KH = 8
NA_KW = 16
MLA_HEADS = 4
MLA_NOPE = 128
MLA_ROPE = 64
MLA_V = 128
MLA_Q_RANK = 384
MLA_KV_RANK = 128
GQA_HEADS = 4
GQA_KV_HEADS = 2
DIFF_HEADS = 4
DIFF_QK = 64
HEAD = 128

LOG2E = math.log2(math.e)
NA_SCALE = HEAD ** -0.5 * LOG2E
MLA_SCALE = (MLA_NOPE + MLA_ROPE) ** -0.5 * LOG2E
GQA_SCALE = HEAD ** -0.5 * LOG2E
DIFF_SCALE = DIFF_QK ** -0.5 * LOG2E

D_FF = 5632
N_EXPERTS = 8
D_FF_EXPERT = 7168

T_LAT = BATCH * SEQ
T_CTX = BATCH * CTX_LEN
ROWS = T_LAT + T_CTX

COL_NA_Q, COL_NA_K, COL_NA_V = 0, 512, 1024
COL_MLA_C = 1536
COL_GQA_Q, COL_GQA_K, COL_GQA_V = 2048, 2560, 2816
COL_DIFF_Q, COL_DIFF_K, COL_DIFF_V = 3072, 3584, 4096
COL_KPE = 4608
IN_COLS_PAD = 5120

V7X_VMEM_LIMIT = 56 * 1024 * 1024

TM = 1024
N_LAT_TILES = T_LAT // TM
N_ALL_TILES = ROWS // TM
TILES_PER_BATCH = SEQ // TM
TQ = 256
NQ = SEQ // TQ
TP = 512
TN_IN = 512
TN_OUT = 512
TF = 256
MOE_TILES = 2 * T_LAT // TM + N_EXPERTS
MOE_ROWS = MOE_TILES * TM
TC = 256


def _cparams(sem, vmem=V7X_VMEM_LIMIT):
    return pltpu.CompilerParams(dimension_semantics=sem, vmem_limit_bytes=vmem)


def _mod_row(i, tm):
    return jnp.where(i < T_LAT // tm, i // (SEQ // tm), BATCH)


def _dot(a, b):
    return jnp.dot(a, b, preferred_element_type=F32)


def _dot_t(a, b):
    return lax.dot_general(a, b, (((1,), (1,)), ((), ())), preferred_element_type=F32)


def _rms(x):
    return x * lax.rsqrt(jnp.mean(x * x, axis=-1, keepdims=True) + EPS)


def _silu(x):
    return x * jax.nn.sigmoid(x)


def _swiglu_accumulate(x, wg_ref, wu_ref, wd_ref, out_ref):
    a = (_silu(_dot(x, wg_ref[...].astype(BF16))) * _dot(x, wu_ref[...].astype(BF16))).astype(BF16)
    for c in range(0, D_MODEL, 512):
        out_ref[:, c:c + 512] += _dot(a, wd_ref[:, c:c + 512].astype(BF16))


def _mod_kernel(c_ref, w_ref, b_ref, o_ref):
    s = _silu(c_ref[...]).astype(BF16)
    o_ref[0] = _dot(s, w_ref[0].astype(BF16)) + b_ref[0]


def _modulation(cvec, w_mod, b_mod):
    tn = 1024
    return pl.pallas_call(
        _mod_kernel,
        grid=(DEPTH, 6 * D_MODEL // tn),
        in_specs=[pl.BlockSpec((16, D_MODEL), lambda l, j: (0, 0)),
                  pl.BlockSpec((1, D_MODEL, tn), lambda l, j: (l, 0, j)),
                  pl.BlockSpec((1, 1, tn), lambda l, j: (l, 0, j))],
        out_specs=pl.BlockSpec((1, 16, tn), lambda l, j: (l, 0, j)),
        out_shape=jax.ShapeDtypeStruct((DEPTH, 16, 6 * D_MODEL), F32),
        compiler_params=_cparams(("arbitrary", "arbitrary")),
        name="modulation",
    )(cvec, w_mod, b_mod.reshape(DEPTH, 1, 6 * D_MODEL))


def _in_proj_kernel(h_ref, g_ref, sh_ref, sc_ref, w_ref, o_ref, u_scr):
    @pl.when(pl.program_id(1) == 0)
    def _():
        y = _rms(h_ref[...]) * g_ref[...]
        u_scr[...] = (y * (1.0 + sc_ref[0]) + sh_ref[0]).astype(BF16)

    o_ref[...] = _dot(u_scr[...], w_ref[...]).astype(o_ref.dtype)


def _in_proj(h, g, mod, w_in):
    return pl.pallas_call(
        _in_proj_kernel,
        grid=(N_ALL_TILES, IN_COLS_PAD // TN_IN),
        in_specs=[pl.BlockSpec((TM, D_MODEL), lambda i, j: (i, 0)),
                  pl.BlockSpec((1, D_MODEL), lambda i, j: (0, 0)),
                  pl.BlockSpec((1, 1, D_MODEL), lambda i, j: (_mod_row(i, TM), 0, 0)),
                  pl.BlockSpec((1, 1, D_MODEL), lambda i, j: (_mod_row(i, TM), 0, 1)),
                  pl.BlockSpec((D_MODEL, TN_IN), lambda i, j: (0, j))],
        out_specs=pl.BlockSpec((TM, TN_IN), lambda i, j: (i, j)),
        out_shape=jax.ShapeDtypeStruct((ROWS, IN_COLS_PAD), BF16),
        scratch_shapes=[pltpu.VMEM((TM, D_MODEL), BF16)],
        compiler_params=_cparams(("arbitrary", "arbitrary")),
        name="in_proj",
    )(h, g, mod, mod, w_in)


def _rope(x, c, s, q):
    w = x.shape[-1]
    lane = lax.broadcasted_iota(jnp.int32, x.shape, 1)
    partner = jnp.where((lane & q) == 0, pltpu.roll(x, w - q, 1), pltpu.roll(x, q, 1))
    return x * c + partner * s


def _prep_kernel(mc_ref, gq_ref, gk_ref, dq_ref, dk_ref, kpe_ref,
                 g_cq_ref, w_qup_ref, g_ckv_ref, w_kvk_ref, w_kvv_ref, g_gq_ref, g_gk_ref,
                 c64_ref, s64_ref, c128_ref, s128_ref,
                 qm_ref, km_ref, vm_ref, qg_ref, kg_ref, qd_ref, kd_ref):
    c64, s64 = c64_ref[...], s64_ref[...]
    c128, s128 = c128_ref[...], s128_ref[...]
    q64, q128 = MLA_ROPE // 4, HEAD // 4

    mc = mc_ref[...].astype(F32)
    cq = (_rms(mc[:, :MLA_Q_RANK]) * g_cq_ref[...]).astype(BF16)
    ckv = (_rms(mc[:, MLA_Q_RANK:]) * g_ckv_ref[...]).astype(BF16)
    q = _dot(cq, w_qup_ref[...])
    kn = _dot(ckv, w_kvk_ref[...])
    vm_ref[...] = _dot(ckv, w_kvv_ref[...]).astype(BF16)
    kpe = _rope(kpe_ref[...].astype(F32), c64, s64, q64).astype(BF16)
    for h in range(MLA_HEADS):
        lo = 2 * HEAD * h
        qm_ref[:, lo:lo + HEAD] = (q[:, lo:lo + HEAD] * MLA_SCALE).astype(BF16)
        pe = _rope(q[:, lo + HEAD:lo + 2 * HEAD], c64, s64, q64)
        qm_ref[:, lo + HEAD:lo + 2 * HEAD] = (pe * MLA_SCALE).astype(BF16)
        km_ref[:, lo:lo + HEAD] = kn[:, h * HEAD:(h + 1) * HEAD].astype(BF16)
        km_ref[:, lo + HEAD:lo + 2 * HEAD] = kpe

    for h in range(GQA_HEADS):
        sl = slice(h * HEAD, (h + 1) * HEAD)
        z = _rms(gq_ref[:, sl].astype(F32)) * g_gq_ref[...]
        qg_ref[:, sl] = (_rope(z, c128, s128, q128) * GQA_SCALE).astype(BF16)
    for h in range(GQA_KV_HEADS):
        sl = slice(h * HEAD, (h + 1) * HEAD)
        z = _rms(gk_ref[:, sl].astype(F32)) * g_gk_ref[...]
        kg_ref[:, sl] = _rope(z, c128, s128, q128).astype(BF16)

    for h in range(DIFF_HEADS):
        sl = slice(h * HEAD, (h + 1) * HEAD)
        qd_ref[:, sl] = (_rope(dq_ref[:, sl].astype(F32), c64, s64, q64) * DIFF_SCALE).astype(BF16)
        kd_ref[:, sl] = _rope(dk_ref[:, sl].astype(F32), c64, s64, q64).astype(BF16)


def _prep(proj, lw, tabs):
    n_tiles = ROWS // TP
    lat_tiles = T_LAT // TP
    per_batch = SEQ // TP

    def col(width, start):
        return pl.BlockSpec((TP, width), lambda i: (i, start // width))

    def full(shape):
        return pl.BlockSpec(shape, lambda i: (0,) * len(shape))

    def tab():
        return pl.BlockSpec((TP, HEAD), lambda i: (jnp.where(i < lat_tiles, i % per_batch, per_batch), 0))

    def out(width):
        return pl.BlockSpec((TP, width), lambda i: (i, 0))

    widths = (1024, 1024, 512, 512, 256, 512, 512)
    return pl.pallas_call(
        _prep_kernel,
        grid=(n_tiles,),
        in_specs=[col(512, COL_MLA_C), col(512, COL_GQA_Q), col(256, COL_GQA_K),
                  col(512, COL_DIFF_Q), col(512, COL_DIFF_K), col(128, COL_KPE),
                  full((1, MLA_Q_RANK)), full((MLA_Q_RANK, 1024)), full((1, MLA_KV_RANK)),
                  full((MLA_KV_RANK, 512)), full((MLA_KV_RANK, 512)), full((1, HEAD)), full((1, HEAD)),
                  tab(), tab(), tab(), tab()],
        out_specs=[out(w) for w in widths],
        out_shape=[jax.ShapeDtypeStruct((ROWS, w), BF16) for w in widths],
        compiler_params=_cparams(("arbitrary",)),
        name="prep",
    )(proj, proj, proj, proj, proj, proj,
      lw["g_cq"], lw["w_qup"], lw["g_ckv"], lw["w_kvk"], lw["w_kvv"], lw["g_gq"], lw["g_gk"],
      tabs["c64"], tabs["s64"], tabs["c128"], tabs["s128"])


N_HEADS = 4
N_KEYS = CTX_LEN + SEQ


def _attend(s_c, vc, s_l=None, vl=None):
    m = jnp.max(s_c, axis=-1, keepdims=True)
    if s_l is not None:
        m = jnp.maximum(m, jnp.max(s_l, axis=-1, keepdims=True))
    p_c = jnp.exp2(s_c - m)
    den = jnp.sum(p_c, axis=-1, keepdims=True)
    o = _dot(p_c.astype(BF16), vc)
    if s_l is not None:
        p_l = jnp.exp2(s_l - m)
        den = den + jnp.sum(p_l, axis=-1, keepdims=True)
        o = o + _dot(p_l.astype(BF16), vl)
    return o / den


def _attend_ones(q, k, v_ones):
    s = _dot_t(q, k)
    p = jnp.exp2(s - jnp.max(s, axis=-1, keepdims=True)).astype(BF16)
    o = _dot(p, v_ones)
    return o[:, :HEAD] / o[:, HEAD:HEAD + 1]


def _on_query_tiles(with_ctx, lat_fn, ctx_fn):
    if not with_ctx:
        lat_fn()
        return
    qt = pl.program_id(1)
    pl.when(qt < NQ)(lat_fn)
    pl.when(qt == NQ)(ctx_fn)


def _global_kernel(q_ref, kc_ref, vc_ref, kl_ref, vl_ref, *rest, with_ctx, n_kv, q_w, k_w, lam_init):
    if lam_init is None:
        o_ref, kcat, vcat = rest
    else:
        lam_ref, g_ref, o_ref, kcat, vcat = rest

    @pl.when(pl.program_id(1) == 0)
    def _():
        lane = lax.broadcasted_iota(jnp.int32, (N_KEYS, HEAD), 1)
        ones = jnp.where(lane == 0, 1.0, 0.0).astype(BF16)
        for kv in range(n_kv):
            kcat[kv, :CTX_LEN] = kc_ref[:, kv * k_w:(kv + 1) * k_w]
            kcat[kv, CTX_LEN:] = kl_ref[:, kv * k_w:(kv + 1) * k_w]
            vcat[kv, :CTX_LEN, :HEAD] = vc_ref[:, kv * HEAD:(kv + 1) * HEAD]
            vcat[kv, CTX_LEN:, :HEAD] = vl_ref[:, kv * HEAD:(kv + 1) * HEAD]
            vcat[kv, :, HEAD:] = ones

    if lam_init is not None:
        lv = lam_ref[...]
        lam = (jnp.exp(jnp.sum(lv[0:1] * lv[1:2], axis=-1, keepdims=True))
               - jnp.exp(jnp.sum(lv[2:3] * lv[3:4], axis=-1, keepdims=True)) + lam_init)

    def run(n_keys):
        for h in range(N_HEADS):
            kv = h // (N_HEADS // n_kv)
            k, v = kcat[kv, :n_keys], vcat[kv, :n_keys]
            q = q_ref[:, h * q_w:(h + 1) * q_w]
            if lam_init is None:
                o = _attend_ones(q, k, v)
            else:
                lane = lax.broadcasted_iota(jnp.int32, q.shape, 1)
                zero = jnp.zeros_like(q)
                o1 = _attend_ones(jnp.where(lane < DIFF_QK, q, zero), k, v)
                o2 = _attend_ones(jnp.where(lane >= DIFF_QK, q, zero), k, v)
                o = _rms(o1 - lam * o2) * g_ref[...] * (1.0 - lam_init)
            o_ref[:, h * HEAD:(h + 1) * HEAD] = o.astype(o_ref.dtype)

    _on_query_tiles(with_ctx, lambda: run(N_KEYS), lambda: run(CTX_LEN))


def _q_row_block(b, qt):
    return jnp.where(qt < NQ, b * NQ + qt, T_LAT // TQ + b)


def _global_attention(q, q_w, k, k_w, n_kv, k_col, v, v_col, with_ctx, diff=None):
    n_qt = NQ + 1 if with_ctx else NQ
    out_rows = ROWS if with_ctx else T_LAT
    ctx_blk = T_LAT // CTX_LEN

    in_specs = [
        pl.BlockSpec((TQ, N_HEADS * q_w), lambda b, t: (_q_row_block(b, t), 0)),
        pl.BlockSpec((CTX_LEN, n_kv * k_w), lambda b, t: (ctx_blk + b, k_col)),
        pl.BlockSpec((CTX_LEN, n_kv * HEAD), lambda b, t: (ctx_blk + b, v_col)),
        pl.BlockSpec((SEQ, n_kv * k_w), lambda b, t: (b, k_col)),
        pl.BlockSpec((SEQ, n_kv * HEAD), lambda b, t: (b, v_col)),
    ]
    args = [q, k, v, k, v]
    lam_init = None
    if diff is not None:
        lam_vecs, g_sub, lam_init = diff
        in_specs += [pl.BlockSpec((4, DIFF_QK), lambda b, t: (0, 0)),
                     pl.BlockSpec((1, HEAD), lambda b, t: (0, 0))]
        args += [lam_vecs, g_sub]
    return pl.pallas_call(
        functools.partial(_global_kernel, with_ctx=with_ctx, n_kv=n_kv, q_w=q_w, k_w=k_w, lam_init=lam_init),
        grid=(BATCH, n_qt),
        in_specs=in_specs,
        out_specs=pl.BlockSpec((TQ, N_HEADS * HEAD), lambda b, t: (_q_row_block(b, t), 0)),
        out_shape=jax.ShapeDtypeStruct((out_rows, N_HEADS * HEAD), BF16),
        scratch_shapes=[pltpu.VMEM((n_kv, N_KEYS, k_w), BF16), pltpu.VMEM((n_kv, N_KEYS, 2 * HEAD), BF16)],
        compiler_params=_cparams(("arbitrary",) * 2),
        name="diff_attention" if diff is not None else "global_attention",
    )(*args)


NA_QROWS = TQ // GRID_W
NA_WROWS = 12
NA_WIN = NA_WROWS * GRID_W
NA_MASKED = 2 * NA_KH - 1


def _na_kernel(q_ref, kc_ref, vc_ref, k_ref, v_ref, bias_ref, o_ref, bias_scr, *, with_ctx):
    def head(h):
        return slice(h * HEAD, (h + 1) * HEAD)

    def lat():
        r0 = pl.program_id(1) * NA_QROWS
        ws = jnp.clip(r0 - NA_KH // 2, 0, GRID_H - NA_WROWS)
        start = pl.multiple_of(ws * GRID_W, TQ)
        block = {}
        for i in range(NA_QROWS):
            qr = r0 + i
            rs = jnp.clip(qr - NA_KH // 2, 0, GRID_H - NA_KH)
            for j in range(NA_WROWS):
                kr = ws + j
                block[i, j] = jnp.where((kr >= rs) & (kr < rs + NA_KH), kr - qr + NA_KH - 1, NA_MASKED)
        for h in range(NA_HEADS):
            for (i, j), a in block.items():
                half = (j % 2) * GRID_W
                bias_scr[h, i * GRID_W:(i + 1) * GRID_W, j * GRID_W:(j + 1) * GRID_W] = (
                    bias_ref[h, a][:, half:half + GRID_W])
            q = q_ref[:, head(h)]
            kw = k_ref[pl.ds(start, NA_WIN), head(h)]
            vw = v_ref[pl.ds(start, NA_WIN), head(h)]
            s_l = _dot_t(q, kw) * NA_SCALE + bias_scr[h]
            s_c = _dot_t(q, kc_ref[:, head(h)]) * NA_SCALE
            o_ref[:, head(h)] = _attend(s_c, vc_ref[:, head(h)], s_l, vw).astype(o_ref.dtype)

    def ctx():
        for h in range(NA_HEADS):
            s_c = _dot_t(q_ref[:, head(h)], kc_ref[:, head(h)]) * NA_SCALE
            o_ref[:, head(h)] = _attend(s_c, vc_ref[:, head(h)]).astype(o_ref.dtype)

    _on_query_tiles(with_ctx, lat, ctx)


def _na_attention(proj, bias_tab, with_ctx):
    n_qt = NQ + 1 if with_ctx else NQ
    out_rows = ROWS if with_ctx else T_LAT
    ctx_blk = T_LAT // CTX_LEN
    width = NA_HEADS * HEAD
    kcol, vcol = COL_NA_K // width, COL_NA_V // width
    return pl.pallas_call(
        functools.partial(_na_kernel, with_ctx=with_ctx),
        grid=(BATCH, n_qt),
        in_specs=[pl.BlockSpec((TQ, width), lambda b, t: (_q_row_block(b, t), 0)),
                  pl.BlockSpec((CTX_LEN, width), lambda b, t: (ctx_blk + b, kcol)),
                  pl.BlockSpec((CTX_LEN, width), lambda b, t: (ctx_blk + b, vcol)),
                  pl.BlockSpec((SEQ, width), lambda b, t: (b, kcol)),
                  pl.BlockSpec((SEQ, width), lambda b, t: (b, vcol)),
                  pl.BlockSpec((NA_HEADS, 2 * NA_KH, GRID_W, HEAD), lambda b, t: (0, 0, 0, 0))],
        out_specs=pl.BlockSpec((TQ, width), lambda b, t: (_q_row_block(b, t), 0)),
        out_shape=jax.ShapeDtypeStruct((out_rows, width), BF16),
        scratch_shapes=[pltpu.VMEM((NA_HEADS, TQ, NA_WIN), F32)],
        compiler_params=_cparams(("arbitrary",) * 2),
        name="na_attention",
    )(proj, proj, proj, proj, proj, bias_tab)


def _out_proj_kernel(o0_ref, o1_ref, o2_ref, o3_ref, w_ref, h_ref, gt_ref, out_ref):
    acc = None
    for t, o_ref in enumerate((o0_ref, o1_ref, o2_ref, o3_ref)):
        w = w_ref[t * 512:(t + 1) * 512, :].astype(BF16)
        part = _dot(o_ref[...], w)
        acc = part if acc is None else acc + part
    out_ref[...] = h_ref[...] + gt_ref[0] * acc


def _out_proj(o_parts, w_out, h, mod, n_tiles):
    tn = TN_OUT
    o_spec = pl.BlockSpec((TM, 512), lambda i, j: (i, 0))
    return pl.pallas_call(
        _out_proj_kernel,
        grid=(n_tiles, D_MODEL // tn),
        in_specs=[o_spec, o_spec, o_spec, o_spec,
                  pl.BlockSpec((D_MODEL, tn), lambda i, j: (0, j)),
                  pl.BlockSpec((TM, tn), lambda i, j: (i, j)),
                  pl.BlockSpec((1, 1, tn), lambda i, j: (_mod_row(i, TM), 0, 2 * (D_MODEL // tn) + j))],
        out_specs=pl.BlockSpec((TM, tn), lambda i, j: (i, j)),
        out_shape=jax.ShapeDtypeStruct((n_tiles * TM, D_MODEL), F32),
        compiler_params=_cparams(("arbitrary", "arbitrary")),
        name="out_proj",
    )(*o_parts, w_out, h, mod)


def _ffn_kernel(h_ref, g_ref, sh_ref, sc_ref, gt_ref, wg_ref, wu_ref, wd_ref, out_ref, u_scr):
    j = pl.program_id(1)

    @pl.when(j == 0)
    def _():
        y = _rms(h_ref[...]) * g_ref[...]
        u_scr[...] = (y * (1.0 + sc_ref[0]) + sh_ref[0]).astype(BF16)
        out_ref[...] = jnp.zeros_like(out_ref)

    _swiglu_accumulate(u_scr[...], wg_ref, wu_ref, wd_ref, out_ref)

    @pl.when(j == pl.num_programs(1) - 1)
    def _():
        out_ref[...] = h_ref[...] + gt_ref[0] * out_ref[...]


def _ffn_dense(h, g, mod, w_gate, w_up, w_down):
    def mod_spec(chunk):
        return pl.BlockSpec((1, 1, D_MODEL), lambda i, j: (_mod_row(i, TM), 0, chunk))

    return pl.pallas_call(
        _ffn_kernel,
        grid=(N_ALL_TILES, D_FF // TF),
        in_specs=[pl.BlockSpec((TM, D_MODEL), lambda i, j: (i, 0), pipeline_mode=pl.Buffered(1)),
                  pl.BlockSpec((1, D_MODEL), lambda i, j: (0, 0)),
                  mod_spec(3), mod_spec(4), mod_spec(5),
                  pl.BlockSpec((D_MODEL, TF), lambda i, j: (0, j)),
                  pl.BlockSpec((D_MODEL, TF), lambda i, j: (0, j)),
                  pl.BlockSpec((TF, D_MODEL), lambda i, j: (j, 0))],
        out_specs=pl.BlockSpec((TM, D_MODEL), lambda i, j: (i, 0)),
        out_shape=jax.ShapeDtypeStruct((ROWS, D_MODEL), F32),
        scratch_shapes=[pltpu.VMEM((TM, D_MODEL), BF16)],
        compiler_params=_cparams(("arbitrary", "arbitrary")),
        name="ffn_dense",
    )(h, g, mod, mod, mod, w_gate, w_up, w_down)


META_LANES = 128


def _route_kernel(h_ref, g_ref, sh_ref, sc_ref, wr_ref, br_ref, u_ref, meta_ref, cnt_ref, carry):
    i = pl.program_id(0)

    @pl.when(i == 0)
    def _():
        carry[...] = jnp.zeros_like(carry)

    y = _rms(h_ref[...]) * g_ref[...]
    u = y * (1.0 + sc_ref[0]) + sh_ref[0]
    u_ref[...] = u

    w = wr_ref[...]
    u_hi = u.astype(BF16)
    u_lo = (u - u_hi.astype(F32)).astype(BF16)
    w_hi = w.astype(BF16)
    w_lo = (w - w_hi.astype(F32)).astype(BF16)
    logits = _dot(u_hi, w_hi) + (_dot(u_hi, w_lo) + _dot(u_lo, w_hi)) + br_ref[...]

    lane = lax.broadcasted_iota(jnp.int32, logits.shape, 1)
    logits = jnp.where(lane < N_EXPERTS, logits, NEG_INF)
    lane_f = lane.astype(F32)
    m1 = jnp.max(logits, axis=-1, keepdims=True)
    i1 = jnp.min(jnp.where(logits == m1, lane_f, float(META_LANES)), axis=-1, keepdims=True)
    pick1 = lane_f == i1
    rest = jnp.where(pick1, NEG_INF, logits)
    m2 = jnp.max(rest, axis=-1, keepdims=True)
    i2 = jnp.min(jnp.where(rest == m2, lane_f, float(META_LANES)), axis=-1, keepdims=True)
    pick2 = lane_f == i2
    e = jnp.exp(m2 - m1)
    g1 = 1.0 / (1.0 + e)
    g2 = e / (1.0 + e)

    onehot = jnp.where(pick1 | pick2, 1.0, 0.0)
    rr = lax.broadcasted_iota(jnp.int32, (TM, TM), 0)
    cc = lax.broadcasted_iota(jnp.int32, (TM, TM), 1)
    earlier = jnp.where(cc < rr, 1.0, 0.0).astype(BF16)
    before = _dot(earlier, onehot.astype(BF16)) + carry[...]
    r1 = jnp.sum(jnp.where(pick1, before, 0.0), axis=-1, keepdims=True)
    r2 = jnp.sum(jnp.where(pick2, before, 0.0), axis=-1, keepdims=True)
    carry[...] += jnp.sum(onehot, axis=0, keepdims=True)

    meta = jnp.zeros(logits.shape, F32)
    for k, val in enumerate((i1, i2, r1, r2, g1, g2)):
        meta = jnp.where(lane == k, val, meta)
    meta_ref[...] = meta
    cnt_ref[...] = jnp.broadcast_to(carry[...], cnt_ref.shape)


def _moe_route(h, g, mod, w_router, b_router):
    def mod_spec(chunk):
        return pl.BlockSpec((1, 1, D_MODEL), lambda i: (_mod_row(i, TM), 0, chunk))

    return pl.pallas_call(
        _route_kernel,
        grid=(N_LAT_TILES,),
        in_specs=[pl.BlockSpec((TM, D_MODEL), lambda i: (i, 0)),
                  pl.BlockSpec((1, D_MODEL), lambda i: (0, 0)),
                  mod_spec(3), mod_spec(4),
                  pl.BlockSpec((D_MODEL, META_LANES), lambda i: (0, 0)),
                  pl.BlockSpec((1, META_LANES), lambda i: (0, 0))],
        out_specs=[pl.BlockSpec((TM, D_MODEL), lambda i: (i, 0)),
                   pl.BlockSpec((TM, META_LANES), lambda i: (i, 0)),
                   pl.BlockSpec((8, META_LANES), lambda i: (0, 0))],
        out_shape=[jax.ShapeDtypeStruct((T_LAT, D_MODEL), F32),
                   jax.ShapeDtypeStruct((T_LAT, META_LANES), F32),
                   jax.ShapeDtypeStruct((8, META_LANES), F32)],
        scratch_shapes=[pltpu.VMEM((1, META_LANES), F32)],
        compiler_params=_cparams(("arbitrary",)),
        name="moe_route",
    )(h, g, mod, mod, w_router, b_router)


def _row_copy(src, s, dst, d, sem):
    return pltpu.make_async_copy(src.at[pl.ds(s, 1)], dst.at[pl.ds(d, 1)], sem)


MOE_NJ = D_FF_EXPERT // TF
GATHER_ROWS_PER_STEP = -(-TM // MOE_NJ)


def _moe_ffn_kernel(te_ref, nu_ref, src_ref, u_ref, wg_ref, wu_ref, wd_ref, out_ref, xg_scr, xb_scr, sem):
    i, j = pl.program_id(0), pl.program_id(1)
    n_used = nu_ref[0]
    used = i < n_used

    def gather(tile, lo, hi):
        def body(r, c):
            _row_copy(u_ref, src_ref[tile * TM + r], xg_scr, r, sem.at[0]).start()
            return c
        lax.fori_loop(lo, hi, body, 0)

    @pl.when((i == 0) & (j == 0))
    def _():
        gather(0, 0, TM)

    @pl.when(used & (j == 0))
    def _():
        def wait(r, c):
            _row_copy(u_ref, 0, xg_scr, 0, sem.at[0]).wait()
            return c
        lax.fori_loop(0, TM, wait, 0)
        xb_scr[...] = xg_scr[...].astype(BF16)

    @pl.when(j == 0)
    def _():
        out_ref[...] = jnp.zeros_like(out_ref)

    @pl.when(i + 1 < n_used)
    def _():
        lo = j * GATHER_ROWS_PER_STEP
        gather(i + 1, lo, jnp.minimum(lo + GATHER_ROWS_PER_STEP, TM))

    @pl.when(used)
    def _():
        _swiglu_accumulate(xb_scr[...], wg_ref, wu_ref, wd_ref, out_ref)


def _moe_ffn(u, src, tile_expert, n_used, w_gate, w_up, w_down):
    def jj(i, j, nu):
        return jnp.where(i < nu[0], j, MOE_NJ - 1)

    grid_spec = pltpu.PrefetchScalarGridSpec(
        num_scalar_prefetch=3,
        grid=(MOE_TILES, MOE_NJ),
        in_specs=[pl.BlockSpec(memory_space=pl.ANY),
                  pl.BlockSpec((None, D_MODEL, TF), lambda i, j, te, nu, src: (te[i], 0, jj(i, j, nu))),
                  pl.BlockSpec((None, D_MODEL, TF), lambda i, j, te, nu, src: (te[i], 0, jj(i, j, nu))),
                  pl.BlockSpec((None, TF, D_MODEL), lambda i, j, te, nu, src: (te[i], jj(i, j, nu), 0))],
        out_specs=pl.BlockSpec((TM, D_MODEL), lambda i, j, te, nu, src: (i, 0)),
        scratch_shapes=[pltpu.VMEM((TM, D_MODEL), F32), pltpu.VMEM((TM, D_MODEL), BF16),
                        pltpu.SemaphoreType.DMA((1,))],
    )
    return pl.pallas_call(
        _moe_ffn_kernel,
        grid_spec=grid_spec,
        out_shape=jax.ShapeDtypeStruct((MOE_ROWS, D_MODEL), F32),
        compiler_params=_cparams(("arbitrary", "arbitrary")),
        name="moe_ffn",
    )(tile_expert, n_used, src, u, w_gate, w_up, w_down)


def _combine_kernel(pos_ref, h_ref, meta_ref, gt_ref, gf_ref, ys_ref, out_ref, y1, y2, sem):
    i = pl.program_id(0)
    slot = i % 2

    def gather(tile, s):
        def body(r, c):
            _row_copy(ys_ref, pos_ref[tile * TC + r], y1.at[s], r, sem.at[s, 0]).start()
            _row_copy(ys_ref, pos_ref[T_LAT + tile * TC + r], y2.at[s], r, sem.at[s, 1]).start()
            return c
        lax.fori_loop(0, TC, body, 0)

    @pl.when(i == 0)
    def _():
        gather(0, 0)

    @pl.when(i + 1 < pl.num_programs(0))
    def _():
        gather(i + 1, 1 - slot)

    def wait(r, c):
        _row_copy(ys_ref, 0, y1.at[slot], 0, sem.at[slot, 0]).wait()
        _row_copy(ys_ref, 0, y2.at[slot], 0, sem.at[slot, 1]).wait()
        return c

    lax.fori_loop(0, TC, wait, 0)

    meta = meta_ref[...]
    y = meta[:, 4:5] * y1[slot] + meta[:, 5:6] * y2[slot]
    hn = h_ref[...] + gt_ref[0] * y
    out_ref[...] = _rms(hn) * gf_ref[...]


def _moe_combine(pos, h, meta, mod, g_final, ys):
    grid_spec = pltpu.PrefetchScalarGridSpec(
        num_scalar_prefetch=1,
        grid=(T_LAT // TC,),
        in_specs=[pl.BlockSpec((TC, D_MODEL), lambda i, p: (i, 0)),
                  pl.BlockSpec((TC, META_LANES), lambda i, p: (i, 0)),
                  pl.BlockSpec((1, 1, D_MODEL), lambda i, p: (i // (SEQ // TC), 0, 5)),
                  pl.BlockSpec((1, D_MODEL), lambda i, p: (0, 0)),
                  pl.BlockSpec(memory_space=pl.ANY)],
        out_specs=pl.BlockSpec((TC, D_MODEL), lambda i, p: (i, 0)),
        scratch_shapes=[pltpu.VMEM((2, TC, D_MODEL), F32), pltpu.VMEM((2, TC, D_MODEL), F32),
                        pltpu.SemaphoreType.DMA((2, 2))],
    )
    return pl.pallas_call(
        _combine_kernel,
        grid_spec=grid_spec,
        out_shape=jax.ShapeDtypeStruct((T_LAT, D_MODEL), F32),
        compiler_params=_cparams(("arbitrary",)),
        name="moe_combine",
    )(pos, h, meta, mod, g_final, ys)


def _moe_plan(meta, counts_f):
    i1 = meta[:, 0].astype(jnp.int32)
    i2 = meta[:, 1].astype(jnp.int32)
    r1 = meta[:, 2].astype(jnp.int32)
    r2 = meta[:, 3].astype(jnp.int32)
    counts = counts_f[0, :N_EXPERTS].astype(jnp.int32)
    padded = (counts + TM - 1) // TM * TM
    ends = jnp.cumsum(padded)
    starts = ends - padded
    pos = jnp.concatenate([starts[i1] + r1, starts[i2] + r2])
    n_used = ends[-1] // TM
    tile_start = jnp.arange(MOE_TILES, dtype=jnp.int32) * TM
    expert = jnp.minimum(jnp.sum(tile_start[:, None] >= ends[None, :], axis=1), N_EXPERTS - 1)
    expert = expert[jnp.minimum(jnp.arange(MOE_TILES), n_used - 1)].astype(jnp.int32)
    token = jnp.arange(T_LAT, dtype=jnp.int32)
    src = jnp.zeros((MOE_ROWS,), jnp.int32).at[pos].set(jnp.concatenate([token, token]), unique_indices=True)
    return pos.astype(jnp.int32), src, expert, n_used.reshape(1).astype(jnp.int32)


def _rope_tables():
    t = jnp.arange(SEQ, dtype=jnp.int32)
    row = (t // GRID_W).astype(F32)
    col = (t % GRID_W).astype(F32)
    out = {}
    for dim, name in ((MLA_ROPE, "64"), (HEAD, "128")):
        half = dim // 2
        freqs = ROPE_THETA ** (-jnp.arange(0, half, 2, dtype=F32) / half)
        ar, ac = row[:, None] * freqs, col[:, None] * freqs
        c = jnp.concatenate([jnp.cos(ar), jnp.cos(ar), jnp.cos(ac), jnp.cos(ac)], axis=1)
        s = jnp.concatenate([-jnp.sin(ar), jnp.sin(ar), -jnp.sin(ac), jnp.sin(ac)], axis=1)
        c = jnp.tile(c, (1, HEAD // dim))
        s = jnp.tile(s, (1, HEAD // dim))
        out["c" + name] = jnp.concatenate([c, jnp.ones((TP, HEAD), F32)], axis=0)
        out["s" + name] = jnp.concatenate([s, jnp.zeros((TP, HEAD), F32)], axis=0)
    return out


def _na_bias_table(rel_bias):
    qc = np.arange(GRID_W)[:, None]
    kc = np.arange(GRID_W)[None, :]
    cs = np.clip(qc - NA_KW // 2, 0, GRID_W - NA_KW)
    col_ok = (kc >= cs) & (kc < cs + NA_KW)
    col_off = np.clip(kc - qc + NA_KW - 1, 0, 2 * NA_KW - 2)
    tab = jnp.where(jnp.asarray(col_ok), rel_bias[:, :, col_off] * LOG2E, NEG_INF)
    tab = jnp.concatenate([tab, jnp.full((NA_HEADS, 1, GRID_W, GRID_W), NEG_INF, F32)], axis=1)
    return jnp.concatenate([tab, tab], axis=-1)


def _layer_weights(l, w_in, mla_g_q, mla_w_qup, mla_g_kv, mla_w_kvup, gqa_g_q, gqa_g_k):
    sizes = (512, 512, 512, 384, 128, 64, 512, 256, 256, 512, 512, 512)
    parts = jnp.split(w_in[l], np.cumsum(sizes)[:-1].tolist(), axis=1)
    order = (0, 1, 2, 3, 4, 6, 7, 8, 9, 10, 11, 5)
    pad = jnp.zeros((D_MODEL, IN_COLS_PAD - sum(sizes)), F32)
    w_in_p = jnp.concatenate([parts[k] for k in order] + [pad], axis=1).astype(BF16)

    qup = mla_w_qup[l].reshape(MLA_Q_RANK, MLA_HEADS, MLA_NOPE + MLA_ROPE)
    qup = jnp.pad(qup, ((0, 0), (0, 0), (0, 2 * HEAD - MLA_NOPE - MLA_ROPE)))
    kvup = mla_w_kvup[l].reshape(MLA_KV_RANK, MLA_HEADS, MLA_NOPE + MLA_V)
    return {
        "w_in": w_in_p,
        "g_cq": mla_g_q[l].reshape(1, -1),
        "w_qup": qup.reshape(MLA_Q_RANK, MLA_HEADS * 2 * HEAD).astype(BF16),
        "g_ckv": mla_g_kv[l].reshape(1, -1),
        "w_kvk": kvup[:, :, :MLA_NOPE].reshape(MLA_KV_RANK, -1).astype(BF16),
        "w_kvv": kvup[:, :, MLA_NOPE:].reshape(MLA_KV_RANK, -1).astype(BF16),
        "g_gq": gqa_g_q[l].reshape(1, -1),
        "g_gk": gqa_g_k[l].reshape(1, -1),
    }


def kernel(x, c, ctx, c_ctx, w_mod, b_mod, g_mix, w_in, na_rel_bias, mla_g_q, mla_w_qup, mla_g_kv, mla_w_kvup,
           gqa_g_q, gqa_g_k, diff_lq1, diff_lk1, diff_lq2, diff_lk2, diff_g_sub, w_out, g_ffn,
           ffn_w_gate, ffn_w_up, ffn_w_down, moe_w_router, moe_b_router, moe_w_gate, moe_w_up, moe_w_down,
           g_final):
    tabs = _rope_tables()
    cvec = jnp.concatenate([c, c_ctx[None, :], jnp.zeros((16 - BATCH - 1, D_MODEL), F32)], axis=0)
    mod_all = _modulation(cvec, w_mod, b_mod)
    h = jnp.concatenate([x.reshape(T_LAT, D_MODEL), ctx.reshape(T_CTX, D_MODEL)], axis=0)

    out = None
    for l in range(DEPTH):
        last = l == DEPTH - 1
        with_ctx = not last
        lam_init = 0.8 - 0.6 * math.exp(-0.3 * l)
        mod = mod_all[l].reshape(16, 1, 6 * D_MODEL)
        lw = _layer_weights(l, w_in, mla_g_q, mla_w_qup, mla_g_kv, mla_w_kvup, gqa_g_q, gqa_g_k)

        proj = _in_proj(h, g_mix[l].reshape(1, -1), mod, lw["w_in"])
        q_mla, k_mla, v_mla, q_gqa, k_gqa, q_diff, k_diff = _prep(proj, lw, tabs)

        o_na = _na_attention(proj, _na_bias_table(na_rel_bias[l]), with_ctx)
        o_mla = _global_attention(q_mla, 2 * HEAD, k_mla, 2 * HEAD, MLA_HEADS, 0, v_mla, 0, with_ctx)
        o_gqa = _global_attention(q_gqa, HEAD, k_gqa, HEAD, GQA_KV_HEADS, 0,
                                  proj, COL_GQA_V // (GQA_KV_HEADS * HEAD), with_ctx)
        lam_vecs = jnp.stack([diff_lq1[l], diff_lk1[l], diff_lq2[l], diff_lk2[l]])
        o_diff = _global_attention(q_diff, HEAD, k_diff, HEAD, DIFF_HEADS, 0,
                                   proj, COL_DIFF_V // (DIFF_HEADS * HEAD), with_ctx,
                                   diff=(lam_vecs, diff_g_sub[l].reshape(1, -1), lam_init))

        n_tiles = N_ALL_TILES if with_ctx else N_LAT_TILES
        h = _out_proj((o_na, o_mla, o_gqa, o_diff), w_out[l], h, mod, n_tiles)

        if l % 2 == 0:
            h = _ffn_dense(h, g_ffn[l].reshape(1, -1), mod, ffn_w_gate[l // 2], ffn_w_up[l // 2],
                           ffn_w_down[l // 2])
        else:
            m = l // 2
            w_r = jnp.pad(moe_w_router[m], ((0, 0), (0, META_LANES - N_EXPERTS)))
            b_r = jnp.pad(moe_b_router[m], (0, META_LANES - N_EXPERTS)).reshape(1, -1)
            u, meta, counts = _moe_route(h, g_ffn[l].reshape(1, -1), mod, w_r, b_r)
            pos, src, tile_expert, n_used = _moe_plan(meta, counts)
            ys = _moe_ffn(u, src, tile_expert, n_used, moe_w_gate[m], moe_w_up[m], moe_w_down[m])
            out = _moe_combine(pos, h, meta, mod, g_final.reshape(1, -1), ys)
    return out.reshape(BATCH, SEQ, D_MODEL)
```

```python
import functools
import math

import numpy as np
import jax
import jax.numpy as jnp
from jax import lax
from jax.experimental import pallas as pl
from jax.experimental.pallas import tpu as pltpu

F32 = jnp.float32
BF16 = jnp.bfloat16

D_MODEL = 2048
BATCH = 8
SEQ = 2048
DEPTH = 2
GRID_W = 64
GRID_H = SEQ // GRID_W
CTX_LEN = 256
ROPE_THETA = 10000.0
EPS = 1e-6
NEG_INF = -1e30

NA_HEADS = 4
NA_KH = 8
NA_KW = 16
MLA_HEADS = 4
MLA_NOPE = 128
MLA_ROPE = 64
MLA_V = 128
MLA_Q_RANK = 384
MLA_KV_RANK = 128
GQA_HEADS = 4
GQA_KV_HEADS = 2
DIFF_HEADS = 4
DIFF_QK = 64
HEAD = 128

LOG2E = math.log2(math.e)
NA_SCALE = HEAD ** -0.5 * LOG2E
MLA_SCALE = (MLA_NOPE + MLA_ROPE) ** -0.5 * LOG2E
GQA_SCALE = HEAD ** -0.5 * LOG2E
DIFF_SCALE = DIFF_QK ** -0.5 * LOG2E

D_FF = 5632
N_EXPERTS = 8
D_FF_EXPERT = 7168

T_LAT = BATCH * SEQ
T_CTX = BATCH * CTX_LEN
ROWS = T_LAT + T_CTX

COL_NA_Q, COL_NA_K, COL_NA_V = 0, 512, 1024
COL_MLA_C = 1536
COL_GQA_Q, COL_GQA_K, COL_GQA_V = 2048, 2560, 2816
COL_DIFF_Q, COL_DIFF_K, COL_DIFF_V = 3072, 3584, 4096
COL_KPE = 4608
IN_COLS_PAD = 5120

V7X_VMEM_LIMIT = 56 * 1024 * 1024

TM = 1024
N_LAT_TILES = T_LAT // TM
N_ALL_TILES = ROWS // TM
TILES_PER_BATCH = SEQ // TM
TQ = 256
NQ = SEQ // TQ
TP = 512
TN_IN = 512
TM_O = 512
TF = 256
MOE_TILES = 2 * T_LAT // TM + N_EXPERTS
MOE_ROWS = MOE_TILES * TM
TC = 256


def _cparams(sem, vmem=V7X_VMEM_LIMIT):
    return pltpu.CompilerParams(dimension_semantics=sem, vmem_limit_bytes=vmem)


def _mod_row(i, tm):
    return jnp.where(i < T_LAT // tm, i // (SEQ // tm), BATCH)


def _dot(a, b):
    return jnp.dot(a, b, preferred_element_type=F32)


def _dot_t(a, b):
    return lax.dot_general(a, b, (((1,), (1,)), ((), ())), preferred_element_type=F32)


def _rms(x):
    return x * lax.rsqrt(jnp.mean(x * x, axis=-1, keepdims=True) + EPS)


def _silu(x):
    return x * jax.nn.sigmoid(x)


def _swiglu_accumulate(x, wg_ref, wu_ref, wd_ref, out_ref):
    a = (_silu(_dot(x, wg_ref[...].astype(BF16))) * _dot(x, wu_ref[...].astype(BF16))).astype(BF16)
    for c in range(0, D_MODEL, 512):
        out_ref[:, c:c + 512] += _dot(a, wd_ref[:, c:c + 512].astype(BF16))


def _mod_kernel(c_ref, w_ref, b_ref, o_ref):
    s = _silu(c_ref[...]).astype(BF16)
    o_ref[0] = _dot(s, w_ref[0].astype(BF16)) + b_ref[0]


def _modulation(cvec, w_mod, b_mod):
    tn = 1024
    return pl.pallas_call(
        _mod_kernel,
        grid=(DEPTH, 6 * D_MODEL // tn),
        in_specs=[pl.BlockSpec((16, D_MODEL), lambda l, j: (0, 0)),
                  pl.BlockSpec((1, D_MODEL, tn), lambda l, j: (l, 0, j)),
                  pl.BlockSpec((1, 1, tn), lambda l, j: (l, 0, j))],
        out_specs=pl.BlockSpec((1, 16, tn), lambda l, j: (l, 0, j)),
        out_shape=jax.ShapeDtypeStruct((DEPTH, 16, 6 * D_MODEL), F32),
        compiler_params=_cparams(("arbitrary", "arbitrary")),
        name="modulation",
    )(cvec, w_mod, b_mod.reshape(DEPTH, 1, 6 * D_MODEL))


def _stream_specs(stream, tm):
    if len(stream) == 1:
        return [pl.BlockSpec((tm, D_MODEL), lambda i, *_: (i, 0))]
    n_lat = T_LAT // tm
    return [pl.BlockSpec((tm, D_MODEL), lambda i, *_: (jnp.minimum(i, n_lat - 1), 0)),
            pl.BlockSpec((tm, D_MODEL), lambda i, *_: (jnp.maximum(i - n_lat, 0), 0))]


def _with_stream_rows(h_refs, tm, fn):
    if len(h_refs) == 1:
        fn(h_refs[0])
        return
    i = pl.program_id(0)
    n_lat = T_LAT // tm
    pl.when(i < n_lat)(lambda: fn(h_refs[0]))
    pl.when(i >= n_lat)(lambda: fn(h_refs[1]))


def _in_proj_kernel(*refs):
    *h_refs, g_ref, sh_ref, sc_ref, w_ref, o_ref, u_scr = refs

    def norm(h_ref):
        y = _rms(h_ref[...]) * g_ref[...]
        u_scr[...] = (y * (1.0 + sc_ref[0]) + sh_ref[0]).astype(BF16)

    @pl.when(pl.program_id(1) == 0)
    def _():
        _with_stream_rows(h_refs, TM, norm)

    o_ref[...] = _dot(u_scr[...], w_ref[...]).astype(o_ref.dtype)


def _in_proj(stream, g, mod, w_in):
    return pl.pallas_call(
        _in_proj_kernel,
        grid=(N_ALL_TILES, IN_COLS_PAD // TN_IN),
        in_specs=_stream_specs(stream, TM) + [
                  pl.BlockSpec((1, D_MODEL), lambda i, j: (0, 0)),
                  pl.BlockSpec((1, 1, D_MODEL), lambda i, j: (_mod_row(i, TM), 0, 0)),
                  pl.BlockSpec((1, 1, D_MODEL), lambda i, j: (_mod_row(i, TM), 0, 1)),
                  pl.BlockSpec((D_MODEL, TN_IN), lambda i, j: (0, j))],
        out_specs=pl.BlockSpec((TM, TN_IN), lambda i, j: (i, j)),
        out_shape=jax.ShapeDtypeStruct((ROWS, IN_COLS_PAD), BF16),
        scratch_shapes=[pltpu.VMEM((TM, D_MODEL), BF16)],
        compiler_params=_cparams(("arbitrary", "arbitrary")),
        name="in_proj",
    )(*stream, g, mod, mod, w_in)


def _rope(x, c, s, q):
    w = x.shape[-1]
    lane = lax.broadcasted_iota(jnp.int32, x.shape, 1)
    partner = jnp.where((lane & q) == 0, pltpu.roll(x, w - q, 1), pltpu.roll(x, q, 1))
    return x * c + partner * s


def _prep_kernel(mc_ref, gq_ref, gk_ref, dq_ref, dk_ref, kpe_ref,
                 g_cq_ref, w_qup_ref, g_ckv_ref, w_kvk_ref, w_kvv_ref, g_gq_ref, g_gk_ref,
                 c64_ref, s64_ref, c128_ref, s128_ref,
                 qm_ref, km_ref, vm_ref, qg_ref, kg_ref, qd_ref, kd_ref):
    c64, s64 = c64_ref[...], s64_ref[...]
    c128, s128 = c128_ref[...], s128_ref[...]
    q64, q128 = MLA_ROPE // 4, HEAD // 4

    mc = mc_ref[...].astype(F32)
    cq = (_rms(mc[:, :MLA_Q_RANK]) * g_cq_ref[...]).astype(BF16)
    ckv = (_rms(mc[:, MLA_Q_RANK:]) * g_ckv_ref[...]).astype(BF16)
    q = _dot(cq, w_qup_ref[...])
    kn = _dot(ckv, w_kvk_ref[...])
    vm_ref[...] = _dot(ckv, w_kvv_ref[...]).astype(BF16)
    kpe = _rope(kpe_ref[...].astype(F32), c64, s64, q64).astype(BF16)
    for h in range(MLA_HEADS):
        lo = 2 * HEAD * h
        qm_ref[:, lo:lo + HEAD] = (q[:, lo:lo + HEAD] * MLA_SCALE).astype(BF16)
        pe = _rope(q[:, lo + HEAD:lo + 2 * HEAD], c64, s64, q64)
        qm_ref[:, lo + HEAD:lo + 2 * HEAD] = (pe * MLA_SCALE).astype(BF16)
        km_ref[:, lo:lo + HEAD] = kn[:, h * HEAD:(h + 1) * HEAD].astype(BF16)
        km_ref[:, lo + HEAD:lo + 2 * HEAD] = kpe

    for h in range(GQA_HEADS):
        sl = slice(h * HEAD, (h + 1) * HEAD)
        z = _rms(gq_ref[:, sl].astype(F32)) * g_gq_ref[...]
        qg_ref[:, sl] = (_rope(z, c128, s128, q128) * GQA_SCALE).astype(BF16)
    for h in range(GQA_KV_HEADS):
        sl = slice(h * HEAD, (h + 1) * HEAD)
        z = _rms(gk_ref[:, sl].astype(F32)) * g_gk_ref[...]
        kg_ref[:, sl] = _rope(z, c128, s128, q128).astype(BF16)

    for h in range(DIFF_HEADS):
        sl = slice(h * HEAD, (h + 1) * HEAD)
        qd_ref[:, sl] = (_rope(dq_ref[:, sl].astype(F32), c64, s64, q64) * DIFF_SCALE).astype(BF16)
        kd_ref[:, sl] = _rope(dk_ref[:, sl].astype(F32), c64, s64, q64).astype(BF16)


def _prep(proj, lw, tabs):
    n_tiles = ROWS // TP
    lat_tiles = T_LAT // TP
    per_batch = SEQ // TP

    def col(width, start):
        return pl.BlockSpec((TP, width), lambda i: (i, start // width))

    def full(shape):
        return pl.BlockSpec(shape, lambda i: (0,) * len(shape))

    def tab():
        return pl.BlockSpec((TP, HEAD), lambda i: (jnp.where(i < lat_tiles, i % per_batch, per_batch), 0))

    def out(width):
        return pl.BlockSpec((TP, width), lambda i: (i, 0))

    widths = (1024, 1024, 512, 512, 256, 512, 512)
    return pl.pallas_call(
        _prep_kernel,
        grid=(n_tiles,),
        in_specs=[col(512, COL_MLA_C), col(512, COL_GQA_Q), col(256, COL_GQA_K),
                  col(512, COL_DIFF_Q), col(512, COL_DIFF_K), col(128, COL_KPE),
                  full((1, MLA_Q_RANK)), full((MLA_Q_RANK, 1024)), full((1, MLA_KV_RANK)),
                  full((MLA_KV_RANK, 512)), full((MLA_KV_RANK, 512)), full((1, HEAD)), full((1, HEAD)),
                  tab(), tab(), tab(), tab()],
        out_specs=[out(w) for w in widths],
        out_shape=[jax.ShapeDtypeStruct((ROWS, w), BF16) for w in widths],
        compiler_params=_cparams(("arbitrary",)),
        name="prep",
    )(proj, proj, proj, proj, proj, proj,
      lw["g_cq"], lw["w_qup"], lw["g_ckv"], lw["w_kvk"], lw["w_kvv"], lw["g_gq"], lw["g_gk"],
      tabs["c64"], tabs["s64"], tabs["c128"], tabs["s128"])


N_HEADS = 4
N_KEYS = CTX_LEN + SEQ


def _attend(s_c, vc, s_l=None, vl=None):
    m = jnp.max(s_c, axis=-1, keepdims=True)
    if s_l is not None:
        m = jnp.maximum(m, jnp.max(s_l, axis=-1, keepdims=True))
    p_c = jnp.exp2(s_c - m)
    den = jnp.sum(p_c, axis=-1, keepdims=True)
    o = _dot(p_c.astype(BF16), vc)
    if s_l is not None:
        p_l = jnp.exp2(s_l - m)
        den = den + jnp.sum(p_l, axis=-1, keepdims=True)
        o = o + _dot(p_l.astype(BF16), vl)
    return o / den


def _attend_ones(q, k, v_ones):
    s = _dot_t(q, k)
    p = jnp.exp2(s - jnp.max(s, axis=-1, keepdims=True)).astype(BF16)
    o = _dot(p, v_ones)
    return o[:, :HEAD] / o[:, HEAD:HEAD + 1]


def _on_query_tiles(with_ctx, lat_fn, ctx_fn):
    if not with_ctx:
        lat_fn()
        return
    qt = pl.program_id(1)
    pl.when(qt < NQ)(lat_fn)
    pl.when(qt == NQ)(ctx_fn)


def _global_kernel(q_ref, kc_ref, vc_ref, kl_ref, vl_ref, *rest, with_ctx, n_kv, q_w, k_w, lam_init):
    if lam_init is None:
        o_ref, kcat, vcat = rest
    else:
        lam_ref, g_ref, o_ref, kcat, vcat = rest

    @pl.when(pl.program_id(1) == 0)
    def _():
        lane = lax.broadcasted_iota(jnp.int32, (N_KEYS, HEAD), 1)
        ones = jnp.where(lane == 0, 1.0, 0.0).astype(BF16)
        for kv in range(n_kv):
            kcat[kv, :CTX_LEN] = kc_ref[:, kv * k_w:(kv + 1) * k_w]
            kcat[kv, CTX_LEN:] = kl_ref[:, kv * k_w:(kv + 1) * k_w]
            vcat[kv, :CTX_LEN, :HEAD] = vc_ref[:, kv * HEAD:(kv + 1) * HEAD]
            vcat[kv, CTX_LEN:, :HEAD] = vl_ref[:, kv * HEAD:(kv + 1) * HEAD]
            vcat[kv, :, HEAD:] = ones

    if lam_init is not None:
        lv = lam_ref[...]
        lam = (jnp.exp(jnp.sum(lv[0:1] * lv[1:2], axis=-1, keepdims=True))
               - jnp.exp(jnp.sum(lv[2:3] * lv[3:4], axis=-1, keepdims=True)) + lam_init)

    def run(n_keys):
        for h in range(N_HEADS):
            kv = h // (N_HEADS // n_kv)
            k, v = kcat[kv, :n_keys], vcat[kv, :n_keys]
            q = q_ref[:, h * q_w:(h + 1) * q_w]
            if lam_init is None:
                o = _attend_ones(q, k, v)
            else:
                lane = lax.broadcasted_iota(jnp.int32, q.shape, 1)
                zero = jnp.zeros_like(q)
                o1 = _attend_ones(jnp.where(lane < DIFF_QK, q, zero), k, v)
                o2 = _attend_ones(jnp.where(lane >= DIFF_QK, q, zero), k, v)
                o = _rms(o1 - lam * o2) * g_ref[...] * (1.0 - lam_init)
            o_ref[:, h * HEAD:(h + 1) * HEAD] = o.astype(o_ref.dtype)

    _on_query_tiles(with_ctx, lambda: run(N_KEYS), lambda: run(CTX_LEN))


def _q_row_block(b, qt):
    return jnp.where(qt < NQ, b * NQ + qt, T_LAT // TQ + b)


def _global_attention(q, q_w, k, k_w, n_kv, k_col, v, v_col, with_ctx, diff=None):
    n_qt = NQ + 1 if with_ctx else NQ
    out_rows = ROWS if with_ctx else T_LAT
    ctx_blk = T_LAT // CTX_LEN

    in_specs = [
        pl.BlockSpec((TQ, N_HEADS * q_w), lambda b, t: (_q_row_block(b, t), 0)),
        pl.BlockSpec((CTX_LEN, n_kv * k_w), lambda b, t: (ctx_blk + b, k_col)),
        pl.BlockSpec((CTX_LEN, n_kv * HEAD), lambda b, t: (ctx_blk + b, v_col)),
        pl.BlockSpec((SEQ, n_kv * k_w), lambda b, t: (b, k_col)),
        pl.BlockSpec((SEQ, n_kv * HEAD), lambda b, t: (b, v_col)),
    ]
    args = [q, k, v, k, v]
    lam_init = None
    if diff is not None:
        lam_vecs, g_sub, lam_init = diff
        in_specs += [pl.BlockSpec((4, DIFF_QK), lambda b, t: (0, 0)),
                     pl.BlockSpec((1, HEAD), lambda b, t: (0, 0))]
        args += [lam_vecs, g_sub]
    return pl.pallas_call(
        functools.partial(_global_kernel, with_ctx=with_ctx, n_kv=n_kv, q_w=q_w, k_w=k_w, lam_init=lam_init),
        grid=(BATCH, n_qt),
        in_specs=in_specs,
        out_specs=pl.BlockSpec((TQ, N_HEADS * HEAD), lambda b, t: (_q_row_block(b, t), 0)),
        out_shape=jax.ShapeDtypeStruct((out_rows, N_HEADS * HEAD), BF16),
        scratch_shapes=[pltpu.VMEM((n_kv, N_KEYS, k_w), BF16), pltpu.VMEM((n_kv, N_KEYS, 2 * HEAD), BF16)],
        compiler_params=_cparams(("arbitrary",) * 2),
        name="diff_attention" if diff is not None else "global_attention",
    )(*args)


NA_QROWS = TQ // GRID_W
NA_WROWS = 12
NA_WIN = NA_WROWS * GRID_W
NA_MASKED = 2 * NA_KH - 1


def _na_kernel(q_ref, kc_ref, vc_ref, k_ref, v_ref, bias_ref, o_ref, bias_scr, *, with_ctx):
    def head(h):
        return slice(h * HEAD, (h + 1) * HEAD)

    def lat():
        r0 = pl.program_id(1) * NA_QROWS
        ws = jnp.clip(r0 - NA_KH // 2, 0, GRID_H - NA_WROWS)
        start = pl.multiple_of(ws * GRID_W, TQ)
        block = {}
        for i in range(NA_QROWS):
            qr = r0 + i
            rs = jnp.clip(qr - NA_KH // 2, 0, GRID_H - NA_KH)
            for j in range(NA_WROWS):
                kr = ws + j
                block[i, j] = jnp.where((kr >= rs) & (kr < rs + NA_KH), kr - qr + NA_KH - 1, NA_MASKED)
        for h in range(NA_HEADS):
            for (i, j), a in block.items():
                half = (j % 2) * GRID_W
                bias_scr[h, i * GRID_W:(i + 1) * GRID_W, j * GRID_W:(j + 1) * GRID_W] = (
                    bias_ref[h, a][:, half:half + GRID_W])
            q = q_ref[:, head(h)]
            kw = k_ref[pl.ds(start, NA_WIN), head(h)]
            vw = v_ref[pl.ds(start, NA_WIN), head(h)]
            s_l = _dot_t(q, kw) * NA_SCALE + bias_scr[h]
            s_c = _dot_t(q, kc_ref[:, head(h)]) * NA_SCALE
            o_ref[:, head(h)] = _attend(s_c, vc_ref[:, head(h)], s_l, vw).astype(o_ref.dtype)

    def ctx():
        for h in range(NA_HEADS):
            s_c = _dot_t(q_ref[:, head(h)], kc_ref[:, head(h)]) * NA_SCALE
            o_ref[:, head(h)] = _attend(s_c, vc_ref[:, head(h)]).astype(o_ref.dtype)

    _on_query_tiles(with_ctx, lat, ctx)


def _na_attention(proj, bias_tab, with_ctx):
    n_qt = NQ + 1 if with_ctx else NQ
    out_rows = ROWS if with_ctx else T_LAT
    ctx_blk = T_LAT // CTX_LEN
    width = NA_HEADS * HEAD
    kcol, vcol = COL_NA_K // width, COL_NA_V // width
    return pl.pallas_call(
        functools.partial(_na_kernel, with_ctx=with_ctx),
        grid=(BATCH, n_qt),
        in_specs=[pl.BlockSpec((TQ, width), lambda b, t: (_q_row_block(b, t), 0)),
                  pl.BlockSpec((CTX_LEN, width), lambda b, t: (ctx_blk + b, kcol)),
                  pl.BlockSpec((CTX_LEN, width), lambda b, t: (ctx_blk + b, vcol)),
                  pl.BlockSpec((SEQ, width), lambda b, t: (b, kcol)),
                  pl.BlockSpec((SEQ, width), lambda b, t: (b, vcol)),
                  pl.BlockSpec((NA_HEADS, 2 * NA_KH, GRID_W, HEAD), lambda b, t: (0, 0, 0, 0))],
        out_specs=pl.BlockSpec((TQ, width), lambda b, t: (_q_row_block(b, t), 0)),
        out_shape=jax.ShapeDtypeStruct((out_rows, width), BF16),
        scratch_shapes=[pltpu.VMEM((NA_HEADS, TQ, NA_WIN), F32)],
        compiler_params=_cparams(("arbitrary",) * 2),
        name="na_attention",
    )(proj, proj, proj, proj, proj, bias_tab)


def _out_proj_kernel(o0_ref, o1_ref, o2_ref, o3_ref, w_ref, gt_ref, *refs):
    *h_refs, out_ref = refs
    for c in range(0, D_MODEL, 512):
        acc = None
        for t, o_ref in enumerate((o0_ref, o1_ref, o2_ref, o3_ref)):
            part = _dot(o_ref[...], w_ref[t * 512:(t + 1) * 512, c:c + 512])
            acc = part if acc is None else acc + part
        out_ref[:, c:c + 512] = gt_ref[0, :, c:c + 512] * acc

    def residual(h_ref):
        out_ref[...] += h_ref[...]

    _with_stream_rows(h_refs, TM_O, residual)


def _out_proj(o_parts, w_out, stream, mod, n_rows):
    o_spec = pl.BlockSpec((TM_O, 512), lambda i: (i, 0))
    return pl.pallas_call(
        _out_proj_kernel,
        grid=(n_rows // TM_O,),
        in_specs=[o_spec, o_spec, o_spec, o_spec,
                  pl.BlockSpec((D_MODEL, D_MODEL), lambda i: (0, 0), pipeline_mode=pl.Buffered(1)),
                  pl.BlockSpec((1, 1, D_MODEL), lambda i: (_mod_row(i, TM_O), 0, 2))]
                 + _stream_specs(stream, TM_O),
        out_specs=pl.BlockSpec((TM_O, D_MODEL), lambda i: (i, 0)),
        out_shape=jax.ShapeDtypeStruct((n_rows, D_MODEL), F32),
        compiler_params=_cparams(("arbitrary",)),
        name="out_proj",
    )(*o_parts, w_out, mod, *stream)


def _ffn_kernel(h_ref, g_ref, sh_ref, sc_ref, gt_ref, wg_ref, wu_ref, wd_ref, out_ref, u_scr):
    j = pl.program_id(1)

    @pl.when(j == 0)
    def _():
        y = _rms(h_ref[...]) * g_ref[...]
        u_scr[...] = (y * (1.0 + sc_ref[0]) + sh_ref[0]).astype(BF16)
        out_ref[...] = jnp.zeros_like(out_ref)

    _swiglu_accumulate(u_scr[...], wg_ref, wu_ref, wd_ref, out_ref)

    @pl.when(j == pl.num_programs(1) - 1)
    def _():
        out_ref[...] = h_ref[...] + gt_ref[0] * out_ref[...]


def _ffn_dense(h, g, mod, w_gate, w_up, w_down):
    def mod_spec(chunk):
        return pl.BlockSpec((1, 1, D_MODEL), lambda i, j: (_mod_row(i, TM), 0, chunk))

    return pl.pallas_call(
        _ffn_kernel,
        grid=(N_ALL_TILES, D_FF // TF),
        in_specs=[pl.BlockSpec((TM, D_MODEL), lambda i, j: (i, 0), pipeline_mode=pl.Buffered(1)),
                  pl.BlockSpec((1, D_MODEL), lambda i, j: (0, 0)),
                  mod_spec(3), mod_spec(4), mod_spec(5),
                  pl.BlockSpec((D_MODEL, TF), lambda i, j: (0, j)),
                  pl.BlockSpec((D_MODEL, TF), lambda i, j: (0, j)),
                  pl.BlockSpec((TF, D_MODEL), lambda i, j: (j, 0))],
        out_specs=pl.BlockSpec((TM, D_MODEL), lambda i, j: (i, 0)),
        out_shape=jax.ShapeDtypeStruct((ROWS, D_MODEL), F32),
        scratch_shapes=[pltpu.VMEM((TM, D_MODEL), BF16)],
        compiler_params=_cparams(("arbitrary", "arbitrary")),
        name="ffn_dense",
    )(h, g, mod, mod, mod, w_gate, w_up, w_down)


META_LANES = 128


def _route_kernel(h_ref, g_ref, sh_ref, sc_ref, wr_ref, br_ref, u_ref, meta_ref, cnt_ref, carry):
    i = pl.program_id(0)

    @pl.when(i == 0)
    def _():
        carry[...] = jnp.zeros_like(carry)

    y = _rms(h_ref[...]) * g_ref[...]
    u = y * (1.0 + sc_ref[0]) + sh_ref[0]
    u_ref[...] = u

    w = wr_ref[...]
    u_hi = u.astype(BF16)
    u_lo = (u - u_hi.astype(F32)).astype(BF16)
    w_hi = w.astype(BF16)
    w_lo = (w - w_hi.astype(F32)).astype(BF16)
    logits = _dot(u_hi, w_hi) + (_dot(u_hi, w_lo) + _dot(u_lo, w_hi)) + br_ref[...]

    lane = lax.broadcasted_iota(jnp.int32, logits.shape, 1)
    logits = jnp.where(lane < N_EXPERTS, logits, NEG_INF)
    lane_f = lane.astype(F32)
    m1 = jnp.max(logits, axis=-1, keepdims=True)
    i1 = jnp.min(jnp.where(logits == m1, lane_f, float(META_LANES)), axis=-1, keepdims=True)
    pick1 = lane_f == i1
    rest = jnp.where(pick1, NEG_INF, logits)
    m2 = jnp.max(rest, axis=-1, keepdims=True)
    i2 = jnp.min(jnp.where(rest == m2, lane_f, float(META_LANES)), axis=-1, keepdims=True)
    pick2 = lane_f == i2
    e = jnp.exp(m2 - m1)
    g1 = 1.0 / (1.0 + e)
    g2 = e / (1.0 + e)

    onehot = jnp.where(pick1 | pick2, 1.0, 0.0)
    rr = lax.broadcasted_iota(jnp.int32, (TM, TM), 0)
    cc = lax.broadcasted_iota(jnp.int32, (TM, TM), 1)
    earlier = jnp.where(cc < rr, 1.0, 0.0).astype(BF16)
    before = _dot(earlier, onehot.astype(BF16)) + carry[...]
    r1 = jnp.sum(jnp.where(pick1, before, 0.0), axis=-1, keepdims=True)
    r2 = jnp.sum(jnp.where(pick2, before, 0.0), axis=-1, keepdims=True)
    carry[...] += jnp.sum(onehot, axis=0, keepdims=True)

    meta = jnp.zeros(logits.shape, F32)
    for k, val in enumerate((i1, i2, r1, r2, g1, g2)):
        meta = jnp.where(lane == k, val, meta)
    meta_ref[...] = meta
    cnt_ref[...] = jnp.broadcast_to(carry[...], cnt_ref.shape)


def _moe_route(h, g, mod, w_router, b_router):
    def mod_spec(chunk):
        return pl.BlockSpec((1, 1, D_MODEL), lambda i: (_mod_row(i, TM), 0, chunk))

    return pl.pallas_call(
        _route_kernel,
        grid=(N_LAT_TILES,),
        in_specs=[pl.BlockSpec((TM, D_MODEL), lambda i: (i, 0)),
                  pl.BlockSpec((1, D_MODEL), lambda i: (0, 0)),
                  mod_spec(3), mod_spec(4),
                  pl.BlockSpec((D_MODEL, META_LANES), lambda i: (0, 0)),
                  pl.BlockSpec((1, META_LANES), lambda i: (0, 0))],
        out_specs=[pl.BlockSpec((TM, D_MODEL), lambda i: (i, 0)),
                   pl.BlockSpec((TM, META_LANES), lambda i: (i, 0)),
                   pl.BlockSpec((8, META_LANES), lambda i: (0, 0))],
        out_shape=[jax.ShapeDtypeStruct((T_LAT, D_MODEL), F32),
                   jax.ShapeDtypeStruct((T_LAT, META_LANES), F32),
                   jax.ShapeDtypeStruct((8, META_LANES), F32)],
        scratch_shapes=[pltpu.VMEM((1, META_LANES), F32)],
        compiler_params=_cparams(("arbitrary",)),
        name="moe_route",
    )(h, g, mod, mod, w_router, b_router)


def _row_copy(src, s, dst, d, sem):
    return pltpu.make_async_copy(src.at[pl.ds(s, 1)], dst.at[pl.ds(d, 1)], sem)


MOE_NJ = D_FF_EXPERT // TF
GATHER_ROWS_PER_STEP = TM // MOE_NJ


def _moe_ffn_kernel(te_ref, nu_ref, src_ref, u_ref, wg_ref, wu_ref, wd_ref, out_ref, xg_scr, xb_scr, sem):
    i, j = pl.program_id(0), pl.program_id(1)
    n_used = nu_ref[0]
    used = i < n_used
    last = n_used - 1

    def start_row(tile, r):
        _row_copy(u_ref, src_ref[tile * TM + r], xg_scr, r, sem.at[0]).start()

    def wait_tile():
        def wait(r, c):
            _row_copy(u_ref, 0, xg_scr, 0, sem.at[0]).wait()
            return c
        lax.fori_loop(0, TM, wait, 0, unroll=8)

    @pl.when((i == 0) & (j == 0))
    def _():
        lax.fori_loop(0, TM, lambda r, c: (start_row(0, r), c)[1], 0)

    @pl.when(used & (j == 0))
    def _():
        wait_tile()
        xb_scr[...] = xg_scr[...].astype(BF16)
        nxt = jnp.minimum(i + 1, last)
        for r in range(GATHER_ROWS_PER_STEP * MOE_NJ, TM):
            start_row(nxt, r)

    @pl.when(j == 0)
    def _():
        out_ref[...] = jnp.zeros_like(out_ref)

    @pl.when(used)
    def _():
        nxt = jnp.minimum(i + 1, last)
        for r in range(GATHER_ROWS_PER_STEP):
            start_row(nxt, r * MOE_NJ + j)
        _swiglu_accumulate(xb_scr[...], wg_ref, wu_ref, wd_ref, out_ref)

    @pl.when((i == last) & (j == MOE_NJ - 1))
    def _():
        wait_tile()


def _moe_ffn(u, src, tile_expert, n_used, w_gate, w_up, w_down):
    def jj(i, j, nu):
        return jnp.where(i < nu[0], j, MOE_NJ - 1)

    grid_spec = pltpu.PrefetchScalarGridSpec(
        num_scalar_prefetch=3,
        grid=(MOE_TILES, MOE_NJ),
        in_specs=[pl.BlockSpec(memory_space=pl.ANY),
                  pl.BlockSpec((None, D_MODEL, TF), lambda i, j, te, nu, src: (te[i], 0, jj(i, j, nu))),
                  pl.BlockSpec((None, D_MODEL, TF), lambda i, j, te, nu, src: (te[i], 0, jj(i, j, nu))),
                  pl.BlockSpec((None, TF, D_MODEL), lambda i, j, te, nu, src: (te[i], jj(i, j, nu), 0))],
        out_specs=pl.BlockSpec((TM, D_MODEL), lambda i, j, te, nu, src: (i, 0)),
        scratch_shapes=[pltpu.VMEM((TM, D_MODEL), F32), pltpu.VMEM((TM, D_MODEL), BF16),
                        pltpu.SemaphoreType.DMA((1,))],
    )
    return pl.pallas_call(
        _moe_ffn_kernel,
        grid_spec=grid_spec,
        out_shape=jax.ShapeDtypeStruct((MOE_ROWS, D_MODEL), F32),
        compiler_params=_cparams(("arbitrary", "arbitrary")),
        name="moe_ffn",
    )(tile_expert, n_used, src, u, w_gate, w_up, w_down)


def _combine_kernel(pos_ref, h_ref, meta_ref, gt_ref, gf_ref, ys_ref, out_ref, y1, y2, sem):
    i = pl.program_id(0)
    slot = i % 2

    def gather(tile, s):
        def body(r, c):
            _row_copy(ys_ref, pos_ref[tile * TC + r], y1.at[s], r, sem.at[s, 0]).start()
            _row_copy(ys_ref, pos_ref[T_LAT + tile * TC + r], y2.at[s], r, sem.at[s, 1]).start()
            return c
        lax.fori_loop(0, TC, body, 0)

    @pl.when(i == 0)
    def _():
        gather(0, 0)

    @pl.when(i + 1 < pl.num_programs(0))
    def _():
        gather(i + 1, 1 - slot)

    def wait(r, c):
        _row_copy(ys_ref, 0, y1.at[slot], 0, sem.at[slot, 0]).wait()
        _row_copy(ys_ref, 0, y2.at[slot], 0, sem.at[slot, 1]).wait()
        return c

    lax.fori_loop(0, TC, wait, 0)

    meta = meta_ref[...]
    y = meta[:, 4:5] * y1[slot] + meta[:, 5:6] * y2[slot]
    hn = h_ref[...] + gt_ref[0] * y
    out_ref[...] = _rms(hn) * gf_ref[...]


def _moe_combine(pos, h, meta, mod, g_final, ys):
    grid_spec = pltpu.PrefetchScalarGridSpec(
        num_scalar_prefetch=1,
        grid=(T_LAT // TC,),
        in_specs=[pl.BlockSpec((TC, D_MODEL), lambda i, p: (i, 0)),
                  pl.BlockSpec((TC, META_LANES), lambda i, p: (i, 0)),
                  pl.BlockSpec((1, 1, D_MODEL), lambda i, p: (i // (SEQ // TC), 0, 5)),
                  pl.BlockSpec((1, D_MODEL), lambda i, p: (0, 0)),
                  pl.BlockSpec(memory_space=pl.ANY)],
        out_specs=pl.BlockSpec((TC, D_MODEL), lambda i, p: (i, 0)),
        scratch_shapes=[pltpu.VMEM((2, TC, D_MODEL), F32), pltpu.VMEM((2, TC, D_MODEL), F32),
                        pltpu.SemaphoreType.DMA((2, 2))],
    )
    return pl.pallas_call(
        _combine_kernel,
        grid_spec=grid_spec,
        out_shape=jax.ShapeDtypeStruct((T_LAT, D_MODEL), F32),
        compiler_params=_cparams(("arbitrary",)),
        name="moe_combine",
    )(pos, h, meta, mod, g_final, ys)


def _moe_plan(meta, counts_f):
    i1 = meta[:, 0].astype(jnp.int32)
    i2 = meta[:, 1].astype(jnp.int32)
    r1 = meta[:, 2].astype(jnp.int32)
    r2 = meta[:, 3].astype(jnp.int32)
    counts = counts_f[0, :N_EXPERTS].astype(jnp.int32)
    padded = (counts + TM - 1) // TM * TM
    ends = jnp.cumsum(padded)
    starts = ends - padded
    experts = jnp.arange(N_EXPERTS, dtype=jnp.int32)

    def start_of(idx):
        return jnp.sum(jnp.where(idx[:, None] == experts, starts, 0), axis=1)

    pos = jnp.concatenate([start_of(i1) + r1, start_of(i2) + r2])
    n_used = ends[-1] // TM
    tile_start = jnp.arange(MOE_TILES, dtype=jnp.int32) * TM
    expert = jnp.minimum(jnp.sum(tile_start[:, None] >= ends[None, :], axis=1), N_EXPERTS - 1)
    expert = expert[jnp.minimum(jnp.arange(MOE_TILES), n_used - 1)].astype(jnp.int32)
    token = jnp.arange(T_LAT, dtype=jnp.int32)
    src = jnp.zeros((MOE_ROWS,), jnp.int32).at[pos].set(jnp.concatenate([token, token]), unique_indices=True)
    return pos.astype(jnp.int32), src, expert, n_used.reshape(1).astype(jnp.int32)


def _rope_tables():
    t = jnp.arange(SEQ, dtype=jnp.int32)
    row = (t // GRID_W).astype(F32)
    col = (t % GRID_W).astype(F32)
    out = {}
    for dim, name in ((MLA_ROPE, "64"), (HEAD, "128")):
        half = dim // 2
        freqs = ROPE_THETA ** (-jnp.arange(0, half, 2, dtype=F32) / half)
        ar, ac = row[:, None] * freqs, col[:, None] * freqs
        c = jnp.concatenate([jnp.cos(ar), jnp.cos(ar), jnp.cos(ac), jnp.cos(ac)], axis=1)
        s = jnp.concatenate([-jnp.sin(ar), jnp.sin(ar), -jnp.sin(ac), jnp.sin(ac)], axis=1)
        c = jnp.tile(c, (1, HEAD // dim))
        s = jnp.tile(s, (1, HEAD // dim))
        out["c" + name] = jnp.concatenate([c, jnp.ones((TP, HEAD), F32)], axis=0)
        out["s" + name] = jnp.concatenate([s, jnp.zeros((TP, HEAD), F32)], axis=0)
    return out


def _na_bias_table(rel_bias):
    qc = np.arange(GRID_W)[:, None]
    kc = np.arange(GRID_W)[None, :]
    cs = np.clip(qc - NA_KW // 2, 0, GRID_W - NA_KW)
    col_ok = (kc >= cs) & (kc < cs + NA_KW)
    col_off = np.clip(kc - qc + NA_KW - 1, 0, 2 * NA_KW - 2)
    tab = jnp.where(jnp.asarray(col_ok), rel_bias[:, :, col_off] * LOG2E, NEG_INF)
    tab = jnp.concatenate([tab, jnp.full((NA_HEADS, 1, GRID_W, GRID_W), NEG_INF, F32)], axis=1)
    return jnp.concatenate([tab, tab], axis=-1)


def _layer_weights(l, w_in, mla_g_q, mla_w_qup, mla_g_kv, mla_w_kvup, gqa_g_q, gqa_g_k):
    sizes = (512, 512, 512, 384, 128, 64, 512, 256, 256, 512, 512, 512)
    parts = jnp.split(w_in[l], np.cumsum(sizes)[:-1].tolist(), axis=1)
    order = (0, 1, 2, 3, 4, 6, 7, 8, 9, 10, 11, 5)
    pad = jnp.zeros((D_MODEL, IN_COLS_PAD - sum(sizes)), F32)
    w_in_p = jnp.concatenate([parts[k] for k in order] + [pad], axis=1).astype(BF16)

    qup = mla_w_qup[l].reshape(MLA_Q_RANK, MLA_HEADS, MLA_NOPE + MLA_ROPE)
    qup = jnp.pad(qup, ((0, 0), (0, 0), (0, 2 * HEAD - MLA_NOPE - MLA_ROPE)))
    kvup = mla_w_kvup[l].reshape(MLA_KV_RANK, MLA_HEADS, MLA_NOPE + MLA_V)
    return {
        "w_in": w_in_p,
        "g_cq": mla_g_q[l].reshape(1, -1),
        "w_qup": qup.reshape(MLA_Q_RANK, MLA_HEADS * 2 * HEAD).astype(BF16),
        "g_ckv": mla_g_kv[l].reshape(1, -1),
        "w_kvk": kvup[:, :, :MLA_NOPE].reshape(MLA_KV_RANK, -1).astype(BF16),
        "w_kvv": kvup[:, :, MLA_NOPE:].reshape(MLA_KV_RANK, -1).astype(BF16),
        "g_gq": gqa_g_q[l].reshape(1, -1),
        "g_gk": gqa_g_k[l].reshape(1, -1),
    }


def kernel(x, c, ctx, c_ctx, w_mod, b_mod, g_mix, w_in, na_rel_bias, mla_g_q, mla_w_qup, mla_g_kv, mla_w_kvup,
           gqa_g_q, gqa_g_k, diff_lq1, diff_lk1, diff_lq2, diff_lk2, diff_g_sub, w_out, g_ffn,
           ffn_w_gate, ffn_w_up, ffn_w_down, moe_w_router, moe_b_router, moe_w_gate, moe_w_up, moe_w_down,
           g_final):
    tabs = _rope_tables()
    cvec = jnp.concatenate([c, c_ctx[None, :], jnp.zeros((16 - BATCH - 1, D_MODEL), F32)], axis=0)
    mod_all = _modulation(cvec, w_mod, b_mod)
    stream = (x.reshape(T_LAT, D_MODEL), ctx.reshape(T_CTX, D_MODEL))

    out = None
    for l in range(DEPTH):
        last = l == DEPTH - 1
        with_ctx = not last
        lam_init = 0.8 - 0.6 * math.exp(-0.3 * l)
        mod = mod_all[l].reshape(16, 1, 6 * D_MODEL)
        lw = _layer_weights(l, w_in, mla_g_q, mla_w_qup, mla_g_kv, mla_w_kvup, gqa_g_q, gqa_g_k)

        proj = _in_proj(stream, g_mix[l].reshape(1, -1), mod, lw["w_in"])
        q_mla, k_mla, v_mla, q_gqa, k_gqa, q_diff, k_diff = _prep(proj, lw, tabs)

        o_na = _na_attention(proj, _na_bias_table(na_rel_bias[l]), with_ctx)
        o_mla = _global_attention(q_mla, 2 * HEAD, k_mla, 2 * HEAD, MLA_HEADS, 0, v_mla, 0, with_ctx)
        o_gqa = _global_attention(q_gqa, HEAD, k_gqa, HEAD, GQA_KV_HEADS, 0,
                                  proj, COL_GQA_V // (GQA_KV_HEADS * HEAD), with_ctx)
        lam_vecs = jnp.stack([diff_lq1[l], diff_lk1[l], diff_lq2[l], diff_lk2[l]])
        o_diff = _global_attention(q_diff, HEAD, k_diff, HEAD, DIFF_HEADS, 0,
                                   proj, COL_DIFF_V // (DIFF_HEADS * HEAD), with_ctx,
                                   diff=(lam_vecs, diff_g_sub[l].reshape(1, -1), lam_init))

        h = _out_proj((o_na, o_mla, o_gqa, o_diff), w_out[l].astype(BF16), stream, mod,
                      ROWS if with_ctx else T_LAT)

        if l % 2 == 0:
            h = _ffn_dense(h, g_ffn[l].reshape(1, -1), mod, ffn_w_gate[l // 2], ffn_w_up[l // 2],
                           ffn_w_down[l // 2])
            stream = (h,)
        else:
            m = l // 2
            w_r = jnp.pad(moe_w_router[m], ((0, 0), (0, META_LANES - N_EXPERTS)))
            b_r = jnp.pad(moe_b_router[m], (0, META_LANES - N_EXPERTS)).reshape(1, -1)
            u, meta, counts = _moe_route(h, g_ffn[l].reshape(1, -1), mod, w_r, b_r)
            pos, src, tile_expert, n_used = _moe_plan(meta, counts)
            ys = _moe_ffn(u, src, tile_expert, n_used, moe_w_gate[m], moe_w_up[m], moe_w_down[m])
            out = _moe_combine(pos, h, meta, mod, g_final.reshape(1, -1), ys)
    return out.reshape(BATCH, SEQ, D_MODEL)
```

```python
import functools
import math

import numpy as np
import jax
import jax.numpy as jnp
from jax import lax
from jax.experimental import pallas as pl
from jax.experimental.pallas import tpu as pltpu

F32 = jnp.float32
BF16 = jnp.bfloat16

D_MODEL = 2048
BATCH = 8
SEQ = 2048
DEPTH = 2
GRID_W = 64
GRID_H = SEQ // GRID_W
CTX_LEN = 256
ROPE_THETA = 10000.0
EPS = 1e-6
NEG_INF = -1e30

NA_HEADS = 4
NA_KH = 8
NA_KW = 16
MLA_HEADS = 4
MLA_NOPE = 128
MLA_ROPE = 64
MLA_V = 128
MLA_Q_RANK = 384
MLA_KV_RANK = 128
GQA_HEADS = 4
GQA_KV_HEADS = 2
DIFF_HEADS = 4
DIFF_QK = 64
HEAD = 128

LOG2E = math.log2(math.e)
NA_SCALE = HEAD ** -0.5 * LOG2E
MLA_SCALE = (MLA_NOPE + MLA_ROPE) ** -0.5 * LOG2E
GQA_SCALE = HEAD ** -0.5 * LOG2E
DIFF_SCALE = DIFF_QK ** -0.5 * LOG2E

D_FF = 5632
N_EXPERTS = 8
D_FF_EXPERT = 7168

T_LAT = BATCH * SEQ
T_CTX = BATCH * CTX_LEN
ROWS = T_LAT + T_CTX

COL_NA_Q, COL_NA_K, COL_NA_V = 0, 512, 1024
COL_MLA_C = 1536
COL_GQA_Q, COL_GQA_K, COL_GQA_V = 2048, 2560, 2816
COL_DIFF_Q, COL_DIFF_K, COL_DIFF_V = 3072, 3584, 4096
COL_KPE = 4608
IN_COLS_PAD = 5120

V7X_VMEM_LIMIT = 56 * 1024 * 1024

TM = 1024
N_LAT_TILES = T_LAT // TM
N_ALL_TILES = ROWS // TM
TILES_PER_BATCH = SEQ // TM
TQ = 256
NQ = SEQ // TQ
TP = 512
TN_IN = 512
TM_O = 512
TF = 256
MOE_TILES = 2 * T_LAT // TM + N_EXPERTS
MOE_ROWS = MOE_TILES * TM
TC = 512


def _cparams(sem, vmem=V7X_VMEM_LIMIT):
    return pltpu.CompilerParams(dimension_semantics=sem, vmem_limit_bytes=vmem)


def _mod_row(i, tm):
    return jnp.where(i < T_LAT // tm, i // (SEQ // tm), BATCH)


def _dot(a, b):
    return jnp.dot(a, b, preferred_element_type=F32)


def _dot_t(a, b):
    return lax.dot_general(a, b, (((1,), (1,)), ((), ())), preferred_element_type=F32)


def _rms(x):
    return x * lax.rsqrt(jnp.mean(x * x, axis=-1, keepdims=True) + EPS)


def _silu(x):
    return x * jax.nn.sigmoid(x)


def _swiglu_accumulate(x, wg_ref, wu_ref, wd_ref, out_ref):
    rows = x.shape[0]
    a = (_silu(_dot(x, wg_ref[...].astype(BF16))) * _dot(x, wu_ref[...].astype(BF16))).astype(BF16)
    for c in range(0, D_MODEL, 512):
        out_ref[:rows, c:c + 512] += _dot(a, wd_ref[:, c:c + 512].astype(BF16))


def _mod_kernel(c_ref, w_ref, b_ref, o_ref):
    s = _silu(c_ref[...]).astype(BF16)
    o_ref[0] = _dot(s, w_ref[0].astype(BF16)) + b_ref[0]


def _modulation(cvec, w_mod, b_mod):
    tn = 1024
    return pl.pallas_call(
        _mod_kernel,
        grid=(DEPTH, 6 * D_MODEL // tn),
        in_specs=[pl.BlockSpec((16, D_MODEL), lambda l, j: (0, 0)),
                  pl.BlockSpec((1, D_MODEL, tn), lambda l, j: (l, 0, j)),
                  pl.BlockSpec((1, 1, tn), lambda l, j: (l, 0, j))],
        out_specs=pl.BlockSpec((1, 16, tn), lambda l, j: (l, 0, j)),
        out_shape=jax.ShapeDtypeStruct((DEPTH, 16, 6 * D_MODEL), F32),
        compiler_params=_cparams(("arbitrary", "arbitrary")),
        name="modulation",
    )(cvec, w_mod, b_mod.reshape(DEPTH, 1, 6 * D_MODEL))


def _stream_specs(stream, tm):
    if len(stream) == 1:
        return [pl.BlockSpec((tm, D_MODEL), lambda i, *_: (i, 0))]
    n_lat = T_LAT // tm
    return [pl.BlockSpec((tm, D_MODEL), lambda i, *_: (jnp.minimum(i, n_lat - 1), 0)),
            pl.BlockSpec((tm, D_MODEL), lambda i, *_: (jnp.maximum(i - n_lat, 0), 0))]


def _with_stream_rows(h_refs, tm, fn):
    if len(h_refs) == 1:
        fn(h_refs[0])
        return
    i = pl.program_id(0)
    n_lat = T_LAT // tm
    pl.when(i < n_lat)(lambda: fn(h_refs[0]))
    pl.when(i >= n_lat)(lambda: fn(h_refs[1]))


def _in_proj_kernel(*refs):
    *h_refs, g_ref, sh_ref, sc_ref, w_ref, o_ref, u_scr = refs

    def norm(h_ref):
        y = _rms(h_ref[...]) * g_ref[...]
        u_scr[...] = (y * (1.0 + sc_ref[0]) + sh_ref[0]).astype(BF16)

    @pl.when(pl.program_id(1) == 0)
    def _():
        _with_stream_rows(h_refs, TM, norm)

    o_ref[...] = _dot(u_scr[...], w_ref[...]).astype(o_ref.dtype)


def _in_proj(stream, g, mod, w_in):
    return pl.pallas_call(
        _in_proj_kernel,
        grid=(N_ALL_TILES, IN_COLS_PAD // TN_IN),
        in_specs=_stream_specs(stream, TM) + [
                  pl.BlockSpec((1, D_MODEL), lambda i, j: (0, 0)),
                  pl.BlockSpec((1, 1, D_MODEL), lambda i, j: (_mod_row(i, TM), 0, 0)),
                  pl.BlockSpec((1, 1, D_MODEL), lambda i, j: (_mod_row(i, TM), 0, 1)),
                  pl.BlockSpec((D_MODEL, TN_IN), lambda i, j: (0, j))],
        out_specs=pl.BlockSpec((TM, TN_IN), lambda i, j: (i, j)),
        out_shape=jax.ShapeDtypeStruct((ROWS, IN_COLS_PAD), BF16),
        scratch_shapes=[pltpu.VMEM((TM, D_MODEL), BF16)],
        compiler_params=_cparams(("arbitrary", "arbitrary")),
        name="in_proj",
    )(*stream, g, mod, mod, w_in)


def _rope(x, c, s, q):
    w = x.shape[-1]
    lane = lax.broadcasted_iota(jnp.int32, x.shape, 1)
    partner = jnp.where((lane & q) == 0, pltpu.roll(x, w - q, 1), pltpu.roll(x, q, 1))
    return x * c + partner * s


def _prep_kernel(mc_ref, gq_ref, gk_ref, dq_ref, dk_ref, kpe_ref,
                 g_cq_ref, w_qup_ref, g_ckv_ref, w_kvk_ref, w_kvv_ref, g_gq_ref, g_gk_ref,
                 c64_ref, s64_ref, c128_ref, s128_ref,
                 qm_ref, km_ref, vm_ref, qg_ref, kg_ref, qd_ref, kd_ref):
    c64, s64 = c64_ref[...], s64_ref[...]
    c128, s128 = c128_ref[...], s128_ref[...]
    q64, q128 = MLA_ROPE // 4, HEAD // 4

    mc = mc_ref[...].astype(F32)
    cq = (_rms(mc[:, :MLA_Q_RANK]) * g_cq_ref[...]).astype(BF16)
    ckv = (_rms(mc[:, MLA_Q_RANK:]) * g_ckv_ref[...]).astype(BF16)
    q = _dot(cq, w_qup_ref[...])
    kn = _dot(ckv, w_kvk_ref[...])
    vm_ref[...] = _dot(ckv, w_kvv_ref[...]).astype(BF16)
    kpe = _rope(kpe_ref[...].astype(F32), c64, s64, q64).astype(BF16)
    for h in range(MLA_HEADS):
        lo = 2 * HEAD * h
        qm_ref[:, lo:lo + HEAD] = (q[:, lo:lo + HEAD] * MLA_SCALE).astype(BF16)
        pe = _rope(q[:, lo + HEAD:lo + 2 * HEAD], c64, s64, q64)
        qm_ref[:, lo + HEAD:lo + 2 * HEAD] = (pe * MLA_SCALE).astype(BF16)
        km_ref[:, lo:lo + HEAD] = kn[:, h * HEAD:(h + 1) * HEAD].astype(BF16)
        km_ref[:, lo + HEAD:lo + 2 * HEAD] = kpe

    for h in range(GQA_HEADS):
        sl = slice(h * HEAD, (h + 1) * HEAD)
        z = _rms(gq_ref[:, sl].astype(F32)) * g_gq_ref[...]
        qg_ref[:, sl] = (_rope(z, c128, s128, q128) * GQA_SCALE).astype(BF16)
    for h in range(GQA_KV_HEADS):
        sl = slice(h * HEAD, (h + 1) * HEAD)
        z = _rms(gk_ref[:, sl].astype(F32)) * g_gk_ref[...]
        kg_ref[:, sl] = _rope(z, c128, s128, q128).astype(BF16)

    for h in range(DIFF_HEADS):
        sl = slice(h * HEAD, (h + 1) * HEAD)
        qd_ref[:, sl] = (_rope(dq_ref[:, sl].astype(F32), c64, s64, q64) * DIFF_SCALE).astype(BF16)
        kd_ref[:, sl] = _rope(dk_ref[:, sl].astype(F32), c64, s64, q64).astype(BF16)


def _prep(proj, lw, tabs):
    n_tiles = ROWS // TP
    lat_tiles = T_LAT // TP
    per_batch = SEQ // TP

    def col(width, start):
        return pl.BlockSpec((TP, width), lambda i: (i, start // width))

    def full(shape):
        return pl.BlockSpec(shape, lambda i: (0,) * len(shape))

    def tab():
        return pl.BlockSpec((TP, HEAD), lambda i: (jnp.where(i < lat_tiles, i % per_batch, per_batch), 0))

    def out(width):
        return pl.BlockSpec((TP, width), lambda i: (i, 0))

    widths = (1024, 1024, 512, 512, 256, 512, 512)
    return pl.pallas_call(
        _prep_kernel,
        grid=(n_tiles,),
        in_specs=[col(512, COL_MLA_C), col(512, COL_GQA_Q), col(256, COL_GQA_K),
                  col(512, COL_DIFF_Q), col(512, COL_DIFF_K), col(128, COL_KPE),
                  full((1, MLA_Q_RANK)), full((MLA_Q_RANK, 1024)), full((1, MLA_KV_RANK)),
                  full((MLA_KV_RANK, 512)), full((MLA_KV_RANK, 512)), full((1, HEAD)), full((1, HEAD)),
                  tab(), tab(), tab(), tab()],
        out_specs=[out(w) for w in widths],
        out_shape=[jax.ShapeDtypeStruct((ROWS, w), BF16) for w in widths],
        compiler_params=_cparams(("arbitrary",)),
        name="prep",
    )(proj, proj, proj, proj, proj, proj,
      lw["g_cq"], lw["w_qup"], lw["g_ckv"], lw["w_kvk"], lw["w_kvv"], lw["g_gq"], lw["g_gk"],
      tabs["c64"], tabs["s64"], tabs["c128"], tabs["s128"])


N_HEADS = 4
N_KEYS = CTX_LEN + SEQ


def _attend(s_c, vc, s_l=None, vl=None):
    m = jnp.max(s_c, axis=-1, keepdims=True)
    if s_l is not None:
        m = jnp.maximum(m, jnp.max(s_l, axis=-1, keepdims=True))
    p_c = jnp.exp2(s_c - m)
    den = jnp.sum(p_c, axis=-1, keepdims=True)
    o = _dot(p_c.astype(BF16), vc)
    if s_l is not None:
        p_l = jnp.exp2(s_l - m)
        den = den + jnp.sum(p_l, axis=-1, keepdims=True)
        o = o + _dot(p_l.astype(BF16), vl)
    return o / den


def _attend_ones(q, k, v_ones):
    s = _dot_t(q, k)
    p = jnp.exp2(s - jnp.max(s, axis=-1, keepdims=True)).astype(BF16)
    o = _dot(p, v_ones)
    return o[:, :HEAD] / o[:, HEAD:HEAD + 1]


def _on_query_tiles(with_ctx, lat_fn, ctx_fn):
    if not with_ctx:
        lat_fn()
        return
    qt = pl.program_id(1)
    pl.when(qt < NQ)(lat_fn)
    pl.when(qt == NQ)(ctx_fn)


def _global_kernel(q_ref, kc_ref, vc_ref, kl_ref, vl_ref, *rest, with_ctx, n_kv, q_w, k_w, lam_init):
    if lam_init is None:
        o_ref, kcat, vcat = rest
    else:
        lam_ref, g_ref, o_ref, kcat, vcat = rest

    @pl.when(pl.program_id(1) == 0)
    def _():
        lane = lax.broadcasted_iota(jnp.int32, (N_KEYS, HEAD), 1)
        ones = jnp.where(lane == 0, 1.0, 0.0).astype(BF16)
        for kv in range(n_kv):
            kcat[kv, :CTX_LEN] = kc_ref[:, kv * k_w:(kv + 1) * k_w]
            kcat[kv, CTX_LEN:] = kl_ref[:, kv * k_w:(kv + 1) * k_w]
            vcat[kv, :CTX_LEN, :HEAD] = vc_ref[:, kv * HEAD:(kv + 1) * HEAD]
            vcat[kv, CTX_LEN:, :HEAD] = vl_ref[:, kv * HEAD:(kv + 1) * HEAD]
            vcat[kv, :, HEAD:] = ones

    if lam_init is not None:
        lv = lam_ref[...]
        lam = (jnp.exp(jnp.sum(lv[0:1] * lv[1:2], axis=-1, keepdims=True))
               - jnp.exp(jnp.sum(lv[2:3] * lv[3:4], axis=-1, keepdims=True)) + lam_init)

    def run(n_keys):
        for h in range(N_HEADS):
            kv = h // (N_HEADS // n_kv)
            k, v = kcat[kv, :n_keys], vcat[kv, :n_keys]
            q = q_ref[:, h * q_w:(h + 1) * q_w]
            if lam_init is None:
                o = _attend_ones(q, k, v)
            else:
                lane = lax.broadcasted_iota(jnp.int32, q.shape, 1)
                zero = jnp.zeros_like(q)
                o1 = _attend_ones(jnp.where(lane < DIFF_QK, q, zero), k, v)
                o2 = _attend_ones(jnp.where(lane >= DIFF_QK, q, zero), k, v)
                o = _rms(o1 - lam * o2) * g_ref[...] * (1.0 - lam_init)
            o_ref[:, h * HEAD:(h + 1) * HEAD] = o.astype(o_ref.dtype)

    _on_query_tiles(with_ctx, lambda: run(N_KEYS), lambda: run(CTX_LEN))


def _q_row_block(b, qt):
    return jnp.where(qt < NQ, b * NQ + qt, T_LAT // TQ + b)


def _global_attention(q, q_w, k, k_w, n_kv, k_col, v, v_col, with_ctx, diff=None):
    n_qt = NQ + 1 if with_ctx else NQ
    out_rows = ROWS if with_ctx else T_LAT
    ctx_blk = T_LAT // CTX_LEN

    in_specs = [
        pl.BlockSpec((TQ, N_HEADS * q_w), lambda b, t: (_q_row_block(b, t), 0)),
        pl.BlockSpec((CTX_LEN, n_kv * k_w), lambda b, t: (ctx_blk + b, k_col)),
        pl.BlockSpec((CTX_LEN, n_kv * HEAD), lambda b, t: (ctx_blk + b, v_col)),
        pl.BlockSpec((SEQ, n_kv * k_w), lambda b, t: (b, k_col)),
        pl.BlockSpec((SEQ, n_kv * HEAD), lambda b, t: (b, v_col)),
    ]
    args = [q, k, v, k, v]
    lam_init = None
    if diff is not None:
        lam_vecs, g_sub, lam_init = diff
        in_specs += [pl.BlockSpec((4, DIFF_QK), lambda b, t: (0, 0)),
                     pl.BlockSpec((1, HEAD), lambda b, t: (0, 0))]
        args += [lam_vecs, g_sub]
    return pl.pallas_call(
        functools.partial(_global_kernel, with_ctx=with_ctx, n_kv=n_kv, q_w=q_w, k_w=k_w, lam_init=lam_init),
        grid=(BATCH, n_qt),
        in_specs=in_specs,
        out_specs=pl.BlockSpec((TQ, N_HEADS * HEAD), lambda b, t: (_q_row_block(b, t), 0)),
        out_shape=jax.ShapeDtypeStruct((out_rows, N_HEADS * HEAD), BF16),
        scratch_shapes=[pltpu.VMEM((n_kv, N_KEYS, k_w), BF16), pltpu.VMEM((n_kv, N_KEYS, 2 * HEAD), BF16)],
        compiler_params=_cparams(("arbitrary",) * 2),
        name="diff_attention" if diff is not None else "global_attention",
    )(*args)


NA_QROWS = TQ // GRID_W
NA_WROWS = 12
NA_WIN = NA_WROWS * GRID_W
NA_MASKED = 2 * NA_KH - 1


def _na_kernel(q_ref, kc_ref, vc_ref, k_ref, v_ref, bias_ref, o_ref, bias_scr, *, with_ctx):
    def head(h):
        return slice(h * HEAD, (h + 1) * HEAD)

    def lat():
        r0 = pl.program_id(1) * NA_QROWS
        ws = jnp.clip(r0 - NA_KH // 2, 0, GRID_H - NA_WROWS)
        start = pl.multiple_of(ws * GRID_W, TQ)
        block = {}
        for i in range(NA_QROWS):
            qr = r0 + i
            rs = jnp.clip(qr - NA_KH // 2, 0, GRID_H - NA_KH)
            for j in range(NA_WROWS):
                kr = ws + j
                block[i, j] = jnp.where((kr >= rs) & (kr < rs + NA_KH), kr - qr + NA_KH - 1, NA_MASKED)
        for h in range(NA_HEADS):
            for (i, j), a in block.items():
                half = (j % 2) * GRID_W
                bias_scr[h, i * GRID_W:(i + 1) * GRID_W, j * GRID_W:(j + 1) * GRID_W] = (
                    bias_ref[h, a][:, half:half + GRID_W])
            q = q_ref[:, head(h)]
            kw = k_ref[pl.ds(start, NA_WIN), head(h)]
            vw = v_ref[pl.ds(start, NA_WIN), head(h)]
            s_l = _dot_t(q, kw) * NA_SCALE + bias_scr[h]
            s_c = _dot_t(q, kc_ref[:, head(h)]) * NA_SCALE
            o_ref[:, head(h)] = _attend(s_c, vc_ref[:, head(h)], s_l, vw).astype(o_ref.dtype)

    def ctx():
        for h in range(NA_HEADS):
            s_c = _dot_t(q_ref[:, head(h)], kc_ref[:, head(h)]) * NA_SCALE
            o_ref[:, head(h)] = _attend(s_c, vc_ref[:, head(h)]).astype(o_ref.dtype)

    _on_query_tiles(with_ctx, lat, ctx)


def _na_attention(proj, bias_tab, with_ctx):
    n_qt = NQ + 1 if with_ctx else NQ
    out_rows = ROWS if with_ctx else T_LAT
    ctx_blk = T_LAT // CTX_LEN
    width = NA_HEADS * HEAD
    kcol, vcol = COL_NA_K // width, COL_NA_V // width
    return pl.pallas_call(
        functools.partial(_na_kernel, with_ctx=with_ctx),
        grid=(BATCH, n_qt),
        in_specs=[pl.BlockSpec((TQ, width), lambda b, t: (_q_row_block(b, t), 0)),
                  pl.BlockSpec((CTX_LEN, width), lambda b, t: (ctx_blk + b, kcol)),
                  pl.BlockSpec((CTX_LEN, width), lambda b, t: (ctx_blk + b, vcol)),
                  pl.BlockSpec((SEQ, width), lambda b, t: (b, kcol)),
                  pl.BlockSpec((SEQ, width), lambda b, t: (b, vcol)),
                  pl.BlockSpec((NA_HEADS, 2 * NA_KH, GRID_W, HEAD), lambda b, t: (0, 0, 0, 0))],
        out_specs=pl.BlockSpec((TQ, width), lambda b, t: (_q_row_block(b, t), 0)),
        out_shape=jax.ShapeDtypeStruct((out_rows, width), BF16),
        scratch_shapes=[pltpu.VMEM((NA_HEADS, TQ, NA_WIN), F32)],
        compiler_params=_cparams(("arbitrary",) * 2),
        name="na_attention",
    )(proj, proj, proj, proj, proj, bias_tab)


def _out_proj_kernel(o0_ref, o1_ref, o2_ref, o3_ref, w_ref, gt_ref, *refs):
    *h_refs, out_ref = refs
    for c in range(0, D_MODEL, 512):
        acc = None
        for t, o_ref in enumerate((o0_ref, o1_ref, o2_ref, o3_ref)):
            part = _dot(o_ref[...], w_ref[t * 512:(t + 1) * 512, c:c + 512])
            acc = part if acc is None else acc + part
        out_ref[:, c:c + 512] = gt_ref[0, :, c:c + 512] * acc

    def residual(h_ref):
        out_ref[...] += h_ref[...]

    _with_stream_rows(h_refs, TM_O, residual)


def _out_proj(o_parts, w_out, stream, mod, n_rows):
    o_spec = pl.BlockSpec((TM_O, 512), lambda i: (i, 0))
    return pl.pallas_call(
        _out_proj_kernel,
        grid=(n_rows // TM_O,),
        in_specs=[o_spec, o_spec, o_spec, o_spec,
                  pl.BlockSpec((D_MODEL, D_MODEL), lambda i: (0, 0), pipeline_mode=pl.Buffered(1)),
                  pl.BlockSpec((1, 1, D_MODEL), lambda i: (_mod_row(i, TM_O), 0, 2))]
                 + _stream_specs(stream, TM_O),
        out_specs=pl.BlockSpec((TM_O, D_MODEL), lambda i: (i, 0)),
        out_shape=jax.ShapeDtypeStruct((n_rows, D_MODEL), F32),
        compiler_params=_cparams(("arbitrary",)),
        name="out_proj",
    )(*o_parts, w_out, mod, *stream)


def _ffn_kernel(h_ref, g_ref, sh_ref, sc_ref, gt_ref, wg_ref, wu_ref, wd_ref, out_ref, u_scr):
    j = pl.program_id(1)

    @pl.when(j == 0)
    def _():
        y = _rms(h_ref[...]) * g_ref[...]
        u_scr[...] = (y * (1.0 + sc_ref[0]) + sh_ref[0]).astype(BF16)
        out_ref[...] = jnp.zeros_like(out_ref)

    _swiglu_accumulate(u_scr[...], wg_ref, wu_ref, wd_ref, out_ref)

    @pl.when(j == pl.num_programs(1) - 1)
    def _():
        out_ref[...] = h_ref[...] + gt_ref[0] * out_ref[...]


def _ffn_dense(h, g, mod, w_gate, w_up, w_down):
    def mod_spec(chunk):
        return pl.BlockSpec((1, 1, D_MODEL), lambda i, j: (_mod_row(i, TM), 0, chunk))

    return pl.pallas_call(
        _ffn_kernel,
        grid=(N_ALL_TILES, D_FF // TF),
        in_specs=[pl.BlockSpec((TM, D_MODEL), lambda i, j: (i, 0), pipeline_mode=pl.Buffered(1)),
                  pl.BlockSpec((1, D_MODEL), lambda i, j: (0, 0)),
                  mod_spec(3), mod_spec(4), mod_spec(5),
                  pl.BlockSpec((D_MODEL, TF), lambda i, j: (0, j)),
                  pl.BlockSpec((D_MODEL, TF), lambda i, j: (0, j)),
                  pl.BlockSpec((TF, D_MODEL), lambda i, j: (j, 0))],
        out_specs=pl.BlockSpec((TM, D_MODEL), lambda i, j: (i, 0)),
        out_shape=jax.ShapeDtypeStruct((ROWS, D_MODEL), F32),
        scratch_shapes=[pltpu.VMEM((TM, D_MODEL), BF16)],
        compiler_params=_cparams(("arbitrary", "arbitrary")),
        name="ffn_dense",
    )(h, g, mod, mod, mod, w_gate, w_up, w_down)


META_LANES = 128


def _route_kernel(h_ref, g_ref, sh_ref, sc_ref, wr_ref, br_ref, u_ref, meta_ref, cnt_ref, carry):
    i = pl.program_id(0)

    @pl.when(i == 0)
    def _():
        carry[...] = jnp.zeros_like(carry)

    y = _rms(h_ref[...]) * g_ref[...]
    u = y * (1.0 + sc_ref[0]) + sh_ref[0]
    u_ref[...] = u

    w = wr_ref[...]
    u_hi = u.astype(BF16)
    u_lo = (u - u_hi.astype(F32)).astype(BF16)
    w_hi = w.astype(BF16)
    w_lo = (w - w_hi.astype(F32)).astype(BF16)
    logits = _dot(u_hi, w_hi) + (_dot(u_hi, w_lo) + _dot(u_lo, w_hi)) + br_ref[...]

    lane = lax.broadcasted_iota(jnp.int32, logits.shape, 1)
    logits = jnp.where(lane < N_EXPERTS, logits, NEG_INF)
    lane_f = lane.astype(F32)
    m1 = jnp.max(logits, axis=-1, keepdims=True)
    i1 = jnp.min(jnp.where(logits == m1, lane_f, float(META_LANES)), axis=-1, keepdims=True)
    pick1 = lane_f == i1
    rest = jnp.where(pick1, NEG_INF, logits)
    m2 = jnp.max(rest, axis=-1, keepdims=True)
    i2 = jnp.min(jnp.where(rest == m2, lane_f, float(META_LANES)), axis=-1, keepdims=True)
    pick2 = lane_f == i2
    e = jnp.exp(m2 - m1)
    g1 = 1.0 / (1.0 + e)
    g2 = e / (1.0 + e)

    onehot = jnp.where(pick1 | pick2, 1.0, 0.0)
    rr = lax.broadcasted_iota(jnp.int32, (TM, TM), 0)
    cc = lax.broadcasted_iota(jnp.int32, (TM, TM), 1)
    earlier = jnp.where(cc < rr, 1.0, 0.0).astype(BF16)
    before = _dot(earlier, onehot.astype(BF16)) + carry[...]
    r1 = jnp.sum(jnp.where(pick1, before, 0.0), axis=-1, keepdims=True)
    r2 = jnp.sum(jnp.where(pick2, before, 0.0), axis=-1, keepdims=True)
    carry[...] += jnp.sum(onehot, axis=0, keepdims=True)

    meta = jnp.zeros(logits.shape, F32)
    for k, val in enumerate((i1, i2, r1, r2, g1, g2)):
        meta = jnp.where(lane == k, val, meta)
    meta_ref[...] = meta
    cnt_ref[...] = jnp.broadcast_to(carry[...], cnt_ref.shape)


def _moe_route(h, g, mod, w_router, b_router):
    def mod_spec(chunk):
        return pl.BlockSpec((1, 1, D_MODEL), lambda i: (_mod_row(i, TM), 0, chunk))

    return pl.pallas_call(
        _route_kernel,
        grid=(N_LAT_TILES,),
        in_specs=[pl.BlockSpec((TM, D_MODEL), lambda i: (i, 0)),
                  pl.BlockSpec((1, D_MODEL), lambda i: (0, 0)),
                  mod_spec(3), mod_spec(4),
                  pl.BlockSpec((D_MODEL, META_LANES), lambda i: (0, 0)),
                  pl.BlockSpec((1, META_LANES), lambda i: (0, 0))],
        out_specs=[pl.BlockSpec((TM, D_MODEL), lambda i: (i, 0)),
                   pl.BlockSpec((TM, META_LANES), lambda i: (i, 0)),
                   pl.BlockSpec((8, META_LANES), lambda i: (0, 0))],
        out_shape=[jax.ShapeDtypeStruct((T_LAT, D_MODEL), F32),
                   jax.ShapeDtypeStruct((T_LAT, META_LANES), F32),
                   jax.ShapeDtypeStruct((8, META_LANES), F32)],
        scratch_shapes=[pltpu.VMEM((1, META_LANES), F32)],
        compiler_params=_cparams(("arbitrary",)),
        name="moe_route",
    )(h, g, mod, mod, w_router, b_router)


def _row_copy(src, s, dst, d, sem):
    return pltpu.make_async_copy(src.at[pl.ds(s, 1)], dst.at[pl.ds(d, 1)], sem)


MOE_NJ = D_FF_EXPERT // TF
GATHER_ROWS_PER_STEP = TM // MOE_NJ


TAIL_ROWS = range(GATHER_ROWS_PER_STEP * MOE_NJ, TM)
TOK_ROWS = 2 * T_LAT + TM
HALF = TM // 2


def _moe_ffn_kernel(te_ref, nu_ref, nv_ref, src_ref, dst_ref, u_ref, wg_ref, wu_ref, wd_ref, tok_ref,
                    xg_scr, xb_scr, acc, sem):
    i, j = pl.program_id(0), pl.program_id(1)
    n_used = nu_ref[0]
    used = i < n_used
    last = n_used - 1
    slot = i % 2
    nxt = jnp.minimum(i + 1, last)
    prev = jnp.maximum(i - 1, 0)

    def get_row(tile, r):
        _row_copy(u_ref, src_ref[tile * TM + r], xg_scr, r, sem.at[0]).start()

    def put_row(tile, s, r, first):
        d = jnp.where(first, 2 * T_LAT + r, dst_ref[tile * TM + r])
        _row_copy(acc.at[s], r, tok_ref, d, sem.at[1]).start()

    def wait_rows(get):
        def wait(r, c):
            if get:
                _row_copy(u_ref, 0, xg_scr, 0, sem.at[0]).wait()
            else:
                _row_copy(acc.at[0], 0, tok_ref, 0, sem.at[1]).wait()
            return c
        lax.fori_loop(0, TM, wait, 0, unroll=8)

    @pl.when((i == 0) & (j == 0))
    def _():
        lax.fori_loop(0, TM, lambda r, c: (get_row(0, r), c)[1], 0)
        acc[1] = jnp.zeros((TM, D_MODEL), F32)

    @pl.when(used & (j == 0))
    def _():
        wait_rows(True)
        xb_scr[...] = xg_scr[...].astype(BF16)
        acc[slot] = jnp.zeros((TM, D_MODEL), F32)
        for r in TAIL_ROWS:
            get_row(nxt, r)
            put_row(prev, 1 - slot, r, i == 0)

    def step(rows):
        for r in range(GATHER_ROWS_PER_STEP):
            get_row(nxt, r * MOE_NJ + j)
            put_row(prev, 1 - slot, r * MOE_NJ + j, i == 0)
        _swiglu_accumulate(xb_scr[:rows], wg_ref, wu_ref, wd_ref, acc.at[slot])

    pl.when(used & (nv_ref[i] > HALF))(lambda: step(TM))
    pl.when(used & (nv_ref[i] <= HALF))(lambda: step(HALF))

    @pl.when(used & (j == MOE_NJ - 1))
    def _():
        wait_rows(False)

    @pl.when((i == last) & (j == MOE_NJ - 1))
    def _():
        wait_rows(True)
        lax.fori_loop(0, TM, lambda r, c: (put_row(i, slot, r, False), c)[1], 0)
        wait_rows(False)


def _moe_ffn(u, src, dst, tile_expert, n_used, n_valid, w_gate, w_up, w_down):
    def jj(i, j, nu):
        return jnp.where(i < nu[0], j, MOE_NJ - 1)

    grid_spec = pltpu.PrefetchScalarGridSpec(
        num_scalar_prefetch=5,
        grid=(MOE_TILES, MOE_NJ),
        in_specs=[pl.BlockSpec(memory_space=pl.ANY),
                  pl.BlockSpec((None, D_MODEL, TF), lambda i, j, te, nu, *_: (te[i], 0, jj(i, j, nu))),
                  pl.BlockSpec((None, D_MODEL, TF), lambda i, j, te, nu, *_: (te[i], 0, jj(i, j, nu))),
                  pl.BlockSpec((None, TF, D_MODEL), lambda i, j, te, nu, *_: (te[i], jj(i, j, nu), 0))],
        out_specs=pl.BlockSpec(memory_space=pl.ANY),
        scratch_shapes=[pltpu.VMEM((TM, D_MODEL), F32), pltpu.VMEM((TM, D_MODEL), BF16),
                        pltpu.VMEM((2, TM, D_MODEL), F32), pltpu.SemaphoreType.DMA((2,))],
    )
    return pl.pallas_call(
        _moe_ffn_kernel,
        grid_spec=grid_spec,
        out_shape=jax.ShapeDtypeStruct((TOK_ROWS, D_MODEL), F32),
        compiler_params=_cparams(("arbitrary", "arbitrary")),
        name="moe_ffn",
    )(tile_expert, n_used, n_valid, src, dst, u, w_gate, w_up, w_down)


def _combine_kernel(h_ref, meta_ref, gt_ref, gf_ref, y1_ref, y2_ref, out_ref):
    meta = meta_ref[...]
    y = meta[:, 4:5] * y1_ref[...] + meta[:, 5:6] * y2_ref[...]
    hn = h_ref[...] + gt_ref[0] * y
    out_ref[...] = _rms(hn) * gf_ref[...]


def _moe_combine(h, meta, mod, g_final, tok):
    n = T_LAT // TC
    row = pl.BlockSpec((TC, D_MODEL), lambda i: (i, 0))
    return pl.pallas_call(
        _combine_kernel,
        grid=(n,),
        in_specs=[row,
                  pl.BlockSpec((TC, META_LANES), lambda i: (i, 0)),
                  pl.BlockSpec((1, 1, D_MODEL), lambda i: (i // (SEQ // TC), 0, 5)),
                  pl.BlockSpec((1, D_MODEL), lambda i: (0, 0)),
                  row,
                  pl.BlockSpec((TC, D_MODEL), lambda i: (n + i, 0))],
        out_specs=row,
        out_shape=jax.ShapeDtypeStruct((T_LAT, D_MODEL), F32),
        compiler_params=_cparams(("arbitrary",)),
        name="moe_combine",
    )(h, meta, mod, g_final, tok, tok)


def _moe_plan(meta, counts_f):
    i1 = meta[:, 0].astype(jnp.int32)
    i2 = meta[:, 1].astype(jnp.int32)
    r1 = meta[:, 2].astype(jnp.int32)
    r2 = meta[:, 3].astype(jnp.int32)
    counts = counts_f[0, :N_EXPERTS].astype(jnp.int32)
    padded = (counts + TM - 1) // TM * TM
    ends = jnp.cumsum(padded)
    starts = ends - padded
    experts = jnp.arange(N_EXPERTS, dtype=jnp.int32)

    def start_of(idx):
        return jnp.sum(jnp.where(idx[:, None] == experts, starts, 0), axis=1)

    pos = jnp.concatenate([start_of(i1) + r1, start_of(i2) + r2])
    n_used = ends[-1] // TM
    tile_start = jnp.arange(MOE_TILES, dtype=jnp.int32) * TM
    expert = jnp.minimum(jnp.sum(tile_start[:, None] >= ends[None, :], axis=1), N_EXPERTS - 1)
    n_valid = jnp.clip((starts + counts)[expert] - tile_start, 0, TM).astype(jnp.int32)
    expert = expert[jnp.minimum(jnp.arange(MOE_TILES), n_used - 1)].astype(jnp.int32)
    slot = jnp.arange(MOE_ROWS, dtype=jnp.int32)
    owner = jnp.full((MOE_ROWS,), -1, jnp.int32).at[pos].set(jnp.arange(2 * T_LAT, dtype=jnp.int32),
                                                             unique_indices=True)
    dst = jnp.where(owner >= 0, owner, 2 * T_LAT + slot % TM)
    src = jnp.where(owner >= 0, owner % T_LAT, 0)
    return src, dst, expert, n_used.reshape(1).astype(jnp.int32), n_valid


def _rope_tables():
    t = jnp.arange(SEQ, dtype=jnp.int32)
    row = (t // GRID_W).astype(F32)
    col = (t % GRID_W).astype(F32)
    out = {}
    for dim, name in ((MLA_ROPE, "64"), (HEAD, "128")):
        half = dim // 2
        freqs = ROPE_THETA ** (-jnp.arange(0, half, 2, dtype=F32) / half)
        ar, ac = row[:, None] * freqs, col[:, None] * freqs
        c = jnp.concatenate([jnp.cos(ar), jnp.cos(ar), jnp.cos(ac), jnp.cos(ac)], axis=1)
        s = jnp.concatenate([-jnp.sin(ar), jnp.sin(ar), -jnp.sin(ac), jnp.sin(ac)], axis=1)
        c = jnp.tile(c, (1, HEAD // dim))
        s = jnp.tile(s, (1, HEAD // dim))
        out["c" + name] = jnp.concatenate([c, jnp.ones((TP, HEAD), F32)], axis=0)
        out["s" + name] = jnp.concatenate([s, jnp.zeros((TP, HEAD), F32)], axis=0)
    return out


def _na_bias_table(rel_bias):
    qc = np.arange(GRID_W)[:, None]
    kc = np.arange(GRID_W)[None, :]
    cs = np.clip(qc - NA_KW // 2, 0, GRID_W - NA_KW)
    col_ok = (kc >= cs) & (kc < cs + NA_KW)
    col_off = np.clip(kc - qc + NA_KW - 1, 0, 2 * NA_KW - 2)
    pick = jnp.asarray(col_off[None] == np.arange(2 * NA_KW - 1)[:, None, None], F32)
    tab = jnp.einsum("had,dqk->haqk", rel_bias * LOG2E, pick, precision=lax.Precision.HIGHEST)
    tab = jnp.where(jnp.asarray(col_ok), tab, NEG_INF)
    tab = jnp.concatenate([tab, jnp.full((NA_HEADS, 1, GRID_W, GRID_W), NEG_INF, F32)], axis=1)
    return jnp.concatenate([tab, tab], axis=-1)


def _layer_weights(l, w_in, mla_g_q, mla_w_qup, mla_g_kv, mla_w_kvup, gqa_g_q, gqa_g_k):
    sizes = (512, 512, 512, 384, 128, 64, 512, 256, 256, 512, 512, 512)
    parts = jnp.split(w_in[l], np.cumsum(sizes)[:-1].tolist(), axis=1)
    order = (0, 1, 2, 3, 4, 6, 7, 8, 9, 10, 11, 5)
    pad = jnp.zeros((D_MODEL, IN_COLS_PAD - sum(sizes)), F32)
    w_in_p = jnp.concatenate([parts[k] for k in order] + [pad], axis=1).astype(BF16)

    qup = mla_w_qup[l].reshape(MLA_Q_RANK, MLA_HEADS, MLA_NOPE + MLA_ROPE)
    qup = jnp.pad(qup, ((0, 0), (0, 0), (0, 2 * HEAD - MLA_NOPE - MLA_ROPE)))
    kvup = mla_w_kvup[l].reshape(MLA_KV_RANK, MLA_HEADS, MLA_NOPE + MLA_V)
    return {
        "w_in": w_in_p,
        "g_cq": mla_g_q[l].reshape(1, -1),
        "w_qup": qup.reshape(MLA_Q_RANK, MLA_HEADS * 2 * HEAD).astype(BF16),
        "g_ckv": mla_g_kv[l].reshape(1, -1),
        "w_kvk": kvup[:, :, :MLA_NOPE].reshape(MLA_KV_RANK, -1).astype(BF16),
        "w_kvv": kvup[:, :, MLA_NOPE:].reshape(MLA_KV_RANK, -1).astype(BF16),
        "g_gq": gqa_g_q[l].reshape(1, -1),
        "g_gk": gqa_g_k[l].reshape(1, -1),
    }


def kernel(x, c, ctx, c_ctx, w_mod, b_mod, g_mix, w_in, na_rel_bias, mla_g_q, mla_w_qup, mla_g_kv, mla_w_kvup,
           gqa_g_q, gqa_g_k, diff_lq1, diff_lk1, diff_lq2, diff_lk2, diff_g_sub, w_out, g_ffn,
           ffn_w_gate, ffn_w_up, ffn_w_down, moe_w_router, moe_b_router, moe_w_gate, moe_w_up, moe_w_down,
           g_final):
    tabs = _rope_tables()
    cvec = jnp.concatenate([c, c_ctx[None, :], jnp.zeros((16 - BATCH - 1, D_MODEL), F32)], axis=0)
    mod_all = _modulation(cvec, w_mod, b_mod)
    stream = (x.reshape(T_LAT, D_MODEL), ctx.reshape(T_CTX, D_MODEL))

    out = None
    for l in range(DEPTH):
        last = l == DEPTH - 1
        with_ctx = not last
        lam_init = 0.8 - 0.6 * math.exp(-0.3 * l)
        mod = mod_all[l].reshape(16, 1, 6 * D_MODEL)
        lw = _layer_weights(l, w_in, mla_g_q, mla_w_qup, mla_g_kv, mla_w_kvup, gqa_g_q, gqa_g_k)

        proj = _in_proj(stream, g_mix[l].reshape(1, -1), mod, lw["w_in"])
        q_mla, k_mla, v_mla, q_gqa, k_gqa, q_diff, k_diff = _prep(proj, lw, tabs)

        o_na = _na_attention(proj, _na_bias_table(na_rel_bias[l]), with_ctx)
        o_mla = _global_attention(q_mla, 2 * HEAD, k_mla, 2 * HEAD, MLA_HEADS, 0, v_mla, 0, with_ctx)
        o_gqa = _global_attention(q_gqa, HEAD, k_gqa, HEAD, GQA_KV_HEADS, 0,
                                  proj, COL_GQA_V // (GQA_KV_HEADS * HEAD), with_ctx)
        lam_vecs = jnp.stack([diff_lq1[l], diff_lk1[l], diff_lq2[l], diff_lk2[l]])
        o_diff = _global_attention(q_diff, HEAD, k_diff, HEAD, DIFF_HEADS, 0,
                                   proj, COL_DIFF_V // (DIFF_HEADS * HEAD), with_ctx,
                                   diff=(lam_vecs, diff_g_sub[l].reshape(1, -1), lam_init))

        h = _out_proj((o_na, o_mla, o_gqa, o_diff), w_out[l].astype(BF16), stream, mod,
                      ROWS if with_ctx else T_LAT)

        if l % 2 == 0:
            h = _ffn_dense(h, g_ffn[l].reshape(1, -1), mod, ffn_w_gate[l // 2], ffn_w_up[l // 2],
                           ffn_w_down[l // 2])
            stream = (h,)
        else:
            m = l // 2
            w_r = jnp.pad(moe_w_router[m], ((0, 0), (0, META_LANES - N_EXPERTS)))
            b_r = jnp.pad(moe_b_router[m], (0, META_LANES - N_EXPERTS)).reshape(1, -1)
            u, meta, counts = _moe_route(h, g_ffn[l].reshape(1, -1), mod, w_r, b_r)
            src, dst, tile_expert, n_used, n_valid = _moe_plan(meta, counts)
            tok = _moe_ffn(u, src, dst, tile_expert, n_used, n_valid, moe_w_gate[m], moe_w_up[m], moe_w_down[m])
            out = _moe_combine(h, meta, mod, g_final.reshape(1, -1), tok)
    return out.reshape(BATCH, SEQ, D_MODEL)
```

```python
import functools
import math

import numpy as np
import jax
import jax.numpy as jnp
from jax import lax
from jax.experimental import pallas as pl
from jax.experimental.pallas import tpu as pltpu

F32 = jnp.float32
BF16 = jnp.bfloat16

D_MODEL = 2048
BATCH = 8
SEQ = 2048
DEPTH = 2
GRID_W = 64
GRID_H = SEQ // GRID_W
CTX_LEN = 256
ROPE_THETA = 10000.0
EPS = 1e-6
NEG_INF = -1e30

NA_HEADS = 4
NA_KH = 8
NA_KW = 16
MLA_HEADS = 4
MLA_NOPE = 128
MLA_ROPE = 64
MLA_V = 128
MLA_Q_RANK = 384
MLA_KV_RANK = 128
GQA_HEADS = 4
GQA_KV_HEADS = 2
DIFF_HEADS = 4
DIFF_QK = 64
HEAD = 128

LOG2E = math.log2(math.e)
NA_SCALE = HEAD ** -0.5 * LOG2E
MLA_SCALE = (MLA_NOPE + MLA_ROPE) ** -0.5 * LOG2E
GQA_SCALE = HEAD ** -0.5 * LOG2E
DIFF_SCALE = DIFF_QK ** -0.5 * LOG2E

D_FF = 5632
N_EXPERTS = 8
D_FF_EXPERT = 7168

T_LAT = BATCH * SEQ
T_CTX = BATCH * CTX_LEN
ROWS = T_LAT + T_CTX

COL_NA_Q, COL_NA_K, COL_NA_V = 0, 512, 1024
COL_MLA_C = 1536
COL_GQA_Q, COL_GQA_K, COL_GQA_V = 2048, 2560, 2816
COL_DIFF_Q, COL_DIFF_K, COL_DIFF_V = 3072, 3584, 4096
COL_KPE = 4608
IN_COLS_PAD = 5120

V7X_VMEM_LIMIT = 56 * 1024 * 1024

TM = 1024
N_LAT_TILES = T_LAT // TM
N_ALL_TILES = ROWS // TM
TILES_PER_BATCH = SEQ // TM
TQ = 256
NQ = SEQ // TQ
TP = 512
TN_IN = 1024
V7X_MXU_WIDTH = 256
IN_LAST_COLS = -(-(COL_KPE + HEAD - (IN_COLS_PAD - TN_IN)) // V7X_MXU_WIDTH) * V7X_MXU_WIDTH
TM_O = 512
TF = 256
MOE_TILES = 2 * T_LAT // TM + N_EXPERTS
MOE_ROWS = MOE_TILES * TM
TC = 512


def _cparams(sem, vmem=V7X_VMEM_LIMIT):
    return pltpu.CompilerParams(dimension_semantics=sem, vmem_limit_bytes=vmem)


def _mod_row(i, tm):
    return jnp.where(i < T_LAT // tm, i // (SEQ // tm), BATCH)


def _dot(a, b):
    return jnp.dot(a, b, preferred_element_type=F32)


def _dot_t(a, b):
    return lax.dot_general(a, b, (((1,), (1,)), ((), ())), preferred_element_type=F32)


def _rms(x):
    return x * lax.rsqrt(jnp.mean(x * x, axis=-1, keepdims=True) + EPS)


def _silu(x):
    return x * jax.nn.sigmoid(x)


def _swiglu_accumulate(x, wg_ref, wu_ref, wd_ref, out_ref):
    rows = x.shape[0]
    a = (_silu(_dot(x, wg_ref[...].astype(BF16))) * _dot(x, wu_ref[...].astype(BF16))).astype(BF16)
    for c in range(0, D_MODEL, 512):
        out_ref[:rows, c:c + 512] += _dot(a, wd_ref[:, c:c + 512].astype(BF16))


def _mod_kernel(c_ref, w_ref, b_ref, o_ref):
    s = _silu(c_ref[...]).astype(BF16)
    o_ref[0] = _dot(s, w_ref[0].astype(BF16)) + b_ref[0]


def _modulation(cvec, w_mod, b_mod):
    tn = 1024
    return pl.pallas_call(
        _mod_kernel,
        grid=(DEPTH, 6 * D_MODEL // tn),
        in_specs=[pl.BlockSpec((16, D_MODEL), lambda l, j: (0, 0)),
                  pl.BlockSpec((1, D_MODEL, tn), lambda l, j: (l, 0, j)),
                  pl.BlockSpec((1, 1, tn), lambda l, j: (l, 0, j))],
        out_specs=pl.BlockSpec((1, 16, tn), lambda l, j: (l, 0, j)),
        out_shape=jax.ShapeDtypeStruct((DEPTH, 16, 6 * D_MODEL), F32),
        compiler_params=_cparams(("arbitrary", "arbitrary")),
        name="modulation",
    )(cvec, w_mod, b_mod.reshape(DEPTH, 1, 6 * D_MODEL))


def _stream_specs(stream, tm):
    if len(stream) == 1:
        return [pl.BlockSpec((tm, D_MODEL), lambda i, *_: (i, 0))]
    n_lat = T_LAT // tm
    return [pl.BlockSpec((tm, D_MODEL), lambda i, *_: (jnp.minimum(i, n_lat - 1), 0)),
            pl.BlockSpec((tm, D_MODEL), lambda i, *_: (jnp.maximum(i - n_lat, 0), 0))]


def _with_stream_rows(h_refs, tm, fn):
    if len(h_refs) == 1:
        fn(h_refs[0])
        return
    i = pl.program_id(0)
    n_lat = T_LAT // tm
    pl.when(i < n_lat)(lambda: fn(h_refs[0]))
    pl.when(i >= n_lat)(lambda: fn(h_refs[1]))


def _in_proj_kernel(*refs):
    *h_refs, g_ref, sh_ref, sc_ref, w_ref, o_ref, u_scr = refs

    def norm(h_ref):
        y = _rms(h_ref[...]) * g_ref[...]
        u_scr[...] = (y * (1.0 + sc_ref[0]) + sh_ref[0]).astype(BF16)

    @pl.when(pl.program_id(1) == 0)
    def _():
        _with_stream_rows(h_refs, TM, norm)

    last = pl.num_programs(1) - 1

    @pl.when(pl.program_id(1) < last)
    def _():
        o_ref[...] = _dot(u_scr[...], w_ref[...]).astype(o_ref.dtype)

    @pl.when(pl.program_id(1) == last)
    def _():
        o_ref[:, :IN_LAST_COLS] = _dot(u_scr[...], w_ref[:, :IN_LAST_COLS]).astype(o_ref.dtype)
        o_ref[:, IN_LAST_COLS:] = jnp.zeros((TM, TN_IN - IN_LAST_COLS), o_ref.dtype)


def _in_proj(stream, g, mod, w_in):
    return pl.pallas_call(
        _in_proj_kernel,
        grid=(N_ALL_TILES, IN_COLS_PAD // TN_IN),
        in_specs=_stream_specs(stream, TM) + [
                  pl.BlockSpec((1, D_MODEL), lambda i, j: (0, 0)),
                  pl.BlockSpec((1, 1, D_MODEL), lambda i, j: (_mod_row(i, TM), 0, 0)),
                  pl.BlockSpec((1, 1, D_MODEL), lambda i, j: (_mod_row(i, TM), 0, 1)),
                  pl.BlockSpec((D_MODEL, TN_IN), lambda i, j: (0, j))],
        out_specs=pl.BlockSpec((TM, TN_IN), lambda i, j: (i, j)),
        out_shape=jax.ShapeDtypeStruct((ROWS, IN_COLS_PAD), BF16),
        scratch_shapes=[pltpu.VMEM((TM, D_MODEL), BF16)],
        compiler_params=_cparams(("arbitrary", "arbitrary")),
        name="in_proj",
    )(*stream, g, mod, mod, w_in)


def _rope_lane_order(dim):
    q, n_vec = dim // 4, HEAD // dim
    order = np.empty(HEAD, np.int64)
    for b in range(2):
        for v in range(n_vec):
            for a in range(2):
                for i in range(q):
                    order[b * (HEAD // 2) + v * 2 * q + a * q + i] = v * dim + a * 2 * q + b * q + i
    return order


def _rope(x, c, s):
    return x * c + pltpu.roll(x, HEAD // 2, 1) * s


def _prep_kernel(mc_ref, gq_ref, gk_ref, dq_ref, dk_ref, kpe_ref,
                 g_cq_ref, w_qup_ref, g_ckv_ref, w_kvk_ref, w_kvv_ref, g_gq_ref, g_gk_ref,
                 c64_ref, s64_ref, c128_ref, s128_ref,
                 qm_ref, km_ref, vm_ref, qg_ref, kg_ref, qd_ref, kd_ref):
    c64, s64 = c64_ref[...], s64_ref[...]
    c128, s128 = c128_ref[...], s128_ref[...]

    mc = mc_ref[...].astype(F32)
    cq = (_rms(mc[:, :MLA_Q_RANK]) * g_cq_ref[...]).astype(BF16)
    ckv = (_rms(mc[:, MLA_Q_RANK:]) * g_ckv_ref[...]).astype(BF16)
    q = _dot(cq, w_qup_ref[...])
    kn = _dot(ckv, w_kvk_ref[...])
    vm_ref[...] = _dot(ckv, w_kvv_ref[...]).astype(BF16)
    kpe = _rope(kpe_ref[...].astype(F32), c64, s64).astype(BF16)
    for h in range(MLA_HEADS):
        lo = 2 * HEAD * h
        qm_ref[:, lo:lo + HEAD] = (q[:, lo:lo + HEAD] * MLA_SCALE).astype(BF16)
        pe = _rope(q[:, lo + HEAD:lo + 2 * HEAD], c64, s64)
        qm_ref[:, lo + HEAD:lo + 2 * HEAD] = (pe * MLA_SCALE).astype(BF16)
        km_ref[:, lo:lo + HEAD] = kn[:, h * HEAD:(h + 1) * HEAD].astype(BF16)
        km_ref[:, lo + HEAD:lo + 2 * HEAD] = kpe

    for h in range(GQA_HEADS):
        sl = slice(h * HEAD, (h + 1) * HEAD)
        z = _rms(gq_ref[:, sl].astype(F32)) * g_gq_ref[...]
        qg_ref[:, sl] = (_rope(z, c128, s128) * GQA_SCALE).astype(BF16)
    for h in range(GQA_KV_HEADS):
        sl = slice(h * HEAD, (h + 1) * HEAD)
        z = _rms(gk_ref[:, sl].astype(F32)) * g_gk_ref[...]
        kg_ref[:, sl] = _rope(z, c128, s128).astype(BF16)

    for h in range(DIFF_HEADS):
        sl = slice(h * HEAD, (h + 1) * HEAD)
        qd_ref[:, sl] = (_rope(dq_ref[:, sl].astype(F32), c64, s64) * DIFF_SCALE).astype(BF16)
        kd_ref[:, sl] = _rope(dk_ref[:, sl].astype(F32), c64, s64).astype(BF16)


def _prep(proj, lw, tabs):
    n_tiles = ROWS // TP
    lat_tiles = T_LAT // TP
    per_batch = SEQ // TP

    def col(width, start):
        return pl.BlockSpec((TP, width), lambda i: (i, start // width))

    def full(shape):
        return pl.BlockSpec(shape, lambda i: (0,) * len(shape))

    def tab():
        return pl.BlockSpec((TP, HEAD), lambda i: (jnp.where(i < lat_tiles, i % per_batch, per_batch), 0))

    def out(width):
        return pl.BlockSpec((TP, width), lambda i: (i, 0))

    widths = (1024, 1024, 512, 512, 256, 512, 512)
    return pl.pallas_call(
        _prep_kernel,
        grid=(n_tiles,),
        in_specs=[col(512, COL_MLA_C), col(512, COL_GQA_Q), col(256, COL_GQA_K),
                  col(512, COL_DIFF_Q), col(512, COL_DIFF_K), col(128, COL_KPE),
                  full((1, MLA_Q_RANK)), full((MLA_Q_RANK, 1024)), full((1, MLA_KV_RANK)),
                  full((MLA_KV_RANK, 512)), full((MLA_KV_RANK, 512)), full((1, HEAD)), full((1, HEAD)),
                  tab(), tab(), tab(), tab()],
        out_specs=[out(w) for w in widths],
        out_shape=[jax.ShapeDtypeStruct((ROWS, w), BF16) for w in widths],
        compiler_params=_cparams(("arbitrary",)),
        name="prep",
    )(proj, proj, proj, proj, proj, proj,
      lw["g_cq"], lw["w_qup"], lw["g_ckv"], lw["w_kvk"], lw["w_kvv"], lw["g_gq"], lw["g_gk"],
      tabs["c64"], tabs["s64"], tabs["c128"], tabs["s128"])


N_HEADS = 4
N_KEYS = CTX_LEN + SEQ


def _attend(s_c, vc, s_l=None, vl=None):
    m = jnp.max(s_c, axis=-1, keepdims=True)
    if s_l is not None:
        m = jnp.maximum(m, jnp.max(s_l, axis=-1, keepdims=True))
    p_c = jnp.exp2(s_c - m)
    den = jnp.sum(p_c, axis=-1, keepdims=True)
    o = _dot(p_c.astype(BF16), vc)
    if s_l is not None:
        p_l = jnp.exp2(s_l - m)
        den = den + jnp.sum(p_l, axis=-1, keepdims=True)
        o = o + _dot(p_l.astype(BF16), vl)
    return o / den


def _attend_ones(q, k, v_ones):
    s = _dot_t(q, k)
    p = jnp.exp2(s - jnp.max(s, axis=-1, keepdims=True)).astype(BF16)
    o = _dot(p, v_ones)
    return o[:, :HEAD] / o[:, HEAD:HEAD + 1]


def _on_query_tiles(with_ctx, lat_fn, ctx_fn):
    if not with_ctx:
        lat_fn()
        return
    qt = pl.program_id(1)
    pl.when(qt < NQ)(lat_fn)
    pl.when(qt == NQ)(ctx_fn)


def _global_kernel(q_ref, kc_ref, vc_ref, kl_ref, vl_ref, *rest, with_ctx, n_kv, q_w, k_w, lam_init):
    if lam_init is None:
        o_ref, kcat, vcat = rest
    else:
        lam_ref, g_ref, o_ref, kcat, vcat = rest

    @pl.when(pl.program_id(1) == 0)
    def _():
        lane = lax.broadcasted_iota(jnp.int32, (N_KEYS, HEAD), 1)
        ones = jnp.where(lane == 0, 1.0, 0.0).astype(BF16)
        for kv in range(n_kv):
            kcat[kv, :CTX_LEN] = kc_ref[:, kv * k_w:(kv + 1) * k_w]
            kcat[kv, CTX_LEN:] = kl_ref[:, kv * k_w:(kv + 1) * k_w]
            vcat[kv, :CTX_LEN, :HEAD] = vc_ref[:, kv * HEAD:(kv + 1) * HEAD]
            vcat[kv, CTX_LEN:, :HEAD] = vl_ref[:, kv * HEAD:(kv + 1) * HEAD]
            vcat[kv, :, HEAD:] = ones

    if lam_init is not None:
        lv = lam_ref[...]
        lam = (jnp.exp(jnp.sum(lv[0:1] * lv[1:2], axis=-1, keepdims=True))
               - jnp.exp(jnp.sum(lv[2:3] * lv[3:4], axis=-1, keepdims=True)) + lam_init)

    def run(n_keys):
        for h in range(N_HEADS):
            kv = h // (N_HEADS // n_kv)
            k, v = kcat[kv, :n_keys], vcat[kv, :n_keys]
            q = q_ref[:, h * q_w:(h + 1) * q_w]
            if lam_init is None:
                o = _attend_ones(q, k, v)
            else:
                first = (lax.broadcasted_iota(jnp.int32, q.shape, 1) & (DIFF_QK // 2)) == 0
                zero = jnp.zeros_like(q)
                o1 = _attend_ones(jnp.where(first, q, zero), k, v)
                o2 = _attend_ones(jnp.where(first, zero, q), k, v)
                o = _rms(o1 - lam * o2) * g_ref[...] * (1.0 - lam_init)
            o_ref[:, h * HEAD:(h + 1) * HEAD] = o.astype(o_ref.dtype)

    _on_query_tiles(with_ctx, lambda: run(N_KEYS), lambda: run(CTX_LEN))


def _q_row_block(b, qt):
    return jnp.where(qt < NQ, b * NQ + qt, T_LAT // TQ + b)


def _global_attention(q, q_w, k, k_w, n_kv, k_col, v, v_col, with_ctx, diff=None):
    n_qt = NQ + 1 if with_ctx else NQ
    out_rows = ROWS if with_ctx else T_LAT
    ctx_blk = T_LAT // CTX_LEN

    in_specs = [
        pl.BlockSpec((TQ, N_HEADS * q_w), lambda b, t: (_q_row_block(b, t), 0)),
        pl.BlockSpec((CTX_LEN, n_kv * k_w), lambda b, t: (ctx_blk + b, k_col)),
        pl.BlockSpec((CTX_LEN, n_kv * HEAD), lambda b, t: (ctx_blk + b, v_col)),
        pl.BlockSpec((SEQ, n_kv * k_w), lambda b, t: (b, k_col)),
        pl.BlockSpec((SEQ, n_kv * HEAD), lambda b, t: (b, v_col)),
    ]
    args = [q, k, v, k, v]
    lam_init = None
    if diff is not None:
        lam_vecs, g_sub, lam_init = diff
        in_specs += [pl.BlockSpec((4, DIFF_QK), lambda b, t: (0, 0)),
                     pl.BlockSpec((1, HEAD), lambda b, t: (0, 0))]
        args += [lam_vecs, g_sub]
    return pl.pallas_call(
        functools.partial(_global_kernel, with_ctx=with_ctx, n_kv=n_kv, q_w=q_w, k_w=k_w, lam_init=lam_init),
        grid=(BATCH, n_qt),
        in_specs=in_specs,
        out_specs=pl.BlockSpec((TQ, N_HEADS * HEAD), lambda b, t: (_q_row_block(b, t), 0)),
        out_shape=jax.ShapeDtypeStruct((out_rows, N_HEADS * HEAD), BF16),
        scratch_shapes=[pltpu.VMEM((n_kv, N_KEYS, k_w), BF16), pltpu.VMEM((n_kv, N_KEYS, 2 * HEAD), BF16)],
        compiler_params=_cparams(("arbitrary",) * 2),
        name="diff_attention" if diff is not None else "global_attention",
    )(*args)


NA_QROWS = TQ // GRID_W
NA_WROWS = 12
NA_WIN = NA_WROWS * GRID_W
NA_MASKED = 2 * NA_KH - 1


def _na_kernel(q_ref, kc_ref, vc_ref, k_ref, v_ref, bias_ref, o_ref, bias_scr, *, with_ctx):
    def head(h):
        return slice(h * HEAD, (h + 1) * HEAD)

    def lat():
        r0 = pl.program_id(1) * NA_QROWS
        ws = jnp.clip(r0 - NA_KH // 2, 0, GRID_H - NA_WROWS)
        start = pl.multiple_of(ws * GRID_W, TQ)
        block = {}
        for i in range(NA_QROWS):
            qr = r0 + i
            rs = jnp.clip(qr - NA_KH // 2, 0, GRID_H - NA_KH)
            for j in range(NA_WROWS):
                kr = ws + j
                block[i, j] = jnp.where((kr >= rs) & (kr < rs + NA_KH), kr - qr + NA_KH - 1, NA_MASKED)
        for h in range(NA_HEADS):
            for (i, j), a in block.items():
                half = (j % 2) * GRID_W
                bias_scr[h, i * GRID_W:(i + 1) * GRID_W, j * GRID_W:(j + 1) * GRID_W] = (
                    bias_ref[h, a][:, half:half + GRID_W])
            q = q_ref[:, head(h)]
            kw = k_ref[pl.ds(start, NA_WIN), head(h)]
            vw = v_ref[pl.ds(start, NA_WIN), head(h)]
            s_l = _dot_t(q, kw) * NA_SCALE + bias_scr[h]
            s_c = _dot_t(q, kc_ref[:, head(h)]) * NA_SCALE
            o_ref[:, head(h)] = _attend(s_c, vc_ref[:, head(h)], s_l, vw).astype(o_ref.dtype)

    def ctx():
        for h in range(NA_HEADS):
            s_c = _dot_t(q_ref[:, head(h)], kc_ref[:, head(h)]) * NA_SCALE
            o_ref[:, head(h)] = _attend(s_c, vc_ref[:, head(h)]).astype(o_ref.dtype)

    _on_query_tiles(with_ctx, lat, ctx)


def _na_attention(proj, bias_tab, with_ctx):
    n_qt = NQ + 1 if with_ctx else NQ
    out_rows = ROWS if with_ctx else T_LAT
    ctx_blk = T_LAT // CTX_LEN
    width = NA_HEADS * HEAD
    kcol, vcol = COL_NA_K // width, COL_NA_V // width
    return pl.pallas_call(
        functools.partial(_na_kernel, with_ctx=with_ctx),
        grid=(BATCH, n_qt),
        in_specs=[pl.BlockSpec((TQ, width), lambda b, t: (_q_row_block(b, t), 0)),
                  pl.BlockSpec((CTX_LEN, width), lambda b, t: (ctx_blk + b, kcol)),
                  pl.BlockSpec((CTX_LEN, width), lambda b, t: (ctx_blk + b, vcol)),
                  pl.BlockSpec((SEQ, width), lambda b, t: (b, kcol)),
                  pl.BlockSpec((SEQ, width), lambda b, t: (b, vcol)),
                  pl.BlockSpec((NA_HEADS, 2 * NA_KH, GRID_W, HEAD), lambda b, t: (0, 0, 0, 0))],
        out_specs=pl.BlockSpec((TQ, width), lambda b, t: (_q_row_block(b, t), 0)),
        out_shape=jax.ShapeDtypeStruct((out_rows, width), BF16),
        scratch_shapes=[pltpu.VMEM((NA_HEADS, TQ, NA_WIN), F32)],
        compiler_params=_cparams(("arbitrary",) * 2),
        name="na_attention",
    )(proj, proj, proj, proj, proj, bias_tab)


def _out_proj_kernel(o0_ref, o1_ref, o2_ref, o3_ref, w_ref, gt_ref, *refs):
    *h_refs, out_ref = refs
    for c in range(0, D_MODEL, 512):
        acc = None
        for t, o_ref in enumerate((o0_ref, o1_ref, o2_ref, o3_ref)):
            part = _dot(o_ref[...], w_ref[t * 512:(t + 1) * 512, c:c + 512])
            acc = part if acc is None else acc + part
        out_ref[:, c:c + 512] = gt_ref[0, :, c:c + 512] * acc

    def residual(h_ref):
        out_ref[...] += h_ref[...]

    _with_stream_rows(h_refs, TM_O, residual)


def _out_proj(o_parts, w_out, stream, mod, n_rows):
    o_spec = pl.BlockSpec((TM_O, 512), lambda i: (i, 0))
    return pl.pallas_call(
        _out_proj_kernel,
        grid=(n_rows // TM_O,),
        in_specs=[o_spec, o_spec, o_spec, o_spec,
                  pl.BlockSpec((D_MODEL, D_MODEL), lambda i: (0, 0), pipeline_mode=pl.Buffered(1)),
                  pl.BlockSpec((1, 1, D_MODEL), lambda i: (_mod_row(i, TM_O), 0, 2))]
                 + _stream_specs(stream, TM_O),
        out_specs=pl.BlockSpec((TM_O, D_MODEL), lambda i: (i, 0)),
        out_shape=jax.ShapeDtypeStruct((n_rows, D_MODEL), F32),
        compiler_params=_cparams(("arbitrary",)),
        name="out_proj",
    )(*o_parts, w_out, mod, *stream)


def _ffn_kernel(h_ref, g_ref, sh_ref, sc_ref, gt_ref, wg_ref, wu_ref, wd_ref, out_ref, u_scr):
    j = pl.program_id(1)

    @pl.when(j == 0)
    def _():
        y = _rms(h_ref[...]) * g_ref[...]
        u_scr[...] = (y * (1.0 + sc_ref[0]) + sh_ref[0]).astype(BF16)
        out_ref[...] = jnp.zeros_like(out_ref)

    _swiglu_accumulate(u_scr[...], wg_ref, wu_ref, wd_ref, out_ref)

    @pl.when(j == pl.num_programs(1) - 1)
    def _():
        out_ref[...] = h_ref[...] + gt_ref[0] * out_ref[...]


def _ffn_dense(h, g, mod, w_gate, w_up, w_down):
    def mod_spec(chunk):
        return pl.BlockSpec((1, 1, D_MODEL), lambda i, j: (_mod_row(i, TM), 0, chunk))

    return pl.pallas_call(
        _ffn_kernel,
        grid=(N_ALL_TILES, D_FF // TF),
        in_specs=[pl.BlockSpec((TM, D_MODEL), lambda i, j: (i, 0), pipeline_mode=pl.Buffered(1)),
                  pl.BlockSpec((1, D_MODEL), lambda i, j: (0, 0)),
                  mod_spec(3), mod_spec(4), mod_spec(5),
                  pl.BlockSpec((D_MODEL, TF), lambda i, j: (0, j)),
                  pl.BlockSpec((D_MODEL, TF), lambda i, j: (0, j)),
                  pl.BlockSpec((TF, D_MODEL), lambda i, j: (j, 0))],
        out_specs=pl.BlockSpec((TM, D_MODEL), lambda i, j: (i, 0)),
        out_shape=jax.ShapeDtypeStruct((ROWS, D_MODEL), F32),
        scratch_shapes=[pltpu.VMEM((TM, D_MODEL), BF16)],
        compiler_params=_cparams(("arbitrary", "arbitrary")),
        name="ffn_dense",
    )(h, g, mod, mod, mod, w_gate, w_up, w_down)


META_LANES = 128


def _route_kernel(h_ref, g_ref, sh_ref, sc_ref, wr_ref, br_ref, u_ref, meta_ref, cnt_ref, carry):
    i = pl.program_id(0)

    @pl.when(i == 0)
    def _():
        carry[...] = jnp.zeros_like(carry)

    y = _rms(h_ref[...]) * g_ref[...]
    u = y * (1.0 + sc_ref[0]) + sh_ref[0]
    u_ref[...] = u

    w = wr_ref[...]
    u_hi = u.astype(BF16)
    u_lo = (u - u_hi.astype(F32)).astype(BF16)
    w_hi = w.astype(BF16)
    w_lo = (w - w_hi.astype(F32)).astype(BF16)
    logits = _dot(u_hi, w_hi) + (_dot(u_hi, w_lo) + _dot(u_lo, w_hi)) + br_ref[...]

    lane = lax.broadcasted_iota(jnp.int32, logits.shape, 1)
    logits = jnp.where(lane < N_EXPERTS, logits, NEG_INF)
    lane_f = lane.astype(F32)
    m1 = jnp.max(logits, axis=-1, keepdims=True)
    i1 = jnp.min(jnp.where(logits == m1, lane_f, float(META_LANES)), axis=-1, keepdims=True)
    pick1 = lane_f == i1
    rest = jnp.where(pick1, NEG_INF, logits)
    m2 = jnp.max(rest, axis=-1, keepdims=True)
    i2 = jnp.min(jnp.where(rest == m2, lane_f, float(META_LANES)), axis=-1, keepdims=True)
    pick2 = lane_f == i2
    e = jnp.exp(m2 - m1)
    g1 = 1.0 / (1.0 + e)
    g2 = e / (1.0 + e)

    onehot = jnp.where(pick1 | pick2, 1.0, 0.0)
    rr = lax.broadcasted_iota(jnp.int32, (TM, TM), 0)
    cc = lax.broadcasted_iota(jnp.int32, (TM, TM), 1)
    earlier = jnp.where(cc < rr, 1.0, 0.0).astype(BF16)
    before = _dot(earlier, onehot.astype(BF16)) + carry[...]
    r1 = jnp.sum(jnp.where(pick1, before, 0.0), axis=-1, keepdims=True)
    r2 = jnp.sum(jnp.where(pick2, before, 0.0), axis=-1, keepdims=True)
    carry[...] += jnp.sum(onehot, axis=0, keepdims=True)

    meta = jnp.zeros(logits.shape, F32)
    for k, val in enumerate((i1, i2, r1, r2, g1, g2)):
        meta = jnp.where(lane == k, val, meta)
    meta_ref[...] = meta
    cnt_ref[...] = jnp.broadcast_to(carry[...], cnt_ref.shape)


def _moe_route(h, g, mod, w_router, b_router):
    def mod_spec(chunk):
        return pl.BlockSpec((1, 1, D_MODEL), lambda i: (_mod_row(i, TM), 0, chunk))

    return pl.pallas_call(
        _route_kernel,
        grid=(N_LAT_TILES,),
        in_specs=[pl.BlockSpec((TM, D_MODEL), lambda i: (i, 0)),
                  pl.BlockSpec((1, D_MODEL), lambda i: (0, 0)),
                  mod_spec(3), mod_spec(4),
                  pl.BlockSpec((D_MODEL, META_LANES), lambda i: (0, 0)),
                  pl.BlockSpec((1, META_LANES), lambda i: (0, 0))],
        out_specs=[pl.BlockSpec((TM, D_MODEL), lambda i: (i, 0)),
                   pl.BlockSpec((TM, META_LANES), lambda i: (i, 0)),
                   pl.BlockSpec((8, META_LANES), lambda i: (0, 0))],
        out_shape=[jax.ShapeDtypeStruct((T_LAT, D_MODEL), F32),
                   jax.ShapeDtypeStruct((T_LAT, META_LANES), F32),
                   jax.ShapeDtypeStruct((8, META_LANES), F32)],
        scratch_shapes=[pltpu.VMEM((1, META_LANES), F32)],
        compiler_params=_cparams(("arbitrary",)),
        name="moe_route",
    )(h, g, mod, mod, w_router, b_router)


def _row_copy(src, s, dst, d, sem):
    return pltpu.make_async_copy(src.at[pl.ds(s, 1)], dst.at[pl.ds(d, 1)], sem)


MOE_NJ = D_FF_EXPERT // TF
GATHER_ROWS_PER_STEP = TM // MOE_NJ


TAIL_ROWS = range(GATHER_ROWS_PER_STEP * MOE_NJ, TM)
TOK_ROWS = 2 * T_LAT + TM
HALF = TM // 2


def _moe_ffn_kernel(te_ref, nu_ref, nv_ref, src_ref, dst_ref, u_ref, wg_ref, wu_ref, wd_ref, tok_ref,
                    xg_scr, xb_scr, acc, sem):
    i, j = pl.program_id(0), pl.program_id(1)
    n_used = nu_ref[0]
    used = i < n_used
    last = n_used - 1
    slot = i % 2
    nxt = jnp.minimum(i + 1, last)
    prev = i - 1

    def get_row(tile, r):
        _row_copy(u_ref, src_ref[tile * TM + r], xg_scr, r, sem.at[0]).start()

    def put_row(tile, s, r):
        _row_copy(acc.at[s], r, tok_ref, dst_ref[(tile + 1) * TM + r], sem.at[1]).start()

    def wait_rows(get):
        def wait(r, c):
            if get:
                _row_copy(u_ref, 0, xg_scr, 0, sem.at[0]).wait()
            else:
                _row_copy(acc.at[0], 0, tok_ref, 0, sem.at[1]).wait()
            return c
        lax.fori_loop(0, TM, wait, 0, unroll=8)

    @pl.when((i == 0) & (j == 0))
    def _():
        lax.fori_loop(0, TM, lambda r, c: (get_row(0, r), c)[1], 0)
        acc[1] = jnp.zeros((TM, D_MODEL), F32)

    @pl.when(used & (j == 0))
    def _():
        wait_rows(True)
        xb_scr[...] = xg_scr[...].astype(BF16)
        acc[slot] = jnp.zeros((TM, D_MODEL), F32)
        for r in TAIL_ROWS:
            get_row(nxt, r)
            put_row(prev, 1 - slot, r)

    def step(rows):
        for r in range(GATHER_ROWS_PER_STEP):
            get_row(nxt, r * MOE_NJ + j)
            put_row(prev, 1 - slot, r * MOE_NJ + j)
        _swiglu_accumulate(xb_scr[:rows], wg_ref, wu_ref, wd_ref, acc.at[slot])

    pl.when(used & (nv_ref[i] > HALF))(lambda: step(TM))
    pl.when(used & (nv_ref[i] <= HALF))(lambda: step(HALF))

    @pl.when(used & (j == MOE_NJ - 1))
    def _():
        wait_rows(False)

    @pl.when((i == last) & (j == MOE_NJ - 1))
    def _():
        wait_rows(True)
        lax.fori_loop(0, TM, lambda r, c: (put_row(i, slot, r), c)[1], 0)
        wait_rows(False)


def _moe_ffn(u, src, dst, tile_expert, n_used, n_valid, w_gate, w_up, w_down):
    def jj(i, j, nu):
        return jnp.where(i < nu[0], j, MOE_NJ - 1)

    grid_spec = pltpu.PrefetchScalarGridSpec(
        num_scalar_prefetch=5,
        grid=(MOE_TILES, MOE_NJ),
        in_specs=[pl.BlockSpec(memory_space=pl.ANY),
                  pl.BlockSpec((None, D_MODEL, TF), lambda i, j, te, nu, *_: (te[i], 0, jj(i, j, nu))),
                  pl.BlockSpec((None, D_MODEL, TF), lambda i, j, te, nu, *_: (te[i], 0, jj(i, j, nu))),
                  pl.BlockSpec((None, TF, D_MODEL), lambda i, j, te, nu, *_: (te[i], jj(i, j, nu), 0))],
        out_specs=pl.BlockSpec(memory_space=pl.ANY),
        scratch_shapes=[pltpu.VMEM((TM, D_MODEL), F32), pltpu.VMEM((TM, D_MODEL), BF16),
                        pltpu.VMEM((2, TM, D_MODEL), F32), pltpu.SemaphoreType.DMA((2,))],
    )
    return pl.pallas_call(
        _moe_ffn_kernel,
        grid_spec=grid_spec,
        out_shape=jax.ShapeDtypeStruct((TOK_ROWS, D_MODEL), F32),
        compiler_params=_cparams(("arbitrary", "arbitrary")),
        name="moe_ffn",
    )(tile_expert, n_used, n_valid, src, dst, u, w_gate, w_up, w_down)


def _combine_kernel(h_ref, meta_ref, gt_ref, gf_ref, y1_ref, y2_ref, out_ref):
    meta = meta_ref[...]
    y = meta[:, 4:5] * y1_ref[...] + meta[:, 5:6] * y2_ref[...]
    hn = h_ref[...] + gt_ref[0] * y
    out_ref[...] = _rms(hn) * gf_ref[...]


def _moe_combine(h, meta, mod, g_final, tok):
    n = T_LAT // TC
    row = pl.BlockSpec((TC, D_MODEL), lambda i: (i, 0))
    return pl.pallas_call(
        _combine_kernel,
        grid=(n,),
        in_specs=[row,
                  pl.BlockSpec((TC, META_LANES), lambda i: (i, 0)),
                  pl.BlockSpec((1, 1, D_MODEL), lambda i: (i // (SEQ // TC), 0, 5)),
                  pl.BlockSpec((1, D_MODEL), lambda i: (0, 0)),
                  row,
                  pl.BlockSpec((TC, D_MODEL), lambda i: (n + i, 0))],
        out_specs=row,
        out_shape=jax.ShapeDtypeStruct((T_LAT, D_MODEL), F32),
        compiler_params=_cparams(("arbitrary",)),
        name="moe_combine",
    )(h, meta, mod, g_final, tok, tok)


def _moe_plan(meta, counts_f):
    i1 = meta[:, 0].astype(jnp.int32)
    i2 = meta[:, 1].astype(jnp.int32)
    r1 = meta[:, 2].astype(jnp.int32)
    r2 = meta[:, 3].astype(jnp.int32)
    counts = counts_f[0, :N_EXPERTS].astype(jnp.int32)
    padded = (counts + TM - 1) // TM * TM
    ends = jnp.cumsum(padded)
    starts = ends - padded
    experts = jnp.arange(N_EXPERTS, dtype=jnp.int32)

    def start_of(idx):
        return jnp.sum(jnp.where(idx[:, None] == experts, starts, 0), axis=1)

    pos = jnp.concatenate([start_of(i1) + r1, start_of(i2) + r2])
    n_used = ends[-1] // TM
    tile_start = jnp.arange(MOE_TILES, dtype=jnp.int32) * TM
    expert = jnp.minimum(jnp.sum(tile_start[:, None] >= ends[None, :], axis=1), N_EXPERTS - 1)
    n_valid = jnp.clip((starts + counts)[expert] - tile_start, 0, TM).astype(jnp.int32)
    expert = expert[jnp.minimum(jnp.arange(MOE_TILES), n_used - 1)].astype(jnp.int32)
    slot = jnp.arange(MOE_ROWS, dtype=jnp.int32)
    owner = jnp.full((MOE_ROWS,), -1, jnp.int32).at[pos].set(jnp.arange(2 * T_LAT, dtype=jnp.int32),
                                                             unique_indices=True)
    spare = 2 * T_LAT + slot % TM
    dst = jnp.concatenate([spare[:TM], jnp.where(owner >= 0, owner, spare)])
    src = jnp.where(owner >= 0, owner % T_LAT, 0)
    return src, dst, expert, n_used.reshape(1).astype(jnp.int32), n_valid


def _rope_tables():
    t = jnp.arange(SEQ, dtype=jnp.int32)
    row = (t // GRID_W).astype(F32)
    col = (t % GRID_W).astype(F32)
    out = {}
    for dim, name in ((MLA_ROPE, "64"), (HEAD, "128")):
        half = dim // 2
        freqs = ROPE_THETA ** (-jnp.arange(0, half, 2, dtype=F32) / half)
        ar, ac = row[:, None] * freqs, col[:, None] * freqs
        c = jnp.concatenate([jnp.cos(ar), jnp.cos(ar), jnp.cos(ac), jnp.cos(ac)], axis=1)
        s = jnp.concatenate([-jnp.sin(ar), jnp.sin(ar), -jnp.sin(ac), jnp.sin(ac)], axis=1)
        order = _rope_lane_order(dim)
        c = jnp.tile(c, (1, HEAD // dim))[:, order]
        s = jnp.tile(s, (1, HEAD // dim))[:, order]
        out["c" + name] = jnp.concatenate([c, jnp.ones((TP, HEAD), F32)], axis=0)
        out["s" + name] = jnp.concatenate([s, jnp.zeros((TP, HEAD), F32)], axis=0)
    return out


def _na_bias_table(rel_bias):
    qc = np.arange(GRID_W)[:, None]
    kc = np.arange(GRID_W)[None, :]
    cs = np.clip(qc - NA_KW // 2, 0, GRID_W - NA_KW)
    col_ok = (kc >= cs) & (kc < cs + NA_KW)
    col_off = np.clip(kc - qc + NA_KW - 1, 0, 2 * NA_KW - 2)
    pick = jnp.asarray(col_off[None] == np.arange(2 * NA_KW - 1)[:, None, None], F32)
    tab = jnp.einsum("had,dqk->haqk", rel_bias * LOG2E, pick, precision=lax.Precision.HIGHEST)
    tab = jnp.where(jnp.asarray(col_ok), tab, NEG_INF)
    tab = jnp.concatenate([tab, jnp.full((NA_HEADS, 1, GRID_W, GRID_W), NEG_INF, F32)], axis=1)
    return jnp.concatenate([tab, tab], axis=-1)


def _layer_weights(l, w_in, mla_g_q, mla_w_qup, mla_g_kv, mla_w_kvup, gqa_g_q, gqa_g_k):
    sizes = (512, 512, 512, 384, 128, 64, 512, 256, 256, 512, 512, 512)
    parts = jnp.split(w_in[l], np.cumsum(sizes)[:-1].tolist(), axis=1)
    order = (0, 1, 2, 3, 4, 6, 7, 8, 9, 10, 11, 5)
    pad = jnp.zeros((D_MODEL, IN_COLS_PAD - sum(sizes)), F32)
    w_in_p = jnp.concatenate([parts[k] for k in order] + [pad], axis=1)

    o64, o128 = _rope_lane_order(MLA_ROPE), _rope_lane_order(HEAD)
    cols = np.arange(IN_COLS_PAD)
    for start, width, lane_order in ((COL_GQA_Q, 512, o128), (COL_GQA_K, 256, o128),
                                     (COL_DIFF_Q, 512, o64), (COL_DIFF_K, 512, o64), (COL_KPE, HEAD, o64)):
        for c0 in range(start, start + width, HEAD):
            cols[c0:c0 + HEAD] = c0 + lane_order
    w_in_p = w_in_p[:, cols].astype(BF16)

    qup = mla_w_qup[l].reshape(MLA_Q_RANK, MLA_HEADS, MLA_NOPE + MLA_ROPE)
    qup = jnp.pad(qup, ((0, 0), (0, 0), (0, 2 * HEAD - MLA_NOPE - MLA_ROPE)))
    qup = jnp.concatenate([qup[:, :, :HEAD], qup[:, :, HEAD:][:, :, o64]], axis=2)
    kvup = mla_w_kvup[l].reshape(MLA_KV_RANK, MLA_HEADS, MLA_NOPE + MLA_V)
    return {
        "w_in": w_in_p,
        "g_cq": mla_g_q[l].reshape(1, -1),
        "w_qup": qup.reshape(MLA_Q_RANK, MLA_HEADS * 2 * HEAD).astype(BF16),
        "g_ckv": mla_g_kv[l].reshape(1, -1),
        "w_kvk": kvup[:, :, :MLA_NOPE].reshape(MLA_KV_RANK, -1).astype(BF16),
        "w_kvv": kvup[:, :, MLA_NOPE:].reshape(MLA_KV_RANK, -1).astype(BF16),
        "g_gq": gqa_g_q[l][o128].reshape(1, -1),
        "g_gk": gqa_g_k[l][o128].reshape(1, -1),
    }


def kernel(x, c, ctx, c_ctx, w_mod, b_mod, g_mix, w_in, na_rel_bias, mla_g_q, mla_w_qup, mla_g_kv, mla_w_kvup,
           gqa_g_q, gqa_g_k, diff_lq1, diff_lk1, diff_lq2, diff_lk2, diff_g_sub, w_out, g_ffn,
           ffn_w_gate, ffn_w_up, ffn_w_down, moe_w_router, moe_b_router, moe_w_gate, moe_w_up, moe_w_down,
           g_final):
    tabs = _rope_tables()
    cvec = jnp.concatenate([c, c_ctx[None, :], jnp.zeros((16 - BATCH - 1, D_MODEL), F32)], axis=0)
    mod_all = _modulation(cvec, w_mod, b_mod)
    stream = (x.reshape(T_LAT, D_MODEL), ctx.reshape(T_CTX, D_MODEL))

    out = None
    for l in range(DEPTH):
        last = l == DEPTH - 1
        with_ctx = not last
        lam_init = 0.8 - 0.6 * math.exp(-0.3 * l)
        mod = mod_all[l].reshape(16, 1, 6 * D_MODEL)
        lw = _layer_weights(l, w_in, mla_g_q, mla_w_qup, mla_g_kv, mla_w_kvup, gqa_g_q, gqa_g_k)

        proj = _in_proj(stream, g_mix[l].reshape(1, -1), mod, lw["w_in"])
        q_mla, k_mla, v_mla, q_gqa, k_gqa, q_diff, k_diff = _prep(proj, lw, tabs)

        o_na = _na_attention(proj, _na_bias_table(na_rel_bias[l]), with_ctx)
        o_mla = _global_attention(q_mla, 2 * HEAD, k_mla, 2 * HEAD, MLA_HEADS, 0, v_mla, 0, with_ctx)
        o_gqa = _global_attention(q_gqa, HEAD, k_gqa, HEAD, GQA_KV_HEADS, 0,
                                  proj, COL_GQA_V // (GQA_KV_HEADS * HEAD), with_ctx)
        lam_vecs = jnp.stack([diff_lq1[l], diff_lk1[l], diff_lq2[l], diff_lk2[l]])
        o_diff = _global_attention(q_diff, HEAD, k_diff, HEAD, DIFF_HEADS, 0,
                                   proj, COL_DIFF_V // (DIFF_HEADS * HEAD), with_ctx,
                                   diff=(lam_vecs, diff_g_sub[l].reshape(1, -1), lam_init))

        h = _out_proj((o_na, o_mla, o_gqa, o_diff), w_out[l].astype(BF16), stream, mod,
                      ROWS if with_ctx else T_LAT)

        if l % 2 == 0:
            h = _ffn_dense(h, g_ffn[l].reshape(1, -1), mod, ffn_w_gate[l // 2], ffn_w_up[l // 2],
                           ffn_w_down[l // 2])
            stream = (h,)
        else:
            m = l // 2
            w_r = jnp.pad(moe_w_router[m], ((0, 0), (0, META_LANES - N_EXPERTS)))
            b_r = jnp.pad(moe_b_router[m], (0, META_LANES - N_EXPERTS)).reshape(1, -1)
            u, meta, counts = _moe_route(h, g_ffn[l].reshape(1, -1), mod, w_r, b_r)
            src, dst, tile_expert, n_used, n_valid = _moe_plan(meta, counts)
            tok = _moe_ffn(u, src, dst, tile_expert, n_used, n_valid, moe_w_gate[m], moe_w_up[m], moe_w_down[m])
            out = _moe_combine(h, meta, mod, g_final.reshape(1, -1), tok)
    return out.reshape(BATCH, SEQ, D_MODEL)
```

```python
import functools
import math

import numpy as np
import jax
import jax.numpy as jnp
from jax import lax
from jax.experimental import pallas as pl
from jax.experimental.pallas import tpu as pltpu

F32 = jnp.float32
BF16 = jnp.bfloat16

D_MODEL = 2048
BATCH = 8
SEQ = 2048
DEPTH = 2
GRID_W = 64
GRID_H = SEQ // GRID_W
CTX_LEN = 256
ROPE_THETA = 10000.0
EPS = 1e-6
NEG_INF = -1e30

NA_HEADS = 4
NA_KH = 8
NA_KW = 16
MLA_HEADS = 4
MLA_NOPE = 128
MLA_ROPE = 64
MLA_V = 128
MLA_Q_RANK = 384
MLA_KV_RANK = 128
GQA_HEADS = 4
GQA_KV_HEADS = 2
DIFF_HEADS = 4
DIFF_QK = 64
HEAD = 128

LOG2E = math.log2(math.e)
NA_SCALE = HEAD ** -0.5 * LOG2E
MLA_SCALE = (MLA_NOPE + MLA_ROPE) ** -0.5 * LOG2E
GQA_SCALE = HEAD ** -0.5 * LOG2E
DIFF_SCALE = DIFF_QK ** -0.5 * LOG2E

D_FF = 5632
N_EXPERTS = 8
D_FF_EXPERT = 7168

T_LAT = BATCH * SEQ
T_CTX = BATCH * CTX_LEN
ROWS = T_LAT + T_CTX

COL_NA_Q, COL_NA_K, COL_NA_V = 0, 512, 1024
COL_MLA_C = 1536
COL_GQA_Q, COL_GQA_K, COL_GQA_V = 2048, 2560, 2816
COL_DIFF_Q, COL_DIFF_K, COL_DIFF_V = 3072, 3584, 4096
COL_KPE = 4608
IN_COLS_PAD = 5120

V7X_VMEM_LIMIT = 56 * 1024 * 1024

TM = 1024
N_LAT_TILES = T_LAT // TM
N_ALL_TILES = ROWS // TM
TILES_PER_BATCH = SEQ // TM
TQ = 256
NQ = SEQ // TQ
TP = 512
TN_IN = 1024
V7X_MXU_WIDTH = 256
IN_LAST_COLS = -(-(COL_KPE + HEAD - (IN_COLS_PAD - TN_IN)) // V7X_MXU_WIDTH) * V7X_MXU_WIDTH
TM_O = 512
TF = 256
NORM_ROWS = 256
MOE_TILES = 2 * T_LAT // TM + N_EXPERTS
MOE_ROWS = MOE_TILES * TM
TC = 512


def _cparams(sem, vmem=V7X_VMEM_LIMIT):
    return pltpu.CompilerParams(dimension_semantics=sem, vmem_limit_bytes=vmem)


def _mod_row(i, tm):
    return jnp.where(i < T_LAT // tm, i // (SEQ // tm), BATCH)


def _dot(a, b):
    return jnp.dot(a, b, preferred_element_type=F32)


def _dot_t(a, b):
    return lax.dot_general(a, b, (((1,), (1,)), ((), ())), preferred_element_type=F32)


def _rms(x):
    return x * lax.rsqrt(jnp.mean(x * x, axis=-1, keepdims=True) + EPS)


def _silu(x):
    return x * jax.nn.sigmoid(x)


def _swiglu_weights(wg_ref, wu_ref, wd_ref):
    return wg_ref[...].astype(BF16), wu_ref[...].astype(BF16), wd_ref[...].astype(BF16)


def _swiglu_accumulate(x, weights, out_ref, assign=False):
    wg, wu, wd = weights
    rows = x.shape[0]
    a = (_silu(_dot(x, wg)) * _dot(x, wu)).astype(BF16)
    for c in range(0, D_MODEL, 512):
        part = _dot(a, wd[:, c:c + 512])
        if assign:
            out_ref[:rows, c:c + 512] = part
        else:
            out_ref[:rows, c:c + 512] += part


def _norm_modulate(x, g_ref, sh_ref, sc_ref):
    y = _rms(x) * g_ref[...]
    return (y * (1.0 + sc_ref[0]) + sh_ref[0]).astype(BF16)


def _mod_kernel(c_ref, w_ref, b_ref, o_ref):
    s = _silu(c_ref[...]).astype(BF16)
    o_ref[0] = _dot(s, w_ref[0].astype(BF16)) + b_ref[0]


def _modulation(cvec, w_mod, b_mod):
    tn = 1024
    return pl.pallas_call(
        _mod_kernel,
        grid=(DEPTH, 6 * D_MODEL // tn),
        in_specs=[pl.BlockSpec((16, D_MODEL), lambda l, j: (0, 0)),
                  pl.BlockSpec((1, D_MODEL, tn), lambda l, j: (l, 0, j)),
                  pl.BlockSpec((1, 1, tn), lambda l, j: (l, 0, j))],
        out_specs=pl.BlockSpec((1, 16, tn), lambda l, j: (l, 0, j)),
        out_shape=jax.ShapeDtypeStruct((DEPTH, 16, 6 * D_MODEL), F32),
        compiler_params=_cparams(("arbitrary", "arbitrary")),
        name="modulation",
    )(cvec, w_mod, b_mod.reshape(DEPTH, 1, 6 * D_MODEL))


def _stream_specs(stream, tm):
    if len(stream) == 1:
        return [pl.BlockSpec((tm, D_MODEL), lambda i, *_: (i, 0))]
    n_lat = T_LAT // tm
    return [pl.BlockSpec((tm, D_MODEL), lambda i, *_: (jnp.minimum(i, n_lat - 1), 0)),
            pl.BlockSpec((tm, D_MODEL), lambda i, *_: (jnp.maximum(i - n_lat, 0), 0))]


def _with_stream_rows(h_refs, tm, fn):
    if len(h_refs) == 1:
        fn(h_refs[0])
        return
    i = pl.program_id(0)
    n_lat = T_LAT // tm
    pl.when(i < n_lat)(lambda: fn(h_refs[0]))
    pl.when(i >= n_lat)(lambda: fn(h_refs[1]))


def _in_proj_kernel(*refs):
    *h_refs, g_ref, sh_ref, sc_ref, w_ref, o_ref, u_scr = refs

    j = pl.program_id(1)

    def first(h_ref):
        w = w_ref[...]
        for r in range(0, TM, NORM_ROWS):
            u = _norm_modulate(h_ref[r:r + NORM_ROWS], g_ref, sh_ref, sc_ref)
            u_scr[r:r + NORM_ROWS] = u
            o_ref[r:r + NORM_ROWS] = _dot(u, w).astype(o_ref.dtype)

    @pl.when(j == 0)
    def _():
        _with_stream_rows(h_refs, TM, first)

    last = pl.num_programs(1) - 1

    @pl.when((j > 0) & (j < last))
    def _():
        o_ref[...] = _dot(u_scr[...], w_ref[...]).astype(o_ref.dtype)

    @pl.when(j == last)
    def _():
        o_ref[:, :IN_LAST_COLS] = _dot(u_scr[...], w_ref[:, :IN_LAST_COLS]).astype(o_ref.dtype)
        o_ref[:, IN_LAST_COLS:] = jnp.zeros((TM, TN_IN - IN_LAST_COLS), o_ref.dtype)


def _in_proj(stream, g, mod, w_in):
    return pl.pallas_call(
        _in_proj_kernel,
        grid=(N_ALL_TILES, IN_COLS_PAD // TN_IN),
        in_specs=_stream_specs(stream, TM) + [
                  pl.BlockSpec((1, D_MODEL), lambda i, j: (0, 0)),
                  pl.BlockSpec((1, 1, D_MODEL), lambda i, j: (_mod_row(i, TM), 0, 0)),
                  pl.BlockSpec((1, 1, D_MODEL), lambda i, j: (_mod_row(i, TM), 0, 1)),
                  pl.BlockSpec((D_MODEL, TN_IN), lambda i, j: (0, j))],
        out_specs=pl.BlockSpec((TM, TN_IN), lambda i, j: (i, j)),
        out_shape=jax.ShapeDtypeStruct((ROWS, IN_COLS_PAD), BF16),
        scratch_shapes=[pltpu.VMEM((TM, D_MODEL), BF16)],
        compiler_params=_cparams(("arbitrary", "arbitrary")),
        name="in_proj",
    )(*stream, g, mod, mod, w_in)


def _rope_lane_order(dim):
    q, n_vec = dim // 4, HEAD // dim
    order = np.empty(HEAD, np.int64)
    for b in range(2):
        for v in range(n_vec):
            for a in range(2):
                for i in range(q):
                    order[b * (HEAD // 2) + v * 2 * q + a * q + i] = v * dim + a * 2 * q + b * q + i
    return order


def _to_rope_lane_order(w, dim):
    q, n_vec = dim // 4, HEAD // dim
    blocks = w.reshape(w.shape[:-1] + (-1, n_vec, 2, 2, q))
    return jnp.moveaxis(blocks, -2, -4).reshape(w.shape)


def _rope(x, c, s):
    return x * c + pltpu.roll(x, HEAD // 2, 1) * s


def _prep_kernel(mc_ref, gq_ref, gk_ref, dq_ref, dk_ref, kpe_ref,
                 g_cq_ref, w_qup_ref, g_ckv_ref, w_kvk_ref, w_kvv_ref, g_gq_ref, g_gk_ref,
                 c64_ref, s64_ref, c128_ref, s128_ref,
                 qm_ref, km_ref, vm_ref, qg_ref, kg_ref, qd_ref, kd_ref):
    c64, s64 = c64_ref[...], s64_ref[...]
    c128, s128 = c128_ref[...], s128_ref[...]

    mc = mc_ref[...].astype(F32)
    cq = (_rms(mc[:, :MLA_Q_RANK]) * g_cq_ref[...]).astype(BF16)
    ckv = (_rms(mc[:, MLA_Q_RANK:]) * g_ckv_ref[...]).astype(BF16)
    q = _dot(cq, w_qup_ref[...])
    kn = _dot(ckv, w_kvk_ref[...])
    vm_ref[...] = _dot(ckv, w_kvv_ref[...]).astype(BF16)
    kpe = _rope(kpe_ref[...].astype(F32), c64, s64).astype(BF16)
    for h in range(MLA_HEADS):
        lo = 2 * HEAD * h
        qm_ref[:, lo:lo + HEAD] = (q[:, lo:lo + HEAD] * MLA_SCALE).astype(BF16)
        pe = _rope(q[:, lo + HEAD:lo + 2 * HEAD], c64, s64)
        qm_ref[:, lo + HEAD:lo + 2 * HEAD] = (pe * MLA_SCALE).astype(BF16)
        km_ref[:, lo:lo + HEAD] = kn[:, h * HEAD:(h + 1) * HEAD].astype(BF16)
        km_ref[:, lo + HEAD:lo + 2 * HEAD] = kpe

    for h in range(GQA_HEADS):
        sl = slice(h * HEAD, (h + 1) * HEAD)
        z = _rms(gq_ref[:, sl].astype(F32)) * g_gq_ref[...]
        qg_ref[:, sl] = (_rope(z, c128, s128) * GQA_SCALE).astype(BF16)
    for h in range(GQA_KV_HEADS):
        sl = slice(h * HEAD, (h + 1) * HEAD)
        z = _rms(gk_ref[:, sl].astype(F32)) * g_gk_ref[...]
        kg_ref[:, sl] = _rope(z, c128, s128).astype(BF16)

    for h in range(DIFF_HEADS):
        sl = slice(h * HEAD, (h + 1) * HEAD)
        qd_ref[:, sl] = (_rope(dq_ref[:, sl].astype(F32), c64, s64) * DIFF_SCALE).astype(BF16)
        kd_ref[:, sl] = _rope(dk_ref[:, sl].astype(F32), c64, s64).astype(BF16)


def _prep(proj, lw, tabs):
    n_tiles = ROWS // TP
    lat_tiles = T_LAT // TP
    per_batch = SEQ // TP

    def col(width, start):
        return pl.BlockSpec((TP, width), lambda i: (i, start // width))

    def full(shape):
        return pl.BlockSpec(shape, lambda i: (0,) * len(shape))

    def tab():
        return pl.BlockSpec((TP, HEAD), lambda i: (jnp.where(i < lat_tiles, i % per_batch, per_batch), 0))

    def out(width):
        return pl.BlockSpec((TP, width), lambda i: (i, 0))

    widths = (1024, 1024, 512, 512, 256, 512, 512)
    return pl.pallas_call(
        _prep_kernel,
        grid=(n_tiles,),
        in_specs=[col(512, COL_MLA_C), col(512, COL_GQA_Q), col(256, COL_GQA_K),
                  col(512, COL_DIFF_Q), col(512, COL_DIFF_K), col(128, COL_KPE),
                  full((1, MLA_Q_RANK)), full((MLA_Q_RANK, 1024)), full((1, MLA_KV_RANK)),
                  full((MLA_KV_RANK, 512)), full((MLA_KV_RANK, 512)), full((1, HEAD)), full((1, HEAD)),
                  tab(), tab(), tab(), tab()],
        out_specs=[out(w) for w in widths],
        out_shape=[jax.ShapeDtypeStruct((ROWS, w), BF16) for w in widths],
        compiler_params=_cparams(("arbitrary",)),
        name="prep",
    )(proj, proj, proj, proj, proj, proj,
      lw["g_cq"], lw["w_qup"], lw["g_ckv"], lw["w_kvk"], lw["w_kvv"], lw["g_gq"], lw["g_gk"],
      tabs["c64"], tabs["s64"], tabs["c128"], tabs["s128"])


N_HEADS = 4
N_KEYS = CTX_LEN + SEQ


def _attend(s_c, vc, s_l=None, vl=None):
    m = jnp.max(s_c, axis=-1, keepdims=True)
    if s_l is not None:
        m = jnp.maximum(m, jnp.max(s_l, axis=-1, keepdims=True))
    p_c = jnp.exp2(s_c - m)
    den = jnp.sum(p_c, axis=-1, keepdims=True)
    o = _dot(p_c.astype(BF16), vc)
    if s_l is not None:
        p_l = jnp.exp2(s_l - m)
        den = den + jnp.sum(p_l, axis=-1, keepdims=True)
        o = o + _dot(p_l.astype(BF16), vl)
    return o / den


def _attend_ones(q, k, v_ones):
    s = _dot_t(q, k)
    p = jnp.exp2(s - jnp.max(s, axis=-1, keepdims=True)).astype(BF16)
    o = _dot(p, v_ones)
    return o[:, :HEAD] / o[:, HEAD:HEAD + 1]


def _on_query_tiles(with_ctx, lat_fn, ctx_fn):
    if not with_ctx:
        lat_fn()
        return
    qt = pl.program_id(1)
    pl.when(qt < NQ)(lat_fn)
    pl.when(qt == NQ)(ctx_fn)


def _global_kernel(q_ref, kc_ref, vc_ref, kl_ref, vl_ref, *rest, with_ctx, n_kv, q_w, k_w, lam_init):
    if lam_init is None:
        o_ref, kcat, vcat = rest
    else:
        lam_ref, g_ref, o_ref, kcat, vcat = rest

    @pl.when(pl.program_id(1) == 0)
    def _():
        lane = lax.broadcasted_iota(jnp.int32, (N_KEYS, HEAD), 1)
        ones = jnp.where(lane == 0, 1.0, 0.0).astype(BF16)
        for kv in range(n_kv):
            kcat[kv, :CTX_LEN] = kc_ref[:, kv * k_w:(kv + 1) * k_w]
            kcat[kv, CTX_LEN:] = kl_ref[:, kv * k_w:(kv + 1) * k_w]
            vcat[kv, :CTX_LEN, :HEAD] = vc_ref[:, kv * HEAD:(kv + 1) * HEAD]
            vcat[kv, CTX_LEN:, :HEAD] = vl_ref[:, kv * HEAD:(kv + 1) * HEAD]
            vcat[kv, :, HEAD:] = ones

    if lam_init is not None:
        lv = lam_ref[...]
        lam = (jnp.exp(jnp.sum(lv[0:1] * lv[1:2], axis=-1, keepdims=True))
               - jnp.exp(jnp.sum(lv[2:3] * lv[3:4], axis=-1, keepdims=True)) + lam_init)

    def run(n_keys):
        for h in range(N_HEADS):
            kv = h // (N_HEADS // n_kv)
            k, v = kcat[kv, :n_keys], vcat[kv, :n_keys]
            q = q_ref[:, h * q_w:(h + 1) * q_w]
            if lam_init is None:
                o = _attend_ones(q, k, v)
            else:
                first = (lax.broadcasted_iota(jnp.int32, q.shape, 1) & (DIFF_QK // 2)) == 0
                zero = jnp.zeros_like(q)
                o1 = _attend_ones(jnp.where(first, q, zero), k, v)
                o2 = _attend_ones(jnp.where(first, zero, q), k, v)
                o = _rms(o1 - lam * o2) * g_ref[...] * (1.0 - lam_init)
            o_ref[:, h * HEAD:(h + 1) * HEAD] = o.astype(o_ref.dtype)

    _on_query_tiles(with_ctx, lambda: run(N_KEYS), lambda: run(CTX_LEN))


def _q_row_block(b, qt):
    return jnp.where(qt < NQ, b * NQ + qt, T_LAT // TQ + b)


def _global_attention(q, q_w, k, k_w, n_kv, k_col, v, v_col, with_ctx, diff=None):
    n_qt = NQ + 1 if with_ctx else NQ
    out_rows = ROWS if with_ctx else T_LAT
    ctx_blk = T_LAT // CTX_LEN

    in_specs = [
        pl.BlockSpec((TQ, N_HEADS * q_w), lambda b, t: (_q_row_block(b, t), 0)),
        pl.BlockSpec((CTX_LEN, n_kv * k_w), lambda b, t: (ctx_blk + b, k_col)),
        pl.BlockSpec((CTX_LEN, n_kv * HEAD), lambda b, t: (ctx_blk + b, v_col)),
        pl.BlockSpec((SEQ, n_kv * k_w), lambda b, t: (b, k_col)),
        pl.BlockSpec((SEQ, n_kv * HEAD), lambda b, t: (b, v_col)),
    ]
    args = [q, k, v, k, v]
    lam_init = None
    if diff is not None:
        lam_vecs, g_sub, lam_init = diff
        in_specs += [pl.BlockSpec((4, DIFF_QK), lambda b, t: (0, 0)),
                     pl.BlockSpec((1, HEAD), lambda b, t: (0, 0))]
        args += [lam_vecs, g_sub]
    return pl.pallas_call(
        functools.partial(_global_kernel, with_ctx=with_ctx, n_kv=n_kv, q_w=q_w, k_w=k_w, lam_init=lam_init),
        grid=(BATCH, n_qt),
        in_specs=in_specs,
        out_specs=pl.BlockSpec((TQ, N_HEADS * HEAD), lambda b, t: (_q_row_block(b, t), 0)),
        out_shape=jax.ShapeDtypeStruct((out_rows, N_HEADS * HEAD), BF16),
        scratch_shapes=[pltpu.VMEM((n_kv, N_KEYS, k_w), BF16), pltpu.VMEM((n_kv, N_KEYS, 2 * HEAD), BF16)],
        compiler_params=_cparams(("arbitrary",) * 2),
        name="diff_attention" if diff is not None else "global_attention",
    )(*args)


NA_QROWS = TQ // GRID_W
NA_WROWS = 12
NA_WIN = NA_WROWS * GRID_W
NA_MASKED = 2 * NA_KH - 1


def _na_kernel(q_ref, kc_ref, vc_ref, k_ref, v_ref, bias_ref, o_ref, bias_scr, *, with_ctx):
    def head(h):
        return slice(h * HEAD, (h + 1) * HEAD)

    def lat():
        r0 = pl.program_id(1) * NA_QROWS
        ws = jnp.clip(r0 - NA_KH // 2, 0, GRID_H - NA_WROWS)
        start = pl.multiple_of(ws * GRID_W, TQ)
        block = {}
        for i in range(NA_QROWS):
            qr = r0 + i
            rs = jnp.clip(qr - NA_KH // 2, 0, GRID_H - NA_KH)
            for j in range(NA_WROWS):
                kr = ws + j
                block[i, j] = jnp.where((kr >= rs) & (kr < rs + NA_KH), kr - qr + NA_KH - 1, NA_MASKED)
        for h in range(NA_HEADS):
            for (i, j), a in block.items():
                half = (j % 2) * GRID_W
                bias_scr[h, i * GRID_W:(i + 1) * GRID_W, j * GRID_W:(j + 1) * GRID_W] = (
                    bias_ref[h, a][:, half:half + GRID_W])
            q = q_ref[:, head(h)]
            kw = k_ref[pl.ds(start, NA_WIN), head(h)]
            vw = v_ref[pl.ds(start, NA_WIN), head(h)]
            s_l = _dot_t(q, kw) * NA_SCALE + bias_scr[h]
            s_c = _dot_t(q, kc_ref[:, head(h)]) * NA_SCALE
            o_ref[:, head(h)] = _attend(s_c, vc_ref[:, head(h)], s_l, vw).astype(o_ref.dtype)

    def ctx():
        for h in range(NA_HEADS):
            s_c = _dot_t(q_ref[:, head(h)], kc_ref[:, head(h)]) * NA_SCALE
            o_ref[:, head(h)] = _attend(s_c, vc_ref[:, head(h)]).astype(o_ref.dtype)

    _on_query_tiles(with_ctx, lat, ctx)


def _na_attention(proj, bias_tab, with_ctx):
    n_qt = NQ + 1 if with_ctx else NQ
    out_rows = ROWS if with_ctx else T_LAT
    ctx_blk = T_LAT // CTX_LEN
    width = NA_HEADS * HEAD
    kcol, vcol = COL_NA_K // width, COL_NA_V // width
    return pl.pallas_call(
        functools.partial(_na_kernel, with_ctx=with_ctx),
        grid=(BATCH, n_qt),
        in_specs=[pl.BlockSpec((TQ, width), lambda b, t: (_q_row_block(b, t), 0)),
                  pl.BlockSpec((CTX_LEN, width), lambda b, t: (ctx_blk + b, kcol)),
                  pl.BlockSpec((CTX_LEN, width), lambda b, t: (ctx_blk + b, vcol)),
                  pl.BlockSpec((SEQ, width), lambda b, t: (b, kcol)),
                  pl.BlockSpec((SEQ, width), lambda b, t: (b, vcol)),
                  pl.BlockSpec((NA_HEADS, 2 * NA_KH, GRID_W, HEAD), lambda b, t: (0, 0, 0, 0))],
        out_specs=pl.BlockSpec((TQ, width), lambda b, t: (_q_row_block(b, t), 0)),
        out_shape=jax.ShapeDtypeStruct((out_rows, width), BF16),
        scratch_shapes=[pltpu.VMEM((NA_HEADS, TQ, NA_WIN), F32)],
        compiler_params=_cparams(("arbitrary",) * 2),
        name="na_attention",
    )(proj, proj, proj, proj, proj, bias_tab)


def _out_proj_kernel(o0_ref, o1_ref, o2_ref, o3_ref, w_ref, gt_ref, *refs):
    *h_refs, out_ref = refs
    for c in range(0, D_MODEL, 512):
        acc = None
        for t, o_ref in enumerate((o0_ref, o1_ref, o2_ref, o3_ref)):
            part = _dot(o_ref[...], w_ref[t * 512:(t + 1) * 512, c:c + 512])
            acc = part if acc is None else acc + part
        out_ref[:, c:c + 512] = gt_ref[0, :, c:c + 512] * acc

    def residual(h_ref):
        out_ref[...] += h_ref[...]

    _with_stream_rows(h_refs, TM_O, residual)


def _out_proj(o_parts, w_out, stream, mod, n_rows):
    o_spec = pl.BlockSpec((TM_O, 512), lambda i: (i, 0))
    return pl.pallas_call(
        _out_proj_kernel,
        grid=(n_rows // TM_O,),
        in_specs=[o_spec, o_spec, o_spec, o_spec,
                  pl.BlockSpec((D_MODEL, D_MODEL), lambda i: (0, 0), pipeline_mode=pl.Buffered(1)),
                  pl.BlockSpec((1, 1, D_MODEL), lambda i: (_mod_row(i, TM_O), 0, 2))]
                 + _stream_specs(stream, TM_O),
        out_specs=pl.BlockSpec((TM_O, D_MODEL), lambda i: (i, 0)),
        out_shape=jax.ShapeDtypeStruct((n_rows, D_MODEL), F32),
        compiler_params=_cparams(("arbitrary",)),
        name="out_proj",
    )(*o_parts, w_out, mod, *stream)


def _ffn_kernel(h_ref, g_ref, sh_ref, sc_ref, gt_ref, wg_ref, wu_ref, wd_ref, out_ref, u_scr):
    j = pl.program_id(1)

    @pl.when(j == 0)
    def _():
        weights = _swiglu_weights(wg_ref, wu_ref, wd_ref)
        for r in range(0, TM, NORM_ROWS):
            u = _norm_modulate(h_ref[r:r + NORM_ROWS], g_ref, sh_ref, sc_ref)
            u_scr[r:r + NORM_ROWS] = u
            _swiglu_accumulate(u, weights, out_ref.at[pl.ds(r, NORM_ROWS)], assign=True)

    @pl.when(j > 0)
    def _():
        _swiglu_accumulate(u_scr[...], _swiglu_weights(wg_ref, wu_ref, wd_ref), out_ref)

    @pl.when(j == pl.num_programs(1) - 1)
    def _():
        out_ref[...] = h_ref[...] + gt_ref[0] * out_ref[...]


def _ffn_dense(h, g, mod, w_gate, w_up, w_down):
    def mod_spec(chunk):
        return pl.BlockSpec((1, 1, D_MODEL), lambda i, j: (_mod_row(i, TM), 0, chunk))

    return pl.pallas_call(
        _ffn_kernel,
        grid=(N_ALL_TILES, D_FF // TF),
        in_specs=[pl.BlockSpec((TM, D_MODEL), lambda i, j: (i, 0), pipeline_mode=pl.Buffered(1)),
                  pl.BlockSpec((1, D_MODEL), lambda i, j: (0, 0)),
                  mod_spec(3), mod_spec(4), mod_spec(5),
                  pl.BlockSpec((D_MODEL, TF), lambda i, j: (0, j)),
                  pl.BlockSpec((D_MODEL, TF), lambda i, j: (0, j)),
                  pl.BlockSpec((TF, D_MODEL), lambda i, j: (j, 0))],
        out_specs=pl.BlockSpec((TM, D_MODEL), lambda i, j: (i, 0)),
        out_shape=jax.ShapeDtypeStruct((ROWS, D_MODEL), F32),
        scratch_shapes=[pltpu.VMEM((TM, D_MODEL), BF16)],
        compiler_params=_cparams(("arbitrary", "arbitrary")),
        name="ffn_dense",
    )(h, g, mod, mod, mod, w_gate, w_up, w_down)


META_LANES = 128


def _route_kernel(h_ref, g_ref, sh_ref, sc_ref, wr_ref, br_ref, u_ref, meta_ref, cnt_ref, carry):
    i = pl.program_id(0)

    @pl.when(i == 0)
    def _():
        carry[...] = jnp.zeros_like(carry)

    y = _rms(h_ref[...]) * g_ref[...]
    u = y * (1.0 + sc_ref[0]) + sh_ref[0]
    u_ref[...] = u

    w = wr_ref[...]
    u_hi = u.astype(BF16)
    u_lo = (u - u_hi.astype(F32)).astype(BF16)
    w_hi = w.astype(BF16)
    w_lo = (w - w_hi.astype(F32)).astype(BF16)
    logits = _dot(u_hi, w_hi) + (_dot(u_hi, w_lo) + _dot(u_lo, w_hi)) + br_ref[...]

    lane = lax.broadcasted_iota(jnp.int32, logits.shape, 1)
    logits = jnp.where(lane < N_EXPERTS, logits, NEG_INF)
    lane_f = lane.astype(F32)
    m1 = jnp.max(logits, axis=-1, keepdims=True)
    i1 = jnp.min(jnp.where(logits == m1, lane_f, float(META_LANES)), axis=-1, keepdims=True)
    pick1 = lane_f == i1
    rest = jnp.where(pick1, NEG_INF, logits)
    m2 = jnp.max(rest, axis=-1, keepdims=True)
    i2 = jnp.min(jnp.where(rest == m2, lane_f, float(META_LANES)), axis=-1, keepdims=True)
    pick2 = lane_f == i2
    e = jnp.exp(m2 - m1)
    g1 = 1.0 / (1.0 + e)
    g2 = e / (1.0 + e)

    onehot = jnp.where(pick1 | pick2, 1.0, 0.0)
    rr = lax.broadcasted_iota(jnp.int32, (TM, TM), 0)
    cc = lax.broadcasted_iota(jnp.int32, (TM, TM), 1)
    earlier = jnp.where(cc < rr, 1.0, 0.0).astype(BF16)
    before = _dot(earlier, onehot.astype(BF16)) + carry[...]
    r1 = jnp.sum(jnp.where(pick1, before, 0.0), axis=-1, keepdims=True)
    r2 = jnp.sum(jnp.where(pick2, before, 0.0), axis=-1, keepdims=True)
    carry[...] += jnp.sum(onehot, axis=0, keepdims=True)

    meta = jnp.zeros(logits.shape, F32)
    for k, val in enumerate((i1, i2, r1, r2, g1, g2)):
        meta = jnp.where(lane == k, val, meta)
    meta_ref[...] = meta
    cnt_ref[...] = jnp.broadcast_to(carry[...], cnt_ref.shape)


def _moe_route(h, g, mod, w_router, b_router):
    def mod_spec(chunk):
        return pl.BlockSpec((1, 1, D_MODEL), lambda i: (_mod_row(i, TM), 0, chunk))

    return pl.pallas_call(
        _route_kernel,
        grid=(N_LAT_TILES,),
        in_specs=[pl.BlockSpec((TM, D_MODEL), lambda i: (i, 0)),
                  pl.BlockSpec((1, D_MODEL), lambda i: (0, 0)),
                  mod_spec(3), mod_spec(4),
                  pl.BlockSpec((D_MODEL, META_LANES), lambda i: (0, 0)),
                  pl.BlockSpec((1, META_LANES), lambda i: (0, 0))],
        out_specs=[pl.BlockSpec((TM, D_MODEL), lambda i: (i, 0)),
                   pl.BlockSpec((TM, META_LANES), lambda i: (i, 0)),
                   pl.BlockSpec((8, META_LANES), lambda i: (0, 0))],
        out_shape=[jax.ShapeDtypeStruct((T_LAT, D_MODEL), F32),
                   jax.ShapeDtypeStruct((T_LAT, META_LANES), F32),
                   jax.ShapeDtypeStruct((8, META_LANES), F32)],
        scratch_shapes=[pltpu.VMEM((1, META_LANES), F32)],
        compiler_params=_cparams(("arbitrary",)),
        name="moe_route",
    )(h, g, mod, mod, w_router, b_router)


def _row_copy(src, s, dst, d, sem):
    return pltpu.make_async_copy(src.at[pl.ds(s, 1)], dst.at[pl.ds(d, 1)], sem)


MOE_NJ = D_FF_EXPERT // TF
GATHER_ROWS_PER_STEP = TM // MOE_NJ


TAIL_ROWS = range(GATHER_ROWS_PER_STEP * MOE_NJ, TM)
TOK_ROWS = 2 * T_LAT + TM
HALF = TM // 2


def _moe_ffn_kernel(te_ref, nu_ref, nv_ref, src_ref, dst_ref, u_ref, wg_ref, wu_ref, wd_ref, tok_ref,
                    xg_scr, xb_scr, acc, sem):
    i, j = pl.program_id(0), pl.program_id(1)
    n_used = nu_ref[0]
    used = i < n_used
    last = n_used - 1
    slot = i % 2
    nxt = jnp.minimum(i + 1, last)
    prev = i - 1

    def get_row(tile, r):
        _row_copy(u_ref, src_ref[tile * TM + r], xg_scr, r, sem.at[0]).start()

    def put_row(tile, s, r):
        _row_copy(acc.at[s], r, tok_ref, dst_ref[(tile + 1) * TM + r], sem.at[1]).start()

    def wait_rows(get):
        def wait(r, c):
            if get:
                _row_copy(u_ref, 0, xg_scr, 0, sem.at[0]).wait()
            else:
                _row_copy(acc.at[0], 0, tok_ref, 0, sem.at[1]).wait()
            return c
        lax.fori_loop(0, TM, wait, 0, unroll=8)

    @pl.when((i == 0) & (j == 0))
    def _():
        lax.fori_loop(0, TM, lambda r, c: (get_row(0, r), c)[1], 0)
        acc[1] = jnp.zeros((TM, D_MODEL), F32)

    @pl.when(used & (j == 0))
    def _():
        wait_rows(True)
        xb_scr[...] = xg_scr[...].astype(BF16)
        acc[slot] = jnp.zeros((TM, D_MODEL), F32)
        for r in TAIL_ROWS:
            get_row(nxt, r)
            put_row(prev, 1 - slot, r)

    def step(rows):
        for r in range(GATHER_ROWS_PER_STEP):
            get_row(nxt, r * MOE_NJ + j)
            put_row(prev, 1 - slot, r * MOE_NJ + j)
        _swiglu_accumulate(xb_scr[:rows], _swiglu_weights(wg_ref, wu_ref, wd_ref), acc.at[slot])

    pl.when(used & (nv_ref[i] > HALF))(lambda: step(TM))
    pl.when(used & (nv_ref[i] <= HALF))(lambda: step(HALF))

    @pl.when(used & (j == MOE_NJ - 1))
    def _():
        wait_rows(False)

    @pl.when((i == last) & (j == MOE_NJ - 1))
    def _():
        wait_rows(True)
        lax.fori_loop(0, TM, lambda r, c: (put_row(i, slot, r), c)[1], 0)
        wait_rows(False)


def _moe_ffn(u, src, dst, tile_expert, n_used, n_valid, w_gate, w_up, w_down):
    def jj(i, j, nu):
        return jnp.where(i < nu[0], j, MOE_NJ - 1)

    grid_spec = pltpu.PrefetchScalarGridSpec(
        num_scalar_prefetch=5,
        grid=(MOE_TILES, MOE_NJ),
        in_specs=[pl.BlockSpec(memory_space=pl.ANY),
                  pl.BlockSpec((None, D_MODEL, TF), lambda i, j, te, nu, *_: (te[i], 0, jj(i, j, nu))),
                  pl.BlockSpec((None, D_MODEL, TF), lambda i, j, te, nu, *_: (te[i], 0, jj(i, j, nu))),
                  pl.BlockSpec((None, TF, D_MODEL), lambda i, j, te, nu, *_: (te[i], jj(i, j, nu), 0))],
        out_specs=pl.BlockSpec(memory_space=pl.ANY),
        scratch_shapes=[pltpu.VMEM((TM, D_MODEL), F32), pltpu.VMEM((TM, D_MODEL), BF16),
                        pltpu.VMEM((2, TM, D_MODEL), F32), pltpu.SemaphoreType.DMA((2,))],
    )
    return pl.pallas_call(
        _moe_ffn_kernel,
        grid_spec=grid_spec,
        out_shape=jax.ShapeDtypeStruct((TOK_ROWS, D_MODEL), F32),
        compiler_params=_cparams(("arbitrary", "arbitrary")),
        name="moe_ffn",
    )(tile_expert, n_used, n_valid, src, dst, u, w_gate, w_up, w_down)


def _combine_kernel(h_ref, meta_ref, gt_ref, gf_ref, y1_ref, y2_ref, out_ref):
    meta = meta_ref[...]
    y = meta[:, 4:5] * y1_ref[...] + meta[:, 5:6] * y2_ref[...]
    hn = h_ref[...] + gt_ref[0] * y
    out_ref[...] = _rms(hn) * gf_ref[...]


def _moe_combine(h, meta, mod, g_final, tok):
    n = T_LAT // TC
    row = pl.BlockSpec((TC, D_MODEL), lambda i: (i, 0))
    return pl.pallas_call(
        _combine_kernel,
        grid=(n,),
        in_specs=[row,
                  pl.BlockSpec((TC, META_LANES), lambda i: (i, 0)),
                  pl.BlockSpec((1, 1, D_MODEL), lambda i: (i // (SEQ // TC), 0, 5)),
                  pl.BlockSpec((1, D_MODEL), lambda i: (0, 0)),
                  row,
                  pl.BlockSpec((TC, D_MODEL), lambda i: (n + i, 0))],
        out_specs=row,
        out_shape=jax.ShapeDtypeStruct((T_LAT, D_MODEL), F32),
        compiler_params=_cparams(("arbitrary",)),
        name="moe_combine",
    )(h, meta, mod, g_final, tok, tok)


def _moe_plan(meta, counts_f):
    i1 = meta[:, 0].astype(jnp.int32)
    i2 = meta[:, 1].astype(jnp.int32)
    r1 = meta[:, 2].astype(jnp.int32)
    r2 = meta[:, 3].astype(jnp.int32)
    counts = counts_f[0, :N_EXPERTS].astype(jnp.int32)
    padded = (counts + TM - 1) // TM * TM
    ends = jnp.cumsum(padded)
    starts = ends - padded
    experts = jnp.arange(N_EXPERTS, dtype=jnp.int32)

    def start_of(idx):
        return jnp.sum(jnp.where(idx[:, None] == experts, starts, 0), axis=1)

    pos = jnp.concatenate([start_of(i1) + r1, start_of(i2) + r2])
    n_used = ends[-1] // TM
    tile_start = jnp.arange(MOE_TILES, dtype=jnp.int32) * TM
    expert = jnp.minimum(jnp.sum(tile_start[:, None] >= ends[None, :], axis=1), N_EXPERTS - 1)
    n_valid = jnp.clip((starts + counts)[expert] - tile_start, 0, TM).astype(jnp.int32)
    expert = expert[jnp.minimum(jnp.arange(MOE_TILES), n_used - 1)].astype(jnp.int32)
    slot = jnp.arange(MOE_ROWS, dtype=jnp.int32)
    owner = jnp.full((MOE_ROWS,), -1, jnp.int32).at[pos].set(jnp.arange(2 * T_LAT, dtype=jnp.int32),
                                                             unique_indices=True, mode="promise_in_bounds")
    spare = 2 * T_LAT + slot % TM
    dst = jnp.concatenate([spare[:TM], jnp.where(owner >= 0, owner, spare)])
    src = jnp.where(owner >= 0, owner % T_LAT, 0)
    return src, dst, expert, n_used.reshape(1).astype(jnp.int32), n_valid


def _rope_tables():
    t = jnp.arange(SEQ, dtype=jnp.int32)
    row = (t // GRID_W).astype(F32)
    col = (t % GRID_W).astype(F32)
    out = {}
    for dim, name in ((MLA_ROPE, "64"), (HEAD, "128")):
        half = dim // 2
        freqs = ROPE_THETA ** (-jnp.arange(0, half, 2, dtype=F32) / half)
        ar, ac = row[:, None] * freqs, col[:, None] * freqs
        c = jnp.concatenate([jnp.cos(ar), jnp.cos(ar), jnp.cos(ac), jnp.cos(ac)], axis=1)
        s = jnp.concatenate([-jnp.sin(ar), jnp.sin(ar), -jnp.sin(ac), jnp.sin(ac)], axis=1)
        order = _rope_lane_order(dim)
        c = jnp.tile(c, (1, HEAD // dim))[:, order]
        s = jnp.tile(s, (1, HEAD // dim))[:, order]
        out["c" + name] = jnp.concatenate([c, jnp.ones((TP, HEAD), F32)], axis=0)
        out["s" + name] = jnp.concatenate([s, jnp.zeros((TP, HEAD), F32)], axis=0)
    return out


def _na_bias_table(rel_bias):
    qc = np.arange(GRID_W)[:, None]
    kc = np.arange(GRID_W)[None, :]
    cs = np.clip(qc - NA_KW // 2, 0, GRID_W - NA_KW)
    col_ok = (kc >= cs) & (kc < cs + NA_KW)
    col_off = np.clip(kc - qc + NA_KW - 1, 0, 2 * NA_KW - 2)
    pick = jnp.asarray(col_off[None] == np.arange(2 * NA_KW - 1)[:, None, None], F32)
    tab = jnp.einsum("had,dqk->haqk", rel_bias * LOG2E, pick, precision=lax.Precision.HIGHEST)
    tab = jnp.where(jnp.asarray(col_ok), tab, NEG_INF)
    tab = jnp.concatenate([tab, jnp.full((NA_HEADS, 1, GRID_W, GRID_W), NEG_INF, F32)], axis=1)
    return jnp.concatenate([tab, tab], axis=-1)


def _layer_weights(l, w_in, mla_g_q, mla_w_qup, mla_g_kv, mla_w_kvup, gqa_g_q, gqa_g_k):
    sizes = (512, 512, 512, 384, 128, 64, 512, 256, 256, 512, 512, 512)
    parts = jnp.split(w_in[l], np.cumsum(sizes)[:-1].tolist(), axis=1)
    order = (0, 1, 2, 3, 4, 6, 7, 8, 9, 10, 11, 5)
    parts[5] = jnp.pad(parts[5], ((0, 0), (0, HEAD - MLA_ROPE)))
    for k, dim in ((5, MLA_ROPE), (6, HEAD), (7, HEAD), (9, DIFF_QK), (10, DIFF_QK)):
        parts[k] = _to_rope_lane_order(parts[k], dim)
    pad = jnp.zeros((D_MODEL, IN_COLS_PAD - sum(sizes) - (HEAD - MLA_ROPE)), F32)
    w_in_p = jnp.concatenate([parts[k] for k in order] + [pad], axis=1).astype(BF16)

    o64, o128 = _rope_lane_order(MLA_ROPE), _rope_lane_order(HEAD)
    qup = mla_w_qup[l].reshape(MLA_Q_RANK, MLA_HEADS, MLA_NOPE + MLA_ROPE)
    qup = jnp.pad(qup, ((0, 0), (0, 0), (0, 2 * HEAD - MLA_NOPE - MLA_ROPE)))
    qup = jnp.concatenate([qup[:, :, :HEAD], qup[:, :, HEAD:][:, :, o64]], axis=2)
    kvup = mla_w_kvup[l].reshape(MLA_KV_RANK, MLA_HEADS, MLA_NOPE + MLA_V)
    return {
        "w_in": w_in_p,
        "g_cq": mla_g_q[l].reshape(1, -1),
        "w_qup": qup.reshape(MLA_Q_RANK, MLA_HEADS * 2 * HEAD).astype(BF16),
        "g_ckv": mla_g_kv[l].reshape(1, -1),
        "w_kvk": kvup[:, :, :MLA_NOPE].reshape(MLA_KV_RANK, -1).astype(BF16),
        "w_kvv": kvup[:, :, MLA_NOPE:].reshape(MLA_KV_RANK, -1).astype(BF16),
        "g_gq": gqa_g_q[l][o128].reshape(1, -1),
        "g_gk": gqa_g_k[l][o128].reshape(1, -1),
    }


def kernel(x, c, ctx, c_ctx, w_mod, b_mod, g_mix, w_in, na_rel_bias, mla_g_q, mla_w_qup, mla_g_kv, mla_w_kvup,
           gqa_g_q, gqa_g_k, diff_lq1, diff_lk1, diff_lq2, diff_lk2, diff_g_sub, w_out, g_ffn,
           ffn_w_gate, ffn_w_up, ffn_w_down, moe_w_router, moe_b_router, moe_w_gate, moe_w_up, moe_w_down,
           g_final):
    tabs = _rope_tables()
    cvec = jnp.concatenate([c, c_ctx[None, :], jnp.zeros((16 - BATCH - 1, D_MODEL), F32)], axis=0)
    mod_all = _modulation(cvec, w_mod, b_mod)
    stream = (x.reshape(T_LAT, D_MODEL), ctx.reshape(T_CTX, D_MODEL))

    out = None
    for l in range(DEPTH):
        last = l == DEPTH - 1
        with_ctx = not last
        lam_init = 0.8 - 0.6 * math.exp(-0.3 * l)
        mod = mod_all[l].reshape(16, 1, 6 * D_MODEL)
        lw = _layer_weights(l, w_in, mla_g_q, mla_w_qup, mla_g_kv, mla_w_kvup, gqa_g_q, gqa_g_k)

        proj = _in_proj(stream, g_mix[l].reshape(1, -1), mod, lw["w_in"])
        q_mla, k_mla, v_mla, q_gqa, k_gqa, q_diff, k_diff = _prep(proj, lw, tabs)

        o_na = _na_attention(proj, _na_bias_table(na_rel_bias[l]), with_ctx)
        o_mla = _global_attention(q_mla, 2 * HEAD, k_mla, 2 * HEAD, MLA_HEADS, 0, v_mla, 0, with_ctx)
        o_gqa = _global_attention(q_gqa, HEAD, k_gqa, HEAD, GQA_KV_HEADS, 0,
                                  proj, COL_GQA_V // (GQA_KV_HEADS * HEAD), with_ctx)
        lam_vecs = jnp.stack([diff_lq1[l], diff_lk1[l], diff_lq2[l], diff_lk2[l]])
        o_diff = _global_attention(q_diff, HEAD, k_diff, HEAD, DIFF_HEADS, 0,
                                   proj, COL_DIFF_V // (DIFF_HEADS * HEAD), with_ctx,
                                   diff=(lam_vecs, diff_g_sub[l].reshape(1, -1), lam_init))

        h = _out_proj((o_na, o_mla, o_gqa, o_diff), w_out[l].astype(BF16), stream, mod,
                      ROWS if with_ctx else T_LAT)

        if l % 2 == 0:
            h = _ffn_dense(h, g_ffn[l].reshape(1, -1), mod, ffn_w_gate[l // 2], ffn_w_up[l // 2],
                           ffn_w_down[l // 2])
            stream = (h,)
        else:
            m = l // 2
            w_r = jnp.pad(moe_w_router[m], ((0, 0), (0, META_LANES - N_EXPERTS)))
            b_r = jnp.pad(moe_b_router[m], (0, META_LANES - N_EXPERTS)).reshape(1, -1)
            u, meta, counts = _moe_route(h, g_ffn[l].reshape(1, -1), mod, w_r, b_r)
            src, dst, tile_expert, n_used, n_valid = _moe_plan(meta, counts)
            tok = _moe_ffn(u, src, dst, tile_expert, n_used, n_valid, moe_w_gate[m], moe_w_up[m], moe_w_down[m])
            out = _moe_combine(h, meta, mod, g_final.reshape(1, -1), tok)
    return out.reshape(BATCH, SEQ, D_MODEL)
```

```python
import functools
import math

import numpy as np
import jax
import jax.numpy as jnp
from jax import lax
from jax.experimental import pallas as pl
from jax.experimental.pallas import tpu as pltpu

F32 = jnp.float32
BF16 = jnp.bfloat16

D_MODEL = 2048
BATCH = 8
SEQ = 2048
DEPTH = 2
GRID_W = 64
GRID_H = SEQ // GRID_W
CTX_LEN = 256
ROPE_THETA = 10000.0
EPS = 1e-6
NEG_INF = -1e30

NA_HEADS = 4
NA_KH = 8
NA_KW = 16
MLA_HEADS = 4
MLA_NOPE = 128
MLA_ROPE = 64
MLA_V = 128
MLA_Q_RANK = 384
MLA_KV_RANK = 128
GQA_HEADS = 4
GQA_KV_HEADS = 2
DIFF_HEADS = 4
DIFF_QK = 64
HEAD = 128

LOG2E = math.log2(math.e)
NA_SCALE = HEAD ** -0.5 * LOG2E
MLA_SCALE = (MLA_NOPE + MLA_ROPE) ** -0.5 * LOG2E
GQA_SCALE = HEAD ** -0.5 * LOG2E
DIFF_SCALE = DIFF_QK ** -0.5 * LOG2E

D_FF = 5632
N_EXPERTS = 8
D_FF_EXPERT = 7168

T_LAT = BATCH * SEQ
T_CTX = BATCH * CTX_LEN
ROWS = T_LAT + T_CTX

COL_NA_Q, COL_NA_K, COL_NA_V = 0, 512, 1024
COL_MLA_C = 1536
COL_GQA_Q, COL_GQA_K, COL_GQA_V = 2048, 2560, 2816
COL_DIFF_Q, COL_DIFF_K, COL_DIFF_V = 3072, 3584, 4096
COL_KPE = 4608
IN_COLS_PAD = 5120

V7X_VMEM_LIMIT = 56 * 1024 * 1024

TM = 1024
N_LAT_TILES = T_LAT // TM
N_ALL_TILES = ROWS // TM
TILES_PER_BATCH = SEQ // TM
TQ = 256
NQ = SEQ // TQ
TP = 512
TN_IN = 1024
V7X_MXU_WIDTH = 256
IN_LAST_COLS = -(-(COL_KPE + HEAD - (IN_COLS_PAD - TN_IN)) // V7X_MXU_WIDTH) * V7X_MXU_WIDTH
TM_O = 512
TF = 256
NORM_ROWS = 256
MOE_TILES = 2 * T_LAT // TM + N_EXPERTS
MOE_ROWS = MOE_TILES * TM
TC = 512


def _cparams(sem, vmem=V7X_VMEM_LIMIT):
    return pltpu.CompilerParams(dimension_semantics=sem, vmem_limit_bytes=vmem)


def _mod_row(i, tm):
    return jnp.where(i < T_LAT // tm, i // (SEQ // tm), BATCH)


def _dot(a, b):
    return jnp.dot(a, b, preferred_element_type=F32)


def _dot_t(a, b):
    return lax.dot_general(a, b, (((1,), (1,)), ((), ())), preferred_element_type=F32)


def _rms(x):
    return x * lax.rsqrt(jnp.mean(x * x, axis=-1, keepdims=True) + EPS)


def _silu(x):
    return x * jax.nn.sigmoid(x)


def _swiglu_weights(wg_ref, wu_ref, wd_ref):
    return wg_ref[...].astype(BF16), wu_ref[...].astype(BF16), wd_ref[...].astype(BF16)


def _swiglu_accumulate(x, weights, out_ref, assign=False):
    wg, wu, wd = weights
    rows = x.shape[0]
    a = (_silu(_dot(x, wg)) * _dot(x, wu)).astype(BF16)
    for c in range(0, D_MODEL, 512):
        part = _dot(a, wd[:, c:c + 512])
        if assign:
            out_ref[:rows, c:c + 512] = part
        else:
            out_ref[:rows, c:c + 512] += part


def _norm_modulate(x, g_ref, sh_ref, sc_ref):
    y = _rms(x) * g_ref[...]
    return (y * (1.0 + sc_ref[0]) + sh_ref[0]).astype(BF16)


def _mod_kernel(c_ref, w_ref, b_ref, o_ref):
    s = _silu(c_ref[...]).astype(BF16)
    o_ref[0] = _dot(s, w_ref[0].astype(BF16)) + b_ref[0]


def _modulation(cvec, w_mod, b_mod):
    tn = 1024
    return pl.pallas_call(
        _mod_kernel,
        grid=(DEPTH, 6 * D_MODEL // tn),
        in_specs=[pl.BlockSpec((16, D_MODEL), lambda l, j: (0, 0)),
                  pl.BlockSpec((1, D_MODEL, tn), lambda l, j: (l, 0, j)),
                  pl.BlockSpec((1, 1, tn), lambda l, j: (l, 0, j))],
        out_specs=pl.BlockSpec((1, 16, tn), lambda l, j: (l, 0, j)),
        out_shape=jax.ShapeDtypeStruct((DEPTH, 16, 6 * D_MODEL), F32),
        compiler_params=_cparams(("arbitrary", "arbitrary")),
        name="modulation",
    )(cvec, w_mod, b_mod.reshape(DEPTH, 1, 6 * D_MODEL))


def _stream_specs(stream, tm):
    if len(stream) == 1:
        return [pl.BlockSpec((tm, D_MODEL), lambda i, *_: (i, 0))]
    n_lat = T_LAT // tm
    return [pl.BlockSpec((tm, D_MODEL), lambda i, *_: (jnp.minimum(i, n_lat - 1), 0)),
            pl.BlockSpec((tm, D_MODEL), lambda i, *_: (jnp.maximum(i - n_lat, 0), 0))]


def _with_stream_rows(h_refs, tm, fn):
    if len(h_refs) == 1:
        fn(h_refs[0])
        return
    i = pl.program_id(0)
    n_lat = T_LAT // tm
    pl.when(i < n_lat)(lambda: fn(h_refs[0]))
    pl.when(i >= n_lat)(lambda: fn(h_refs[1]))


def _in_proj_kernel(*refs):
    *h_refs, g_ref, sh_ref, sc_ref, w_ref, o_ref, u_scr = refs

    j = pl.program_id(1)

    def first(h_ref):
        w = w_ref[...]
        for r in range(0, TM, NORM_ROWS):
            u = _norm_modulate(h_ref[r:r + NORM_ROWS], g_ref, sh_ref, sc_ref)
            u_scr[r:r + NORM_ROWS] = u
            o_ref[r:r + NORM_ROWS] = _dot(u, w).astype(o_ref.dtype)

    @pl.when(j == 0)
    def _():
        _with_stream_rows(h_refs, TM, first)

    last = pl.num_programs(1) - 1

    @pl.when((j > 0) & (j < last))
    def _():
        o_ref[...] = _dot(u_scr[...], w_ref[...]).astype(o_ref.dtype)

    @pl.when(j == last)
    def _():
        o_ref[:, :IN_LAST_COLS] = _dot(u_scr[...], w_ref[:, :IN_LAST_COLS]).astype(o_ref.dtype)
        o_ref[:, IN_LAST_COLS:] = jnp.zeros((TM, TN_IN - IN_LAST_COLS), o_ref.dtype)


def _in_proj(stream, g, mod, w_in, l):
    return pl.pallas_call(
        _in_proj_kernel,
        grid=(N_ALL_TILES, IN_COLS_PAD // TN_IN),
        in_specs=_stream_specs(stream, TM) + [
                  pl.BlockSpec((1, D_MODEL), lambda i, j: (0, 0)),
                  pl.BlockSpec((1, 1, D_MODEL), lambda i, j: (_mod_row(i, TM), 0, 0)),
                  pl.BlockSpec((1, 1, D_MODEL), lambda i, j: (_mod_row(i, TM), 0, 1)),
                  pl.BlockSpec((None, D_MODEL, TN_IN), lambda i, j: (l, 0, j))],
        out_specs=pl.BlockSpec((TM, TN_IN), lambda i, j: (i, j)),
        out_shape=jax.ShapeDtypeStruct((ROWS, IN_COLS_PAD), BF16),
        scratch_shapes=[pltpu.VMEM((TM, D_MODEL), BF16)],
        compiler_params=_cparams(("arbitrary", "arbitrary")),
        name="in_proj",
    )(*stream, g, mod, mod, w_in)


def _rope_lane_order(dim):
    q, n_vec = dim // 4, HEAD // dim
    order = np.empty(HEAD, np.int64)
    for b in range(2):
        for v in range(n_vec):
            for a in range(2):
                for i in range(q):
                    order[b * (HEAD // 2) + v * 2 * q + a * q + i] = v * dim + a * 2 * q + b * q + i
    return order


def _to_rope_lane_order(w, dim):
    q, n_vec = dim // 4, HEAD // dim
    blocks = w.reshape(w.shape[:-1] + (-1, n_vec, 2, 2, q))
    return jnp.moveaxis(blocks, -2, -4).reshape(w.shape)


def _rope(x, c, s):
    return x * c + pltpu.roll(x, HEAD // 2, 1) * s


def _prep_kernel(mc_ref, gq_ref, gk_ref, dq_ref, dk_ref, kpe_ref,
                 g_cq_ref, w_qup_ref, g_ckv_ref, w_kvk_ref, w_kvv_ref, g_gq_ref, g_gk_ref,
                 c64_ref, s64_ref, c128_ref, s128_ref,
                 qm_ref, km_ref, vm_ref, qg_ref, kg_ref, qd_ref, kd_ref):
    c64, s64 = c64_ref[...], s64_ref[...]
    c128, s128 = c128_ref[...], s128_ref[...]

    mc = mc_ref[...].astype(F32)
    cq = (_rms(mc[:, :MLA_Q_RANK]) * g_cq_ref[...]).astype(BF16)
    ckv = (_rms(mc[:, MLA_Q_RANK:]) * g_ckv_ref[...]).astype(BF16)
    q = _dot(cq, w_qup_ref[...])
    kn = _dot(ckv, w_kvk_ref[...])
    vm_ref[...] = _dot(ckv, w_kvv_ref[...]).astype(BF16)
    kpe = _rope(kpe_ref[...].astype(F32), c64, s64).astype(BF16)
    for h in range(MLA_HEADS):
        lo = 2 * HEAD * h
        qm_ref[:, lo:lo + HEAD] = (q[:, lo:lo + HEAD] * MLA_SCALE).astype(BF16)
        pe = _rope(q[:, lo + HEAD:lo + 2 * HEAD], c64, s64)
        qm_ref[:, lo + HEAD:lo + 2 * HEAD] = (pe * MLA_SCALE).astype(BF16)
        km_ref[:, lo:lo + HEAD] = kn[:, h * HEAD:(h + 1) * HEAD].astype(BF16)
        km_ref[:, lo + HEAD:lo + 2 * HEAD] = kpe

    for h in range(GQA_HEADS):
        sl = slice(h * HEAD, (h + 1) * HEAD)
        z = _rms(gq_ref[:, sl].astype(F32)) * g_gq_ref[...]
        qg_ref[:, sl] = (_rope(z, c128, s128) * GQA_SCALE).astype(BF16)
    for h in range(GQA_KV_HEADS):
        sl = slice(h * HEAD, (h + 1) * HEAD)
        z = _rms(gk_ref[:, sl].astype(F32)) * g_gk_ref[...]
        kg_ref[:, sl] = _rope(z, c128, s128).astype(BF16)

    for h in range(DIFF_HEADS):
        sl = slice(h * HEAD, (h + 1) * HEAD)
        qd_ref[:, sl] = (_rope(dq_ref[:, sl].astype(F32), c64, s64) * DIFF_SCALE).astype(BF16)
        kd_ref[:, sl] = _rope(dk_ref[:, sl].astype(F32), c64, s64).astype(BF16)


def _prep(proj, lw, l, tabs):
    n_tiles = ROWS // TP
    lat_tiles = T_LAT // TP
    per_batch = SEQ // TP

    def col(width, start):
        return pl.BlockSpec((TP, width), lambda i: (i, start // width))

    def full(shape):
        return pl.BlockSpec((None,) + shape, lambda i: (l,) + (0,) * len(shape))

    def tab():
        return pl.BlockSpec((TP, HEAD), lambda i: (jnp.where(i < lat_tiles, i % per_batch, per_batch), 0))

    def out(width):
        return pl.BlockSpec((TP, width), lambda i: (i, 0))

    widths = (1024, 1024, 512, 512, 256, 512, 512)
    return pl.pallas_call(
        _prep_kernel,
        grid=(n_tiles,),
        in_specs=[col(512, COL_MLA_C), col(512, COL_GQA_Q), col(256, COL_GQA_K),
                  col(512, COL_DIFF_Q), col(512, COL_DIFF_K), col(128, COL_KPE),
                  full((1, MLA_Q_RANK)), full((MLA_Q_RANK, 1024)), full((1, MLA_KV_RANK)),
                  full((MLA_KV_RANK, 512)), full((MLA_KV_RANK, 512)), full((1, HEAD)), full((1, HEAD)),
                  tab(), tab(), tab(), tab()],
        out_specs=[out(w) for w in widths],
        out_shape=[jax.ShapeDtypeStruct((ROWS, w), BF16) for w in widths],
        compiler_params=_cparams(("arbitrary",)),
        name="prep",
    )(proj, proj, proj, proj, proj, proj,
      lw["g_cq"], lw["w_qup"], lw["g_ckv"], lw["w_kvk"], lw["w_kvv"], lw["g_gq"], lw["g_gk"],
      tabs["c64"], tabs["s64"], tabs["c128"], tabs["s128"])


N_HEADS = 4
N_KEYS = CTX_LEN + SEQ


def _attend(s_c, vc, s_l=None, vl=None):
    m = jnp.max(s_c, axis=-1, keepdims=True)
    if s_l is not None:
        m = jnp.maximum(m, jnp.max(s_l, axis=-1, keepdims=True))
    p_c = jnp.exp2(s_c - m)
    den = jnp.sum(p_c, axis=-1, keepdims=True)
    o = _dot(p_c.astype(BF16), vc)
    if s_l is not None:
        p_l = jnp.exp2(s_l - m)
        den = den + jnp.sum(p_l, axis=-1, keepdims=True)
        o = o + _dot(p_l.astype(BF16), vl)
    return o / den


def _attend_ones(q, k, v_ones):
    s = _dot_t(q, k)
    p = jnp.exp2(s - jnp.max(s, axis=-1, keepdims=True)).astype(BF16)
    o = _dot(p, v_ones)
    return o[:, :HEAD] / o[:, HEAD:HEAD + 1]


def _on_query_tiles(with_ctx, lat_fn, ctx_fn):
    if not with_ctx:
        lat_fn()
        return
    qt = pl.program_id(1)
    pl.when(qt < NQ)(lat_fn)
    pl.when(qt == NQ)(ctx_fn)


def _global_kernel(q_ref, kc_ref, vc_ref, kl_ref, vl_ref, *rest, with_ctx, n_kv, q_w, k_w, lam_init):
    if lam_init is None:
        o_ref, kcat, vcat = rest
    else:
        lam_ref, g_ref, o_ref, kcat, vcat = rest

    @pl.when(pl.program_id(1) == 0)
    def _():
        lane = lax.broadcasted_iota(jnp.int32, (N_KEYS, HEAD), 1)
        ones = jnp.where(lane == 0, 1.0, 0.0).astype(BF16)
        for kv in range(n_kv):
            kcat[kv, :CTX_LEN] = kc_ref[:, kv * k_w:(kv + 1) * k_w]
            kcat[kv, CTX_LEN:] = kl_ref[:, kv * k_w:(kv + 1) * k_w]
            vcat[kv, :CTX_LEN, :HEAD] = vc_ref[:, kv * HEAD:(kv + 1) * HEAD]
            vcat[kv, CTX_LEN:, :HEAD] = vl_ref[:, kv * HEAD:(kv + 1) * HEAD]
            vcat[kv, :, HEAD:] = ones

    if lam_init is not None:
        lv = lam_ref[...]
        lam = (jnp.exp(jnp.sum(lv[0:1] * lv[1:2], axis=-1, keepdims=True))
               - jnp.exp(jnp.sum(lv[2:3] * lv[3:4], axis=-1, keepdims=True)) + lam_init)

    def run(n_keys):
        for h in range(N_HEADS):
            kv = h // (N_HEADS // n_kv)
            k, v = kcat[kv, :n_keys], vcat[kv, :n_keys]
            q = q_ref[:, h * q_w:(h + 1) * q_w]
            if lam_init is None:
                o = _attend_ones(q, k, v)
            else:
                first = (lax.broadcasted_iota(jnp.int32, q.shape, 1) & (DIFF_QK // 2)) == 0
                zero = jnp.zeros_like(q)
                o1 = _attend_ones(jnp.where(first, q, zero), k, v)
                o2 = _attend_ones(jnp.where(first, zero, q), k, v)
                o = _rms(o1 - lam * o2) * g_ref[...] * (1.0 - lam_init)
            o_ref[:, h * HEAD:(h + 1) * HEAD] = o.astype(o_ref.dtype)

    _on_query_tiles(with_ctx, lambda: run(N_KEYS), lambda: run(CTX_LEN))


def _q_row_block(b, qt):
    return jnp.where(qt < NQ, b * NQ + qt, T_LAT // TQ + b)


def _global_attention(q, q_w, k, k_w, n_kv, k_col, v, v_col, with_ctx, diff=None):
    n_qt = NQ + 1 if with_ctx else NQ
    out_rows = ROWS if with_ctx else T_LAT
    ctx_blk = T_LAT // CTX_LEN

    in_specs = [
        pl.BlockSpec((TQ, N_HEADS * q_w), lambda b, t: (_q_row_block(b, t), 0)),
        pl.BlockSpec((CTX_LEN, n_kv * k_w), lambda b, t: (ctx_blk + b, k_col)),
        pl.BlockSpec((CTX_LEN, n_kv * HEAD), lambda b, t: (ctx_blk + b, v_col)),
        pl.BlockSpec((SEQ, n_kv * k_w), lambda b, t: (b, k_col)),
        pl.BlockSpec((SEQ, n_kv * HEAD), lambda b, t: (b, v_col)),
    ]
    args = [q, k, v, k, v]
    lam_init = None
    if diff is not None:
        lam_vecs, g_sub, lam_init = diff
        in_specs += [pl.BlockSpec((4, DIFF_QK), lambda b, t: (0, 0)),
                     pl.BlockSpec((1, HEAD), lambda b, t: (0, 0))]
        args += [lam_vecs, g_sub]
    return pl.pallas_call(
        functools.partial(_global_kernel, with_ctx=with_ctx, n_kv=n_kv, q_w=q_w, k_w=k_w, lam_init=lam_init),
        grid=(BATCH, n_qt),
        in_specs=in_specs,
        out_specs=pl.BlockSpec((TQ, N_HEADS * HEAD), lambda b, t: (_q_row_block(b, t), 0)),
        out_shape=jax.ShapeDtypeStruct((out_rows, N_HEADS * HEAD), BF16),
        scratch_shapes=[pltpu.VMEM((n_kv, N_KEYS, k_w), BF16), pltpu.VMEM((n_kv, N_KEYS, 2 * HEAD), BF16)],
        compiler_params=_cparams(("arbitrary",) * 2),
        name="diff_attention" if diff is not None else "global_attention",
    )(*args)


NA_QROWS = TQ // GRID_W
NA_WROWS = 12
NA_WIN = NA_WROWS * GRID_W
NA_MASKED = 2 * NA_KH - 1


def _na_kernel(q_ref, kc_ref, vc_ref, k_ref, v_ref, bias_ref, o_ref, bias_scr, *, with_ctx):
    def head(h):
        return slice(h * HEAD, (h + 1) * HEAD)

    def lat():
        r0 = pl.program_id(1) * NA_QROWS
        ws = jnp.clip(r0 - NA_KH // 2, 0, GRID_H - NA_WROWS)
        start = pl.multiple_of(ws * GRID_W, TQ)
        block = {}
        for i in range(NA_QROWS):
            qr = r0 + i
            rs = jnp.clip(qr - NA_KH // 2, 0, GRID_H - NA_KH)
            for j in range(NA_WROWS):
                kr = ws + j
                block[i, j] = jnp.where((kr >= rs) & (kr < rs + NA_KH), kr - qr + NA_KH - 1, NA_MASKED)
        for h in range(NA_HEADS):
            for (i, j), a in block.items():
                half = (j % 2) * GRID_W
                bias_scr[h, i * GRID_W:(i + 1) * GRID_W, j * GRID_W:(j + 1) * GRID_W] = (
                    bias_ref[h, a][:, half:half + GRID_W])
            q = q_ref[:, head(h)]
            kw = k_ref[pl.ds(start, NA_WIN), head(h)]
            vw = v_ref[pl.ds(start, NA_WIN), head(h)]
            s_l = _dot_t(q, kw) * NA_SCALE + bias_scr[h]
            s_c = _dot_t(q, kc_ref[:, head(h)]) * NA_SCALE
            o_ref[:, head(h)] = _attend(s_c, vc_ref[:, head(h)], s_l, vw).astype(o_ref.dtype)

    def ctx():
        for h in range(NA_HEADS):
            s_c = _dot_t(q_ref[:, head(h)], kc_ref[:, head(h)]) * NA_SCALE
            o_ref[:, head(h)] = _attend(s_c, vc_ref[:, head(h)]).astype(o_ref.dtype)

    _on_query_tiles(with_ctx, lat, ctx)


def _na_attention(proj, bias_tab, with_ctx):
    n_qt = NQ + 1 if with_ctx else NQ
    out_rows = ROWS if with_ctx else T_LAT
    ctx_blk = T_LAT // CTX_LEN
    width = NA_HEADS * HEAD
    kcol, vcol = COL_NA_K // width, COL_NA_V // width
    return pl.pallas_call(
        functools.partial(_na_kernel, with_ctx=with_ctx),
        grid=(BATCH, n_qt),
        in_specs=[pl.BlockSpec((TQ, width), lambda b, t: (_q_row_block(b, t), 0)),
                  pl.BlockSpec((CTX_LEN, width), lambda b, t: (ctx_blk + b, kcol)),
                  pl.BlockSpec((CTX_LEN, width), lambda b, t: (ctx_blk + b, vcol)),
                  pl.BlockSpec((SEQ, width), lambda b, t: (b, kcol)),
                  pl.BlockSpec((SEQ, width), lambda b, t: (b, vcol)),
                  pl.BlockSpec((NA_HEADS, 2 * NA_KH, GRID_W, HEAD), lambda b, t: (0, 0, 0, 0))],
        out_specs=pl.BlockSpec((TQ, width), lambda b, t: (_q_row_block(b, t), 0)),
        out_shape=jax.ShapeDtypeStruct((out_rows, width), BF16),
        scratch_shapes=[pltpu.VMEM((NA_HEADS, TQ, NA_WIN), F32)],
        compiler_params=_cparams(("arbitrary",) * 2),
        name="na_attention",
    )(proj, proj, proj, proj, proj, bias_tab)


def _out_proj_kernel(o0_ref, o1_ref, o2_ref, o3_ref, w_ref, gt_ref, *refs):
    *h_refs, out_ref = refs
    for c in range(0, D_MODEL, 512):
        acc = None
        for t, o_ref in enumerate((o0_ref, o1_ref, o2_ref, o3_ref)):
            part = _dot(o_ref[...], w_ref[t * 512:(t + 1) * 512, c:c + 512])
            acc = part if acc is None else acc + part
        gated = gt_ref[0, :, c:c + 512] * acc
        out_ref[:, c:c + 512] = h_refs[0][:, c:c + 512] + gated if len(h_refs) == 1 else gated

    def residual(h_ref):
        out_ref[...] += h_ref[...]

    if len(h_refs) > 1:
        _with_stream_rows(h_refs, TM_O, residual)


def _out_proj(o_parts, w_out, stream, mod, n_rows):
    o_spec = pl.BlockSpec((TM_O, 512), lambda i: (i, 0))
    return pl.pallas_call(
        _out_proj_kernel,
        grid=(n_rows // TM_O,),
        in_specs=[o_spec, o_spec, o_spec, o_spec,
                  pl.BlockSpec((D_MODEL, D_MODEL), lambda i: (0, 0), pipeline_mode=pl.Buffered(1)),
                  pl.BlockSpec((1, 1, D_MODEL), lambda i: (_mod_row(i, TM_O), 0, 2))]
                 + _stream_specs(stream, TM_O),
        out_specs=pl.BlockSpec((TM_O, D_MODEL), lambda i: (i, 0)),
        out_shape=jax.ShapeDtypeStruct((n_rows, D_MODEL), F32),
        compiler_params=_cparams(("arbitrary",)),
        name="out_proj",
    )(*o_parts, w_out, mod, *stream)


def _ffn_kernel(h_ref, g_ref, sh_ref, sc_ref, gt_ref, wg_ref, wu_ref, wd_ref, out_ref, u_scr):
    j = pl.program_id(1)

    @pl.when(j == 0)
    def _():
        weights = _swiglu_weights(wg_ref, wu_ref, wd_ref)
        for r in range(0, TM, NORM_ROWS):
            u = _norm_modulate(h_ref[r:r + NORM_ROWS], g_ref, sh_ref, sc_ref)
            u_scr[r:r + NORM_ROWS] = u
            _swiglu_accumulate(u, weights, out_ref.at[pl.ds(r, NORM_ROWS)], assign=True)

    @pl.when(j > 0)
    def _():
        _swiglu_accumulate(u_scr[...], _swiglu_weights(wg_ref, wu_ref, wd_ref), out_ref)

    @pl.when(j == pl.num_programs(1) - 1)
    def _():
        out_ref[...] = h_ref[...] + gt_ref[0] * out_ref[...]


def _ffn_dense(h, g, mod, w_gate, w_up, w_down):
    def mod_spec(chunk):
        return pl.BlockSpec((1, 1, D_MODEL), lambda i, j: (_mod_row(i, TM), 0, chunk))

    return pl.pallas_call(
        _ffn_kernel,
        grid=(N_ALL_TILES, D_FF // TF),
        in_specs=[pl.BlockSpec((TM, D_MODEL), lambda i, j: (i, 0), pipeline_mode=pl.Buffered(1)),
                  pl.BlockSpec((1, D_MODEL), lambda i, j: (0, 0)),
                  mod_spec(3), mod_spec(4), mod_spec(5),
                  pl.BlockSpec((D_MODEL, TF), lambda i, j: (0, j)),
                  pl.BlockSpec((D_MODEL, TF), lambda i, j: (0, j)),
                  pl.BlockSpec((TF, D_MODEL), lambda i, j: (j, 0))],
        out_specs=pl.BlockSpec((TM, D_MODEL), lambda i, j: (i, 0)),
        out_shape=jax.ShapeDtypeStruct((ROWS, D_MODEL), F32),
        scratch_shapes=[pltpu.VMEM((TM, D_MODEL), BF16)],
        compiler_params=_cparams(("arbitrary", "arbitrary")),
        name="ffn_dense",
    )(h, g, mod, mod, mod, w_gate, w_up, w_down)


META_LANES = 128


def _route_kernel(h_ref, g_ref, sh_ref, sc_ref, wr_ref, br_ref, u_ref, meta_ref, cnt_ref, carry):
    i = pl.program_id(0)

    @pl.when(i == 0)
    def _():
        carry[...] = jnp.zeros_like(carry)

    y = _rms(h_ref[...]) * g_ref[...]
    u = y * (1.0 + sc_ref[0]) + sh_ref[0]
    u_ref[...] = u

    w = wr_ref[...]
    u_hi = u.astype(BF16)
    u_lo = (u - u_hi.astype(F32)).astype(BF16)
    w_hi = w.astype(BF16)
    w_lo = (w - w_hi.astype(F32)).astype(BF16)
    logits = _dot(u_hi, w_hi) + (_dot(u_hi, w_lo) + _dot(u_lo, w_hi)) + br_ref[...]

    lane = lax.broadcasted_iota(jnp.int32, logits.shape, 1)
    logits = jnp.where(lane < N_EXPERTS, logits, NEG_INF)
    lane_f = lane.astype(F32)
    m1 = jnp.max(logits, axis=-1, keepdims=True)
    i1 = jnp.min(jnp.where(logits == m1, lane_f, float(META_LANES)), axis=-1, keepdims=True)
    pick1 = lane_f == i1
    rest = jnp.where(pick1, NEG_INF, logits)
    m2 = jnp.max(rest, axis=-1, keepdims=True)
    i2 = jnp.min(jnp.where(rest == m2, lane_f, float(META_LANES)), axis=-1, keepdims=True)
    pick2 = lane_f == i2
    e = jnp.exp(m2 - m1)
    g1 = 1.0 / (1.0 + e)
    g2 = e / (1.0 + e)

    onehot = jnp.where(pick1 | pick2, 1.0, 0.0)
    rr = lax.broadcasted_iota(jnp.int32, (TM, TM), 0)
    cc = lax.broadcasted_iota(jnp.int32, (TM, TM), 1)
    earlier = jnp.where(cc < rr, 1.0, 0.0).astype(BF16)
    before = _dot(earlier, onehot.astype(BF16)) + carry[...]
    r1 = jnp.sum(jnp.where(pick1, before, 0.0), axis=-1, keepdims=True)
    r2 = jnp.sum(jnp.where(pick2, before, 0.0), axis=-1, keepdims=True)
    carry[...] += jnp.sum(onehot, axis=0, keepdims=True)

    meta = jnp.zeros(logits.shape, F32)
    for k, val in enumerate((i1, i2, r1, r2, g1, g2)):
        meta = jnp.where(lane == k, val, meta)
    meta_ref[...] = meta
    cnt_ref[...] = jnp.broadcast_to(carry[...], cnt_ref.shape)


def _moe_route(h, g, mod, w_router, b_router):
    def mod_spec(chunk):
        return pl.BlockSpec((1, 1, D_MODEL), lambda i: (_mod_row(i, TM), 0, chunk))

    return pl.pallas_call(
        _route_kernel,
        grid=(N_LAT_TILES,),
        in_specs=[pl.BlockSpec((TM, D_MODEL), lambda i: (i, 0)),
                  pl.BlockSpec((1, D_MODEL), lambda i: (0, 0)),
                  mod_spec(3), mod_spec(4),
                  pl.BlockSpec((D_MODEL, META_LANES), lambda i: (0, 0)),
                  pl.BlockSpec((1, META_LANES), lambda i: (0, 0))],
        out_specs=[pl.BlockSpec((TM, D_MODEL), lambda i: (i, 0)),
                   pl.BlockSpec((TM, META_LANES), lambda i: (i, 0)),
                   pl.BlockSpec((8, META_LANES), lambda i: (0, 0))],
        out_shape=[jax.ShapeDtypeStruct((T_LAT, D_MODEL), F32),
                   jax.ShapeDtypeStruct((T_LAT, META_LANES), F32),
                   jax.ShapeDtypeStruct((8, META_LANES), F32)],
        scratch_shapes=[pltpu.VMEM((1, META_LANES), F32)],
        compiler_params=_cparams(("arbitrary",)),
        name="moe_route",
    )(h, g, mod, mod, w_router, b_router)


def _row_copy(src, s, dst, d, sem):
    return pltpu.make_async_copy(src.at[pl.ds(s, 1)], dst.at[pl.ds(d, 1)], sem)


MOE_NJ = D_FF_EXPERT // TF
GATHER_ROWS_PER_STEP = TM // MOE_NJ


TAIL_ROWS = range(GATHER_ROWS_PER_STEP * MOE_NJ, TM)
TOK_ROWS = 2 * T_LAT + TM
HALF = TM // 2


def _moe_ffn_kernel(te_ref, nu_ref, nv_ref, src_ref, dst_ref, u_ref, wg_ref, wu_ref, wd_ref, tok_ref,
                    xg_scr, xb_scr, acc, sem):
    i, j = pl.program_id(0), pl.program_id(1)
    n_used = nu_ref[0]
    used = i < n_used
    last = n_used - 1
    slot = i % 2
    nxt = jnp.minimum(i + 1, last)
    prev = i - 1

    def get_row(tile, r):
        _row_copy(u_ref, src_ref[tile * TM + r], xg_scr, r, sem.at[0]).start()

    def put_row(tile, s, r):
        _row_copy(acc.at[s], r, tok_ref, dst_ref[(tile + 1) * TM + r], sem.at[1]).start()

    def wait_rows(get):
        def wait(r, c):
            if get:
                _row_copy(u_ref, 0, xg_scr, 0, sem.at[0]).wait()
            else:
                _row_copy(acc.at[0], 0, tok_ref, 0, sem.at[1]).wait()
            return c
        lax.fori_loop(0, TM, wait, 0, unroll=8)

    @pl.when((i == 0) & (j == 0))
    def _():
        lax.fori_loop(0, TM, lambda r, c: (get_row(0, r), c)[1], 0)
        acc[1] = jnp.zeros((TM, D_MODEL), F32)

    @pl.when(used & (j == 0))
    def _():
        wait_rows(True)
        xb_scr[...] = xg_scr[...].astype(BF16)
        acc[slot] = jnp.zeros((TM, D_MODEL), F32)
        for r in TAIL_ROWS:
            get_row(nxt, r)
            put_row(prev, 1 - slot, r)

    def step(rows):
        for r in range(GATHER_ROWS_PER_STEP):
            get_row(nxt, r * MOE_NJ + j)
            put_row(prev, 1 - slot, r * MOE_NJ + j)
        _swiglu_accumulate(xb_scr[:rows], _swiglu_weights(wg_ref, wu_ref, wd_ref), acc.at[slot])

    pl.when(used & (nv_ref[i] > HALF))(lambda: step(TM))
    pl.when(used & (nv_ref[i] <= HALF))(lambda: step(HALF))

    @pl.when(used & (j == MOE_NJ - 1))
    def _():
        wait_rows(False)

    @pl.when((i == last) & (j == MOE_NJ - 1))
    def _():
        wait_rows(True)
        lax.fori_loop(0, TM, lambda r, c: (put_row(i, slot, r), c)[1], 0)
        wait_rows(False)


def _moe_ffn(u, src, dst, tile_expert, n_used, n_valid, w_gate, w_up, w_down):
    def jj(i, j, nu):
        return jnp.where(i < nu[0], j, MOE_NJ - 1)

    grid_spec = pltpu.PrefetchScalarGridSpec(
        num_scalar_prefetch=5,
        grid=(MOE_TILES, MOE_NJ),
        in_specs=[pl.BlockSpec(memory_space=pl.ANY),
                  pl.BlockSpec((None, D_MODEL, TF), lambda i, j, te, nu, *_: (te[i], 0, jj(i, j, nu))),
                  pl.BlockSpec((None, D_MODEL, TF), lambda i, j, te, nu, *_: (te[i], 0, jj(i, j, nu))),
                  pl.BlockSpec((None, TF, D_MODEL), lambda i, j, te, nu, *_: (te[i], jj(i, j, nu), 0))],
        out_specs=pl.BlockSpec(memory_space=pl.ANY),
        scratch_shapes=[pltpu.VMEM((TM, D_MODEL), F32), pltpu.VMEM((TM, D_MODEL), BF16),
                        pltpu.VMEM((2, TM, D_MODEL), F32), pltpu.SemaphoreType.DMA((2,))],
    )
    return pl.pallas_call(
        _moe_ffn_kernel,
        grid_spec=grid_spec,
        out_shape=jax.ShapeDtypeStruct((TOK_ROWS, D_MODEL), F32),
        compiler_params=_cparams(("arbitrary", "arbitrary")),
        name="moe_ffn",
    )(tile_expert, n_used, n_valid, src, dst, u, w_gate, w_up, w_down)


def _combine_kernel(h_ref, meta_ref, gt_ref, gf_ref, y1_ref, y2_ref, out_ref):
    meta = meta_ref[...]
    y = meta[:, 4:5] * y1_ref[...] + meta[:, 5:6] * y2_ref[...]
    hn = h_ref[...] + gt_ref[0] * y
    out_ref[...] = _rms(hn) * gf_ref[...]


def _moe_combine(h, meta, mod, g_final, tok):
    n = T_LAT // TC
    row = pl.BlockSpec((TC, D_MODEL), lambda i: (i, 0))
    return pl.pallas_call(
        _combine_kernel,
        grid=(n,),
        in_specs=[row,
                  pl.BlockSpec((TC, META_LANES), lambda i: (i, 0)),
                  pl.BlockSpec((1, 1, D_MODEL), lambda i: (i // (SEQ // TC), 0, 5)),
                  pl.BlockSpec((1, D_MODEL), lambda i: (0, 0)),
                  row,
                  pl.BlockSpec((TC, D_MODEL), lambda i: (n + i, 0))],
        out_specs=row,
        out_shape=jax.ShapeDtypeStruct((T_LAT, D_MODEL), F32),
        compiler_params=_cparams(("arbitrary",)),
        name="moe_combine",
    )(h, meta, mod, g_final, tok, tok)


def _moe_plan(meta, counts_f):
    i1 = meta[:, 0].astype(jnp.int32)
    i2 = meta[:, 1].astype(jnp.int32)
    r1 = meta[:, 2].astype(jnp.int32)
    r2 = meta[:, 3].astype(jnp.int32)
    counts = counts_f[0, :N_EXPERTS].astype(jnp.int32)
    padded = (counts + TM - 1) // TM * TM
    ends = jnp.cumsum(padded)
    starts = ends - padded
    experts = jnp.arange(N_EXPERTS, dtype=jnp.int32)

    def start_of(idx):
        return jnp.sum(jnp.where(idx[:, None] == experts, starts, 0), axis=1)

    pos = jnp.concatenate([start_of(i1) + r1, start_of(i2) + r2])
    n_used = ends[-1] // TM
    tile_start = jnp.arange(MOE_TILES, dtype=jnp.int32) * TM
    expert = jnp.minimum(jnp.sum(tile_start[:, None] >= ends[None, :], axis=1), N_EXPERTS - 1)
    n_valid = jnp.clip((starts + counts)[expert] - tile_start, 0, TM).astype(jnp.int32)
    expert = expert[jnp.minimum(jnp.arange(MOE_TILES), n_used - 1)].astype(jnp.int32)
    slot = jnp.arange(MOE_ROWS, dtype=jnp.int32)
    owner = jnp.full((MOE_ROWS,), -1, jnp.int32).at[pos].set(jnp.arange(2 * T_LAT, dtype=jnp.int32),
                                                             unique_indices=True, mode="promise_in_bounds")
    spare = 2 * T_LAT + slot % TM
    dst = jnp.concatenate([spare[:TM], jnp.where(owner >= 0, owner, spare)])
    src = jnp.where(owner >= 0, owner % T_LAT, 0)
    return src, dst, expert, n_used.reshape(1).astype(jnp.int32), n_valid


def _rope_tables():
    t = np.arange(SEQ)
    row, col = (t // GRID_W).astype(np.float64), (t % GRID_W).astype(np.float64)
    out = {}
    for dim, name in ((MLA_ROPE, "64"), (HEAD, "128")):
        half = dim // 2
        freqs = ROPE_THETA ** (-np.arange(0, half, 2, dtype=np.float64) / half)
        ar, ac = row[:, None] * freqs, col[:, None] * freqs
        c = np.concatenate([np.cos(ar), np.cos(ar), np.cos(ac), np.cos(ac)], axis=1)
        s = np.concatenate([-np.sin(ar), np.sin(ar), -np.sin(ac), np.sin(ac)], axis=1)
        order = _rope_lane_order(dim)
        c = np.tile(c, (1, HEAD // dim))[:, order]
        s = np.tile(s, (1, HEAD // dim))[:, order]
        out["c" + name] = jnp.asarray(np.concatenate([c, np.ones((TP, HEAD))], axis=0), F32)
        out["s" + name] = jnp.asarray(np.concatenate([s, np.zeros((TP, HEAD))], axis=0), F32)
    return out


def _na_bias_table(rel_bias):
    qc = np.arange(GRID_W)[:, None]
    kc = np.arange(GRID_W)[None, :]
    cs = np.clip(qc - NA_KW // 2, 0, GRID_W - NA_KW)
    col_ok = (kc >= cs) & (kc < cs + NA_KW)
    col_off = np.clip(kc - qc + NA_KW - 1, 0, 2 * NA_KW - 2)
    pick = jnp.asarray(col_off[None] == np.arange(2 * NA_KW - 1)[:, None, None], F32)
    tab = jnp.einsum("had,dqk->haqk", rel_bias * LOG2E, pick, precision=lax.Precision.HIGHEST)
    tab = jnp.where(jnp.asarray(col_ok), tab, NEG_INF)
    tab = jnp.concatenate([tab, jnp.full((NA_HEADS, 1, GRID_W, GRID_W), NEG_INF, F32)], axis=1)
    return jnp.concatenate([tab, tab], axis=-1)


def _layer_weights(w_in, mla_g_q, mla_w_qup, mla_g_kv, mla_w_kvup, gqa_g_q, gqa_g_k):
    cuts = (1536, 2048, 2112, 2880, 3136, 4160)
    na, mla_c, kpe, gqa_qk, gqa_v, diff_qk, diff_v = jnp.split(w_in, cuts, axis=2)
    kpe = _to_rope_lane_order(jnp.pad(kpe, ((0, 0), (0, 0), (0, HEAD - MLA_ROPE))), MLA_ROPE)
    gqa_qk = _to_rope_lane_order(gqa_qk, HEAD)
    diff_qk = _to_rope_lane_order(diff_qk, DIFF_QK)
    pad = jnp.zeros((DEPTH, D_MODEL, IN_COLS_PAD - COL_KPE - HEAD), F32)
    w_in_p = jnp.concatenate([na, mla_c, gqa_qk, gqa_v, diff_qk, diff_v, kpe, pad], axis=2).astype(BF16)

    o64, o128 = _rope_lane_order(MLA_ROPE), _rope_lane_order(HEAD)
    qup = mla_w_qup.reshape(DEPTH, MLA_Q_RANK, MLA_HEADS, MLA_NOPE + MLA_ROPE)
    qup = jnp.pad(qup, ((0, 0), (0, 0), (0, 0), (0, 2 * HEAD - MLA_NOPE - MLA_ROPE)))
    qup = jnp.concatenate([qup[..., :HEAD], qup[..., HEAD:][..., o64]], axis=-1)
    kvup = mla_w_kvup.reshape(DEPTH, MLA_KV_RANK, MLA_HEADS, MLA_NOPE + MLA_V)
    return {
        "w_in": w_in_p,
        "g_cq": mla_g_q.reshape(DEPTH, 1, -1),
        "w_qup": qup.reshape(DEPTH, MLA_Q_RANK, MLA_HEADS * 2 * HEAD).astype(BF16),
        "g_ckv": mla_g_kv.reshape(DEPTH, 1, -1),
        "w_kvk": kvup[..., :MLA_NOPE].reshape(DEPTH, MLA_KV_RANK, -1).astype(BF16),
        "w_kvv": kvup[..., MLA_NOPE:].reshape(DEPTH, MLA_KV_RANK, -1).astype(BF16),
        "g_gq": gqa_g_q[:, o128].reshape(DEPTH, 1, -1),
        "g_gk": gqa_g_k[:, o128].reshape(DEPTH, 1, -1),
    }


def kernel(x, c, ctx, c_ctx, w_mod, b_mod, g_mix, w_in, na_rel_bias, mla_g_q, mla_w_qup, mla_g_kv, mla_w_kvup,
           gqa_g_q, gqa_g_k, diff_lq1, diff_lk1, diff_lq2, diff_lk2, diff_g_sub, w_out, g_ffn,
           ffn_w_gate, ffn_w_up, ffn_w_down, moe_w_router, moe_b_router, moe_w_gate, moe_w_up, moe_w_down,
           g_final):
    tabs = _rope_tables()
    lw = _layer_weights(w_in, mla_g_q, mla_w_qup, mla_g_kv, mla_w_kvup, gqa_g_q, gqa_g_k)
    cvec =jnp.concatenate([c, c_ctx[None, :], jnp.zeros((16 - BATCH - 1, D_MODEL), F32)], axis=0)
    mod_all = _modulation(cvec, w_mod, b_mod)
    stream = (x.reshape(T_LAT, D_MODEL), ctx.reshape(T_CTX, D_MODEL))

    out = None
    for l in range(DEPTH):
        last = l == DEPTH - 1
        with_ctx = not last
        lam_init = 0.8 - 0.6 * math.exp(-0.3 * l)
        mod = mod_all[l].reshape(16, 1, 6 * D_MODEL)

        proj = _in_proj(stream, g_mix[l].reshape(1, -1), mod, lw["w_in"], l)
        q_mla, k_mla, v_mla, q_gqa, k_gqa, q_diff, k_diff = _prep(proj, lw, l, tabs)

        o_na = _na_attention(proj, _na_bias_table(na_rel_bias[l]), with_ctx)
        o_mla = _global_attention(q_mla, 2 * HEAD, k_mla, 2 * HEAD, MLA_HEADS, 0, v_mla, 0, with_ctx)
        o_gqa = _global_attention(q_gqa, HEAD, k_gqa, HEAD, GQA_KV_HEADS, 0,
                                  proj, COL_GQA_V // (GQA_KV_HEADS * HEAD), with_ctx)
        lam_vecs = jnp.stack([diff_lq1[l], diff_lk1[l], diff_lq2[l], diff_lk2[l]])
        o_diff = _global_attention(q_diff, HEAD, k_diff, HEAD, DIFF_HEADS, 0,
                                   proj, COL_DIFF_V // (DIFF_HEADS * HEAD), with_ctx,
                                   diff=(lam_vecs, diff_g_sub[l].reshape(1, -1), lam_init))

        h = _out_proj((o_na, o_mla, o_gqa, o_diff), w_out[l].astype(BF16), stream, mod,
                      ROWS if with_ctx else T_LAT)

        if l % 2 == 0:
            h = _ffn_dense(h, g_ffn[l].reshape(1, -1), mod, ffn_w_gate[l // 2], ffn_w_up[l // 2],
                           ffn_w_down[l // 2])
            stream = (h,)
        else:
            m = l // 2
            w_r = jnp.pad(moe_w_router[m], ((0, 0), (0, META_LANES - N_EXPERTS)))
            b_r = jnp.pad(moe_b_router[m], (0, META_LANES - N_EXPERTS)).reshape(1, -1)
            u, meta, counts = _moe_route(h, g_ffn[l].reshape(1, -1), mod, w_r, b_r)
            src, dst, tile_expert, n_used, n_valid = _moe_plan(meta, counts)
            tok = _moe_ffn(u, src, dst, tile_expert, n_used, n_valid, moe_w_gate[m], moe_w_up[m], moe_w_down[m])
            out = _moe_combine(h, meta, mod, g_final.reshape(1, -1), tok)
    return out.reshape(BATCH, SEQ, D_MODEL)
```

```python
import functools
import math

import numpy as np
import jax
import jax.numpy as jnp
from jax import lax
from jax.experimental import pallas as pl
from jax.experimental.pallas import tpu as pltpu

F32 = jnp.float32
BF16 = jnp.bfloat16

D_MODEL = 2048
BATCH = 8
SEQ = 2048
DEPTH = 2
GRID_W = 64
GRID_H = SEQ // GRID_W
CTX_LEN = 256
ROPE_THETA = 10000.0
EPS = 1e-6
NEG_INF = -1e30

NA_HEADS = 4
NA_KH = 8
NA_KW = 16
MLA_HEADS = 4
MLA_NOPE = 128
MLA_ROPE = 64
MLA_V = 128
MLA_Q_RANK = 384
MLA_KV_RANK = 128
GQA_HEADS = 4
GQA_KV_HEADS = 2
DIFF_HEADS = 4
DIFF_QK = 64
HEAD = 128

LOG2E = math.log2(math.e)
NA_SCALE = HEAD ** -0.5 * LOG2E
MLA_SCALE = (MLA_NOPE + MLA_ROPE) ** -0.5 * LOG2E
GQA_SCALE = HEAD ** -0.5 * LOG2E
DIFF_SCALE = DIFF_QK ** -0.5 * LOG2E

D_FF = 5632
N_EXPERTS = 8
D_FF_EXPERT = 7168

T_LAT = BATCH * SEQ
T_CTX = BATCH * CTX_LEN
ROWS = T_LAT + T_CTX

COL_NA_Q, COL_NA_K, COL_NA_V = 0, 512, 1024
COL_MLA_C = 1536
COL_GQA_Q, COL_GQA_K, COL_GQA_V = 2048, 2560, 2816
COL_DIFF_Q, COL_DIFF_K, COL_DIFF_V = 3072, 3584, 4096
COL_KPE = 4608
IN_COLS_PAD = 5120

V7X_VMEM_LIMIT = 56 * 1024 * 1024

TM = 1024
N_LAT_TILES = T_LAT // TM
N_ALL_TILES = ROWS // TM
TILES_PER_BATCH = SEQ // TM
TQ = 256
NQ = SEQ // TQ
TP = 512
TN_IN = 1024
V7X_MXU_WIDTH = 256
IN_LAST_COLS = -(-(COL_KPE + HEAD - (IN_COLS_PAD - TN_IN)) // V7X_MXU_WIDTH) * V7X_MXU_WIDTH
TM_O = 512
TF = 256
NORM_ROWS = 256
MOE_TILES = 2 * T_LAT // TM + N_EXPERTS
MOE_ROWS = MOE_TILES * TM
TC = 512


def _cparams(sem, vmem=V7X_VMEM_LIMIT):
    return pltpu.CompilerParams(dimension_semantics=sem, vmem_limit_bytes=vmem)


def _mod_row(i, tm):
    return jnp.where(i < T_LAT // tm, i // (SEQ // tm), BATCH)


def _dot(a, b):
    return jnp.dot(a, b, preferred_element_type=F32)


def _dot_t(a, b):
    return lax.dot_general(a, b, (((1,), (1,)), ((), ())), preferred_element_type=F32)


def _rms(x):
    return x * lax.rsqrt(jnp.mean(x * x, axis=-1, keepdims=True) + EPS)


def _silu(x):
    return x * jax.nn.sigmoid(x)


def _swiglu_weights(wg_ref, wu_ref, wd_ref):
    return wg_ref[...].astype(BF16), wu_ref[...].astype(BF16), wd_ref[...].astype(BF16)


def _swiglu_accumulate(x, weights, out_ref, assign=False):
    wg, wu, wd = weights
    rows = x.shape[0]
    a = (_silu(_dot(x, wg)) * _dot(x, wu)).astype(BF16)
    for c in range(0, D_MODEL, 512):
        part = _dot(a, wd[:, c:c + 512])
        if assign:
            out_ref[:rows, c:c + 512] = part
        else:
            out_ref[:rows, c:c + 512] += part


def _norm_modulate(x, g_ref, sh_ref, sc_ref):
    y = _rms(x) * g_ref[...]
    return (y * (1.0 + sc_ref[0]) + sh_ref[0]).astype(BF16)


def _mod_kernel(c_ref, w_ref, b_ref, o_ref):
    s = _silu(c_ref[...]).astype(BF16)
    o_ref[0] = _dot(s, w_ref[0].astype(BF16)) + b_ref[0]


def _modulation(cvec, w_mod, b_mod):
    tn = 1024
    return pl.pallas_call(
        _mod_kernel,
        grid=(DEPTH, 6 * D_MODEL // tn),
        in_specs=[pl.BlockSpec((16, D_MODEL), lambda l, j: (0, 0)),
                  pl.BlockSpec((1, D_MODEL, tn), lambda l, j: (l, 0, j)),
                  pl.BlockSpec((1, 1, tn), lambda l, j: (l, 0, j))],
        out_specs=pl.BlockSpec((1, 16, tn), lambda l, j: (l, 0, j)),
        out_shape=jax.ShapeDtypeStruct((DEPTH, 16, 6 * D_MODEL), F32),
        compiler_params=_cparams(("arbitrary", "arbitrary")),
        name="modulation",
    )(cvec, w_mod, b_mod.reshape(DEPTH, 1, 6 * D_MODEL))


def _stream_specs(stream, tm):
    if len(stream) == 1:
        return [pl.BlockSpec((tm, D_MODEL), lambda i, *_: (i, 0))]
    n_lat = T_LAT // tm
    return [pl.BlockSpec((tm, D_MODEL), lambda i, *_: (jnp.minimum(i, n_lat - 1), 0)),
            pl.BlockSpec((tm, D_MODEL), lambda i, *_: (jnp.maximum(i - n_lat, 0), 0))]


def _with_stream_rows(h_refs, tm, fn):
    if len(h_refs) == 1:
        fn(h_refs[0])
        return
    i = pl.program_id(0)
    n_lat = T_LAT // tm
    pl.when(i < n_lat)(lambda: fn(h_refs[0]))
    pl.when(i >= n_lat)(lambda: fn(h_refs[1]))


def _in_proj_kernel(*refs):
    *h_refs, g_ref, sh_ref, sc_ref, w_ref, o_ref, u_scr = refs

    j = pl.program_id(1)

    def first(h_ref):
        w = w_ref[...]
        for r in range(0, TM, NORM_ROWS):
            u = _norm_modulate(h_ref[r:r + NORM_ROWS], g_ref, sh_ref, sc_ref)
            u_scr[r:r + NORM_ROWS] = u
            o_ref[r:r + NORM_ROWS] = _dot(u, w).astype(o_ref.dtype)

    @pl.when(j == 0)
    def _():
        _with_stream_rows(h_refs, TM, first)

    last = pl.num_programs(1) - 1

    @pl.when((j > 0) & (j < last))
    def _():
        o_ref[...] = _dot(u_scr[...], w_ref[...]).astype(o_ref.dtype)

    @pl.when(j == last)
    def _():
        o_ref[:, :IN_LAST_COLS] = _dot(u_scr[...], w_ref[:, :IN_LAST_COLS]).astype(o_ref.dtype)
        o_ref[:, IN_LAST_COLS:] = jnp.zeros((TM, TN_IN - IN_LAST_COLS), o_ref.dtype)


def _in_proj(stream, g, mod, w_in, l):
    return pl.pallas_call(
        _in_proj_kernel,
        grid=(N_ALL_TILES, IN_COLS_PAD // TN_IN),
        in_specs=_stream_specs(stream, TM) + [
                  pl.BlockSpec((1, D_MODEL), lambda i, j: (0, 0)),
                  pl.BlockSpec((1, 1, D_MODEL), lambda i, j: (_mod_row(i, TM), 0, 0)),
                  pl.BlockSpec((1, 1, D_MODEL), lambda i, j: (_mod_row(i, TM), 0, 1)),
                  pl.BlockSpec((None, D_MODEL, TN_IN), lambda i, j: (l, 0, j))],
        out_specs=pl.BlockSpec((TM, TN_IN), lambda i, j: (i, j)),
        out_shape=jax.ShapeDtypeStruct((ROWS, IN_COLS_PAD), BF16),
        scratch_shapes=[pltpu.VMEM((TM, D_MODEL), BF16)],
        compiler_params=_cparams(("arbitrary", "arbitrary")),
        name="in_proj",
    )(*stream, g, mod, mod, w_in)


def _rope_lane_order(dim):
    q, n_vec = dim // 4, HEAD // dim
    order = np.empty(HEAD, np.int64)
    for b in range(2):
        for v in range(n_vec):
            for a in range(2):
                for i in range(q):
                    order[b * (HEAD // 2) + v * 2 * q + a * q + i] = v * dim + a * 2 * q + b * q + i
    return order


def _to_rope_lane_order(w, dim):
    q, n_vec = dim // 4, HEAD // dim
    blocks = w.reshape(w.shape[:-1] + (-1, n_vec, 2, 2, q))
    return jnp.moveaxis(blocks, -2, -4).reshape(w.shape)


def _rope(x, c, s):
    return x * c + pltpu.roll(x, HEAD // 2, 1) * s


def _prep_kernel(mc_ref, gq_ref, gk_ref, dq_ref, dk_ref, kpe_ref,
                 g_cq_ref, w_qup_ref, g_ckv_ref, w_kvk_ref, w_kvv_ref, g_gq_ref, g_gk_ref,
                 c64_ref, s64_ref, c128_ref, s128_ref,
                 qm_ref, km_ref, vm_ref, qg_ref, kg_ref, qd_ref, kd_ref):
    c64, s64 = c64_ref[...], s64_ref[...]
    c128, s128 = c128_ref[...], s128_ref[...]

    mc = mc_ref[...].astype(F32)
    cq = (_rms(mc[:, :MLA_Q_RANK]) * g_cq_ref[...]).astype(BF16)
    ckv = (_rms(mc[:, MLA_Q_RANK:]) * g_ckv_ref[...]).astype(BF16)
    q = _dot(cq, w_qup_ref[...])
    kn = _dot(ckv, w_kvk_ref[...])
    vm_ref[...] = _dot(ckv, w_kvv_ref[...]).astype(BF16)
    kpe = _rope(kpe_ref[...].astype(F32), c64, s64).astype(BF16)
    for h in range(MLA_HEADS):
        lo = 2 * HEAD * h
        qm_ref[:, lo:lo + HEAD] = (q[:, lo:lo + HEAD] * MLA_SCALE).astype(BF16)
        pe = _rope(q[:, lo + HEAD:lo + 2 * HEAD], c64, s64)
        qm_ref[:, lo + HEAD:lo + 2 * HEAD] = (pe * MLA_SCALE).astype(BF16)
        km_ref[:, lo:lo + HEAD] = kn[:, h * HEAD:(h + 1) * HEAD].astype(BF16)
        km_ref[:, lo + HEAD:lo + 2 * HEAD] = kpe

    for h in range(GQA_HEADS):
        sl = slice(h * HEAD, (h + 1) * HEAD)
        z = _rms(gq_ref[:, sl].astype(F32)) * g_gq_ref[...]
        qg_ref[:, sl] = (_rope(z, c128, s128) * GQA_SCALE).astype(BF16)
    for h in range(GQA_KV_HEADS):
        sl = slice(h * HEAD, (h + 1) * HEAD)
        z = _rms(gk_ref[:, sl].astype(F32)) * g_gk_ref[...]
        kg_ref[:, sl] = _rope(z, c128, s128).astype(BF16)

    for h in range(DIFF_HEADS):
        sl = slice(h * HEAD, (h + 1) * HEAD)
        qd_ref[:, sl] = (_rope(dq_ref[:, sl].astype(F32), c64, s64) * DIFF_SCALE).astype(BF16)
        kd_ref[:, sl] = _rope(dk_ref[:, sl].astype(F32), c64, s64).astype(BF16)


def _prep(proj, lw, l, tabs):
    n_tiles = ROWS // TP
    lat_tiles = T_LAT // TP
    per_batch = SEQ // TP

    def col(width, start):
        return pl.BlockSpec((TP, width), lambda i: (i, start // width))

    def full(shape):
        return pl.BlockSpec((None,) + shape, lambda i: (l,) + (0,) * len(shape))

    def tab():
        return pl.BlockSpec((TP, HEAD), lambda i: (jnp.where(i < lat_tiles, i % per_batch, per_batch), 0))

    def out(width):
        return pl.BlockSpec((TP, width), lambda i: (i, 0))

    widths = (1024, 1024, 512, 512, 256, 512, 512)
    return pl.pallas_call(
        _prep_kernel,
        grid=(n_tiles,),
        in_specs=[col(512, COL_MLA_C), col(512, COL_GQA_Q), col(256, COL_GQA_K),
                  col(512, COL_DIFF_Q), col(512, COL_DIFF_K), col(128, COL_KPE),
                  full((1, MLA_Q_RANK)), full((MLA_Q_RANK, 1024)), full((1, MLA_KV_RANK)),
                  full((MLA_KV_RANK, 512)), full((MLA_KV_RANK, 512)), full((1, HEAD)), full((1, HEAD)),
                  tab(), tab(), tab(), tab()],
        out_specs=[out(w) for w in widths],
        out_shape=[jax.ShapeDtypeStruct((ROWS, w), BF16) for w in widths],
        compiler_params=_cparams(("arbitrary",)),
        name="prep",
    )(proj, proj, proj, proj, proj, proj,
      lw["g_cq"], lw["w_qup"], lw["g_ckv"], lw["w_kvk"], lw["w_kvv"], lw["g_gq"], lw["g_gk"],
      tabs["c64"], tabs["s64"], tabs["c128"], tabs["s128"])


N_HEADS = 4
N_KEYS = CTX_LEN + SEQ


def _attend(s_c, vc, s_l=None, vl=None):
    m = jnp.max(s_c, axis=-1, keepdims=True)
    if s_l is not None:
        m = jnp.maximum(m, jnp.max(s_l, axis=-1, keepdims=True))
    p_c = jnp.exp2(s_c - m)
    den = jnp.sum(p_c, axis=-1, keepdims=True)
    o = _dot(p_c.astype(BF16), vc)
    if s_l is not None:
        p_l = jnp.exp2(s_l - m)
        den = den + jnp.sum(p_l, axis=-1, keepdims=True)
        o = o + _dot(p_l.astype(BF16), vl)
    return o / den


def _attend_ones(q, k, v_ones):
    s = _dot_t(q, k)
    p = jnp.exp2(s - jnp.max(s, axis=-1, keepdims=True)).astype(BF16)
    o = _dot(p, v_ones)
    return o[:, :HEAD] / o[:, HEAD:HEAD + 1]


def _on_query_tiles(with_ctx, lat_fn, ctx_fn):
    if not with_ctx:
        lat_fn()
        return
    qt = pl.program_id(1)
    pl.when(qt < NQ)(lat_fn)
    pl.when(qt == NQ)(ctx_fn)


def _global_kernel(q_ref, kc_ref, vc_ref, kl_ref, vl_ref, *rest, with_ctx, n_kv, q_w, k_w, lam_init):
    if lam_init is None:
        o_ref, kcat, vcat = rest
    else:
        lam_ref, g_ref, o_ref, kcat, vcat = rest

    @pl.when(pl.program_id(1) == 0)
    def _():
        lane = lax.broadcasted_iota(jnp.int32, (N_KEYS, HEAD), 1)
        ones = jnp.where(lane == 0, 1.0, 0.0).astype(BF16)
        for kv in range(n_kv):
            kcat[kv, :CTX_LEN] = kc_ref[:, kv * k_w:(kv + 1) * k_w]
            kcat[kv, CTX_LEN:] = kl_ref[:, kv * k_w:(kv + 1) * k_w]
            vcat[kv, :CTX_LEN, :HEAD] = vc_ref[:, kv * HEAD:(kv + 1) * HEAD]
            vcat[kv, CTX_LEN:, :HEAD] = vl_ref[:, kv * HEAD:(kv + 1) * HEAD]
            vcat[kv, :, HEAD:] = ones

    if lam_init is not None:
        lv = lam_ref[...]
        lam = (jnp.exp(jnp.sum(lv[0:1] * lv[1:2], axis=-1, keepdims=True))
               - jnp.exp(jnp.sum(lv[2:3] * lv[3:4], axis=-1, keepdims=True)) + lam_init)

    def run(n_keys):
        for h in range(N_HEADS):
            kv = h // (N_HEADS // n_kv)
            k, v = kcat[kv, :n_keys], vcat[kv, :n_keys]
            q = q_ref[:, h * q_w:(h + 1) * q_w]
            if lam_init is None:
                o = _attend_ones(q, k, v)
            else:
                first = (lax.broadcasted_iota(jnp.int32, q.shape, 1) & (DIFF_QK // 2)) == 0
                zero = jnp.zeros_like(q)
                o1 = _attend_ones(jnp.where(first, q, zero), k, v)
                o2 = _attend_ones(jnp.where(first, zero, q), k, v)
                o = _rms(o1 - lam * o2) * g_ref[...] * (1.0 - lam_init)
            o_ref[:, h * HEAD:(h + 1) * HEAD] = o.astype(o_ref.dtype)

    _on_query_tiles(with_ctx, lambda: run(N_KEYS), lambda: run(CTX_LEN))


def _q_row_block(b, qt):
    return jnp.where(qt < NQ, b * NQ + qt, T_LAT // TQ + b)


def _global_attention(q, q_w, k, k_w, n_kv, k_col, v, v_col, with_ctx, diff=None):
    n_qt = NQ + 1 if with_ctx else NQ
    out_rows = ROWS if with_ctx else T_LAT
    ctx_blk = T_LAT // CTX_LEN

    in_specs = [
        pl.BlockSpec((TQ, N_HEADS * q_w), lambda b, t: (_q_row_block(b, t), 0)),
        pl.BlockSpec((CTX_LEN, n_kv * k_w), lambda b, t: (ctx_blk + b, k_col)),
        pl.BlockSpec((CTX_LEN, n_kv * HEAD), lambda b, t: (ctx_blk + b, v_col)),
        pl.BlockSpec((SEQ, n_kv * k_w), lambda b, t: (b, k_col)),
        pl.BlockSpec((SEQ, n_kv * HEAD), lambda b, t: (b, v_col)),
    ]
    args = [q, k, v, k, v]
    lam_init = None
    if diff is not None:
        lam_vecs, g_sub, lam_init = diff
        in_specs += [pl.BlockSpec((4, DIFF_QK), lambda b, t: (0, 0)),
                     pl.BlockSpec((1, HEAD), lambda b, t: (0, 0))]
        args += [lam_vecs, g_sub]
    return pl.pallas_call(
        functools.partial(_global_kernel, with_ctx=with_ctx, n_kv=n_kv, q_w=q_w, k_w=k_w, lam_init=lam_init),
        grid=(BATCH, n_qt),
        in_specs=in_specs,
        out_specs=pl.BlockSpec((TQ, N_HEADS * HEAD), lambda b, t: (_q_row_block(b, t), 0)),
        out_shape=jax.ShapeDtypeStruct((out_rows, N_HEADS * HEAD), BF16),
        scratch_shapes=[pltpu.VMEM((n_kv, N_KEYS, k_w), BF16), pltpu.VMEM((n_kv, N_KEYS, 2 * HEAD), BF16)],
        compiler_params=_cparams(("arbitrary",) * 2),
        name="diff_attention" if diff is not None else "global_attention",
    )(*args)


NA_QROWS = TQ // GRID_W
NA_WROWS = 12
NA_WIN = NA_WROWS * GRID_W
NA_MASKED = 2 * NA_KH - 1


def _na_kernel(q_ref, kc_ref, vc_ref, k_ref, v_ref, bias_ref, o_ref, bias_scr, *, with_ctx):
    def head(h):
        return slice(h * HEAD, (h + 1) * HEAD)

    def lat():
        r0 = pl.program_id(1) * NA_QROWS
        ws = jnp.clip(r0 - NA_KH // 2, 0, GRID_H - NA_WROWS)
        start = pl.multiple_of(ws * GRID_W, TQ)
        block = {}
        for i in range(NA_QROWS):
            qr = r0 + i
            rs = jnp.clip(qr - NA_KH // 2, 0, GRID_H - NA_KH)
            for j in range(NA_WROWS):
                kr = ws + j
                block[i, j] = jnp.where((kr >= rs) & (kr < rs + NA_KH), kr - qr + NA_KH - 1, NA_MASKED)
        for h in range(NA_HEADS):
            for (i, j), a in block.items():
                half = (j % 2) * GRID_W
                bias_scr[h, i * GRID_W:(i + 1) * GRID_W, j * GRID_W:(j + 1) * GRID_W] = (
                    bias_ref[h, a][:, half:half + GRID_W])
            q = q_ref[:, head(h)]
            kw = k_ref[pl.ds(start, NA_WIN), head(h)]
            vw = v_ref[pl.ds(start, NA_WIN), head(h)]
            s_l = _dot_t(q, kw) * NA_SCALE + bias_scr[h]
            s_c = _dot_t(q, kc_ref[:, head(h)]) * NA_SCALE
            o_ref[:, head(h)] = _attend(s_c, vc_ref[:, head(h)], s_l, vw).astype(o_ref.dtype)

    def ctx():
        for h in range(NA_HEADS):
            s_c = _dot_t(q_ref[:, head(h)], kc_ref[:, head(h)]) * NA_SCALE
            o_ref[:, head(h)] = _attend(s_c, vc_ref[:, head(h)]).astype(o_ref.dtype)

    _on_query_tiles(with_ctx, lat, ctx)


def _na_attention(proj, bias_tab, with_ctx):
    n_qt = NQ + 1 if with_ctx else NQ
    out_rows = ROWS if with_ctx else T_LAT
    ctx_blk = T_LAT // CTX_LEN
    width = NA_HEADS * HEAD
    kcol, vcol = COL_NA_K // width, COL_NA_V // width
    return pl.pallas_call(
        functools.partial(_na_kernel, with_ctx=with_ctx),
        grid=(BATCH, n_qt),
        in_specs=[pl.BlockSpec((TQ, width), lambda b, t: (_q_row_block(b, t), 0)),
                  pl.BlockSpec((CTX_LEN, width), lambda b, t: (ctx_blk + b, kcol)),
                  pl.BlockSpec((CTX_LEN, width), lambda b, t: (ctx_blk + b, vcol)),
                  pl.BlockSpec((SEQ, width), lambda b, t: (b, kcol)),
                  pl.BlockSpec((SEQ, width), lambda b, t: (b, vcol)),
                  pl.BlockSpec((NA_HEADS, 2 * NA_KH, GRID_W, HEAD), lambda b, t: (0, 0, 0, 0))],
        out_specs=pl.BlockSpec((TQ, width), lambda b, t: (_q_row_block(b, t), 0)),
        out_shape=jax.ShapeDtypeStruct((out_rows, width), BF16),
        scratch_shapes=[pltpu.VMEM((NA_HEADS, TQ, NA_WIN), F32)],
        compiler_params=_cparams(("arbitrary",) * 2),
        name="na_attention",
    )(proj, proj, proj, proj, proj, bias_tab)


def _out_proj_kernel(o0_ref, o1_ref, o2_ref, o3_ref, w_ref, gt_ref, *refs):
    *h_refs, out_ref = refs
    for c in range(0, D_MODEL, 512):
        acc = None
        for t, o_ref in enumerate((o0_ref, o1_ref, o2_ref, o3_ref)):
            part = _dot(o_ref[...], w_ref[t * 512:(t + 1) * 512, c:c + 512])
            acc = part if acc is None else acc + part
        gated = gt_ref[0, :, c:c + 512] * acc
        out_ref[:, c:c + 512] = h_refs[0][:, c:c + 512] + gated if len(h_refs) == 1 else gated

    def residual(h_ref):
        out_ref[...] += h_ref[...]

    if len(h_refs) > 1:
        _with_stream_rows(h_refs, TM_O, residual)


def _out_proj(o_parts, w_out, stream, mod, n_rows):
    o_spec = pl.BlockSpec((TM_O, 512), lambda i: (i, 0))
    return pl.pallas_call(
        _out_proj_kernel,
        grid=(n_rows // TM_O,),
        in_specs=[o_spec, o_spec, o_spec, o_spec,
                  pl.BlockSpec((D_MODEL, D_MODEL), lambda i: (0, 0), pipeline_mode=pl.Buffered(1)),
                  pl.BlockSpec((1, 1, D_MODEL), lambda i: (_mod_row(i, TM_O), 0, 2))]
                 + _stream_specs(stream, TM_O),
        out_specs=pl.BlockSpec((TM_O, D_MODEL), lambda i: (i, 0)),
        out_shape=jax.ShapeDtypeStruct((n_rows, D_MODEL), F32),
        compiler_params=_cparams(("arbitrary",)),
        name="out_proj",
    )(*o_parts, w_out, mod, *stream)


def _ffn_kernel(h_ref, g_ref, sh_ref, sc_ref, gt_ref, wg_ref, wu_ref, wd_ref, out_ref, u_scr):
    j = pl.program_id(1)

    @pl.when(j == 0)
    def _():
        weights = _swiglu_weights(wg_ref, wu_ref, wd_ref)
        for r in range(0, TM, NORM_ROWS):
            u = _norm_modulate(h_ref[r:r + NORM_ROWS], g_ref, sh_ref, sc_ref)
            u_scr[r:r + NORM_ROWS] = u
            _swiglu_accumulate(u, weights, out_ref.at[pl.ds(r, NORM_ROWS)], assign=True)

    @pl.when(j > 0)
    def _():
        _swiglu_accumulate(u_scr[...], _swiglu_weights(wg_ref, wu_ref, wd_ref), out_ref)

    @pl.when(j == pl.num_programs(1) - 1)
    def _():
        out_ref[...] = h_ref[...] + gt_ref[0] * out_ref[...]


def _ffn_dense(h, g, mod, w_gate, w_up, w_down):
    def mod_spec(chunk):
        return pl.BlockSpec((1, 1, D_MODEL), lambda i, j: (_mod_row(i, TM), 0, chunk))

    return pl.pallas_call(
        _ffn_kernel,
        grid=(N_ALL_TILES, D_FF // TF),
        in_specs=[pl.BlockSpec((TM, D_MODEL), lambda i, j: (i, 0), pipeline_mode=pl.Buffered(1)),
                  pl.BlockSpec((1, D_MODEL), lambda i, j: (0, 0)),
                  mod_spec(3), mod_spec(4), mod_spec(5),
                  pl.BlockSpec((D_MODEL, TF), lambda i, j: (0, j)),
                  pl.BlockSpec((D_MODEL, TF), lambda i, j: (0, j)),
                  pl.BlockSpec((TF, D_MODEL), lambda i, j: (j, 0))],
        out_specs=pl.BlockSpec((TM, D_MODEL), lambda i, j: (i, 0)),
        out_shape=jax.ShapeDtypeStruct((ROWS, D_MODEL), F32),
        scratch_shapes=[pltpu.VMEM((TM, D_MODEL), BF16)],
        compiler_params=_cparams(("arbitrary", "arbitrary")),
        name="ffn_dense",
    )(h, g, mod, mod, mod, w_gate, w_up, w_down)


META_LANES = 128


def _route_kernel(h_ref, g_ref, sh_ref, sc_ref, wr_ref, br_ref, u_ref, meta_ref, cnt_ref, carry):
    i = pl.program_id(0)

    @pl.when(i == 0)
    def _():
        carry[...] = jnp.zeros_like(carry)

    y = _rms(h_ref[...]) * g_ref[...]
    u = y * (1.0 + sc_ref[0]) + sh_ref[0]
    u_ref[...] = u

    w = wr_ref[...]
    u_hi = u.astype(BF16)
    u_lo = (u - u_hi.astype(F32)).astype(BF16)
    w_hi = w.astype(BF16)
    w_lo = (w - w_hi.astype(F32)).astype(BF16)
    logits = _dot(u_hi, w_hi) + (_dot(u_hi, w_lo) + _dot(u_lo, w_hi)) + br_ref[...]

    lane = lax.broadcasted_iota(jnp.int32, logits.shape, 1)
    logits = jnp.where(lane < N_EXPERTS, logits, NEG_INF)
    lane_f = lane.astype(F32)
    m1 = jnp.max(logits, axis=-1, keepdims=True)
    i1 = jnp.min(jnp.where(logits == m1, lane_f, float(META_LANES)), axis=-1, keepdims=True)
    pick1 = lane_f == i1
    rest = jnp.where(pick1, NEG_INF, logits)
    m2 = jnp.max(rest, axis=-1, keepdims=True)
    i2 = jnp.min(jnp.where(rest == m2, lane_f, float(META_LANES)), axis=-1, keepdims=True)
    pick2 = lane_f == i2
    e = jnp.exp(m2 - m1)
    g1 = 1.0 / (1.0 + e)
    g2 = e / (1.0 + e)

    onehot = jnp.where(pick1 | pick2, 1.0, 0.0)
    rr = lax.broadcasted_iota(jnp.int32, (TM, TM), 0)
    cc = lax.broadcasted_iota(jnp.int32, (TM, TM), 1)
    earlier = jnp.where(cc < rr, 1.0, 0.0).astype(BF16)
    before = _dot(earlier, onehot.astype(BF16)) + carry[...]
    r1 = jnp.sum(jnp.where(pick1, before, 0.0), axis=-1, keepdims=True)
    r2 = jnp.sum(jnp.where(pick2, before, 0.0), axis=-1, keepdims=True)
    carry[...] += jnp.sum(onehot, axis=0, keepdims=True)

    meta = jnp.zeros(logits.shape, F32)
    for k, val in enumerate((i1, i2, r1, r2, g1, g2)):
        meta = jnp.where(lane == k, val, meta)
    meta_ref[...] = meta
    cnt_ref[...] = jnp.broadcast_to(carry[...], cnt_ref.shape)


def _moe_route(h, g, mod, w_router, b_router):
    def mod_spec(chunk):
        return pl.BlockSpec((1, 1, D_MODEL), lambda i: (_mod_row(i, TM), 0, chunk))

    return pl.pallas_call(
        _route_kernel,
        grid=(N_LAT_TILES,),
        in_specs=[pl.BlockSpec((TM, D_MODEL), lambda i: (i, 0)),
                  pl.BlockSpec((1, D_MODEL), lambda i: (0, 0)),
                  mod_spec(3), mod_spec(4),
                  pl.BlockSpec((D_MODEL, META_LANES), lambda i: (0, 0)),
                  pl.BlockSpec((1, META_LANES), lambda i: (0, 0))],
        out_specs=[pl.BlockSpec((TM, D_MODEL), lambda i: (i, 0)),
                   pl.BlockSpec((TM, META_LANES), lambda i: (i, 0)),
                   pl.BlockSpec((8, META_LANES), lambda i: (0, 0))],
        out_shape=[jax.ShapeDtypeStruct((T_LAT, D_MODEL), F32),
                   jax.ShapeDtypeStruct((T_LAT, META_LANES), F32),
                   jax.ShapeDtypeStruct((8, META_LANES), F32)],
        scratch_shapes=[pltpu.VMEM((1, META_LANES), F32)],
        compiler_params=_cparams(("arbitrary",)),
        name="moe_route",
    )(h, g, mod, mod, w_router, b_router)


def _row_copy(src, s, dst, d, sem):
    return pltpu.make_async_copy(src.at[pl.ds(s, 1)], dst.at[pl.ds(d, 1)], sem)


MOE_NJ = D_FF_EXPERT // TF
GATHER_ROWS_PER_STEP = TM // MOE_NJ


TAIL_ROWS = range(GATHER_ROWS_PER_STEP * MOE_NJ, TM)
TOK_ROWS = 2 * T_LAT + TM
QUARTER = TM // 4


def _moe_ffn_kernel(te_ref, nu_ref, nv_ref, src_ref, dst_ref, u_ref, wg_ref, wu_ref, wd_ref, tok_ref,
                    xg_scr, xb_scr, acc, sem):
    i, j = pl.program_id(0), pl.program_id(1)
    n_used = nu_ref[0]
    used = i < n_used
    last = n_used - 1
    slot = i % 2
    nxt = jnp.minimum(i + 1, last)
    prev = i - 1

    def get_row(tile, r):
        _row_copy(u_ref, src_ref[tile * TM + r], xg_scr, r, sem.at[0]).start()

    def put_row(tile, s, r):
        _row_copy(acc.at[s], r, tok_ref, dst_ref[(tile + 1) * TM + r], sem.at[1]).start()

    def wait_rows(get):
        def wait(r, c):
            if get:
                _row_copy(u_ref, 0, xg_scr, 0, sem.at[0]).wait()
            else:
                _row_copy(acc.at[0], 0, tok_ref, 0, sem.at[1]).wait()
            return c
        lax.fori_loop(0, TM, wait, 0, unroll=8)

    @pl.when((i == 0) & (j == 0))
    def _():
        lax.fori_loop(0, TM, lambda r, c: (get_row(0, r), c)[1], 0)
        acc[1] = jnp.zeros((TM, D_MODEL), F32)

    @pl.when(used & (j == 0))
    def _():
        wait_rows(True)
        xb_scr[...] = xg_scr[...].astype(BF16)
        acc[slot] = jnp.zeros((TM, D_MODEL), F32)
        for r in TAIL_ROWS:
            get_row(nxt, r)
            put_row(prev, 1 - slot, r)

    def step(rows):
        for r in range(GATHER_ROWS_PER_STEP):
            get_row(nxt, r * MOE_NJ + j)
            put_row(prev, 1 - slot, r * MOE_NJ + j)
        _swiglu_accumulate(xb_scr[:rows], _swiglu_weights(wg_ref, wu_ref, wd_ref), acc.at[slot])

    quarters = (nv_ref[i] + QUARTER - 1) // QUARTER
    for n in range(1, TM // QUARTER + 1):
        pl.when(used & (quarters == n))(functools.partial(step, n * QUARTER))

    @pl.when(used & (j == MOE_NJ - 1))
    def _():
        wait_rows(False)

    @pl.when((i == last) & (j == MOE_NJ - 1))
    def _():
        wait_rows(True)
        lax.fori_loop(0, TM, lambda r, c: (put_row(i, slot, r), c)[1], 0)
        wait_rows(False)


def _moe_ffn(u, src, dst, tile_expert, n_used, n_valid, w_gate, w_up, w_down):
    def jj(i, j, nu):
        return jnp.where(i < nu[0], j, MOE_NJ - 1)

    grid_spec = pltpu.PrefetchScalarGridSpec(
        num_scalar_prefetch=5,
        grid=(MOE_TILES, MOE_NJ),
        in_specs=[pl.BlockSpec(memory_space=pl.ANY),
                  pl.BlockSpec((None, D_MODEL, TF), lambda i, j, te, nu, *_: (te[i], 0, jj(i, j, nu))),
                  pl.BlockSpec((None, D_MODEL, TF), lambda i, j, te, nu, *_: (te[i], 0, jj(i, j, nu))),
                  pl.BlockSpec((None, TF, D_MODEL), lambda i, j, te, nu, *_: (te[i], jj(i, j, nu), 0))],
        out_specs=pl.BlockSpec(memory_space=pl.ANY),
        scratch_shapes=[pltpu.VMEM((TM, D_MODEL), F32), pltpu.VMEM((TM, D_MODEL), BF16),
                        pltpu.VMEM((2, TM, D_MODEL), F32), pltpu.SemaphoreType.DMA((2,))],
    )
    return pl.pallas_call(
        _moe_ffn_kernel,
        grid_spec=grid_spec,
        out_shape=jax.ShapeDtypeStruct((TOK_ROWS, D_MODEL), F32),
        compiler_params=_cparams(("arbitrary", "arbitrary")),
        name="moe_ffn",
    )(tile_expert, n_used, n_valid, src, dst, u, w_gate, w_up, w_down)


def _combine_kernel(h_ref, meta_ref, gt_ref, gf_ref, y1_ref, y2_ref, out_ref):
    meta = meta_ref[...]
    y = meta[:, 4:5] * y1_ref[...] + meta[:, 5:6] * y2_ref[...]
    hn = h_ref[...] + gt_ref[0] * y
    out_ref[...] = _rms(hn) * gf_ref[...]


def _moe_combine(h, meta, mod, g_final, tok):
    n = T_LAT // TC
    row = pl.BlockSpec((TC, D_MODEL), lambda i: (i, 0))
    return pl.pallas_call(
        _combine_kernel,
        grid=(n,),
        in_specs=[row,
                  pl.BlockSpec((TC, META_LANES), lambda i: (i, 0)),
                  pl.BlockSpec((1, 1, D_MODEL), lambda i: (i // (SEQ // TC), 0, 5)),
                  pl.BlockSpec((1, D_MODEL), lambda i: (0, 0)),
                  row,
                  pl.BlockSpec((TC, D_MODEL), lambda i: (n + i, 0))],
        out_specs=row,
        out_shape=jax.ShapeDtypeStruct((T_LAT, D_MODEL), F32),
        compiler_params=_cparams(("arbitrary",)),
        name="moe_combine",
    )(h, meta, mod, g_final, tok, tok)


def _moe_plan(meta, counts_f):
    i1 = meta[:, 0].astype(jnp.int32)
    i2 = meta[:, 1].astype(jnp.int32)
    r1 = meta[:, 2].astype(jnp.int32)
    r2 = meta[:, 3].astype(jnp.int32)
    counts = counts_f[0, :N_EXPERTS].astype(jnp.int32)
    padded = (counts + TM - 1) // TM * TM
    ends = jnp.cumsum(padded)
    starts = ends - padded
    experts = jnp.arange(N_EXPERTS, dtype=jnp.int32)

    def start_of(idx):
        return jnp.sum(jnp.where(idx[:, None] == experts, starts, 0), axis=1)

    pos = jnp.concatenate([start_of(i1) + r1, start_of(i2) + r2])
    n_used = ends[-1] // TM
    tile_start = jnp.arange(MOE_TILES, dtype=jnp.int32) * TM
    expert = jnp.minimum(jnp.sum(tile_start[:, None] >= ends[None, :], axis=1), N_EXPERTS - 1)
    n_valid = jnp.clip((starts + counts)[expert] - tile_start, 0, TM).astype(jnp.int32)
    expert = expert[jnp.minimum(jnp.arange(MOE_TILES), n_used - 1)].astype(jnp.int32)
    slot = jnp.arange(MOE_ROWS, dtype=jnp.int32)
    owner = jnp.full((MOE_ROWS,), -1, jnp.int32).at[pos].set(jnp.arange(2 * T_LAT, dtype=jnp.int32),
                                                             unique_indices=True, mode="promise_in_bounds")
    spare = 2 * T_LAT + slot % TM
    dst = jnp.concatenate([spare[:TM], jnp.where(owner >= 0, owner, spare)])
    src = jnp.where(owner >= 0, owner % T_LAT, 0)
    return src, dst, expert, n_used.reshape(1).astype(jnp.int32), n_valid


def _rope_tables():
    t = np.arange(SEQ)
    row, col = (t // GRID_W).astype(np.float64), (t % GRID_W).astype(np.float64)
    out = {}
    for dim, name in ((MLA_ROPE, "64"), (HEAD, "128")):
        half = dim // 2
        freqs = ROPE_THETA ** (-np.arange(0, half, 2, dtype=np.float64) / half)
        ar, ac = row[:, None] * freqs, col[:, None] * freqs
        c = np.concatenate([np.cos(ar), np.cos(ar), np.cos(ac), np.cos(ac)], axis=1)
        s = np.concatenate([-np.sin(ar), np.sin(ar), -np.sin(ac), np.sin(ac)], axis=1)
        order = _rope_lane_order(dim)
        c = np.tile(c, (1, HEAD // dim))[:, order]
        s = np.tile(s, (1, HEAD // dim))[:, order]
        out["c" + name] = jnp.asarray(np.concatenate([c, np.ones((TP, HEAD))], axis=0), F32)
        out["s" + name] = jnp.asarray(np.concatenate([s, np.zeros((TP, HEAD))], axis=0), F32)
    return out


def _na_bias_table(rel_bias):
    qc = np.arange(GRID_W)[:, None]
    kc = np.arange(GRID_W)[None, :]
    cs = np.clip(qc - NA_KW // 2, 0, GRID_W - NA_KW)
    col_ok = (kc >= cs) & (kc < cs + NA_KW)
    col_off = np.clip(kc - qc + NA_KW - 1, 0, 2 * NA_KW - 2)
    pick = jnp.asarray(col_off[None] == np.arange(2 * NA_KW - 1)[:, None, None], F32)
    tab = jnp.einsum("had,dqk->haqk", rel_bias * LOG2E, pick, precision=lax.Precision.HIGHEST)
    tab = jnp.where(jnp.asarray(col_ok), tab, NEG_INF)
    tab = jnp.concatenate([tab, jnp.full((NA_HEADS, 1, GRID_W, GRID_W), NEG_INF, F32)], axis=1)
    return jnp.concatenate([tab, tab], axis=-1)


def _layer_weights(w_in, mla_g_q, mla_w_qup, mla_g_kv, mla_w_kvup, gqa_g_q, gqa_g_k):
    cuts = (1536, 2048, 2112, 2880, 3136, 4160)
    na, mla_c, kpe, gqa_qk, gqa_v, diff_qk, diff_v = jnp.split(w_in.astype(BF16), cuts, axis=2)
    kpe = _to_rope_lane_order(jnp.pad(kpe, ((0, 0), (0, 0), (0, HEAD - MLA_ROPE))), MLA_ROPE)
    gqa_qk = _to_rope_lane_order(gqa_qk, HEAD)
    diff_qk = _to_rope_lane_order(diff_qk, DIFF_QK)
    pad = jnp.zeros((DEPTH, D_MODEL, IN_COLS_PAD - COL_KPE - HEAD), BF16)
    w_in_p = jnp.concatenate([na, mla_c, gqa_qk, gqa_v, diff_qk, diff_v, kpe, pad], axis=2)

    o64, o128 = _rope_lane_order(MLA_ROPE), _rope_lane_order(HEAD)
    qup = mla_w_qup.reshape(DEPTH, MLA_Q_RANK, MLA_HEADS, MLA_NOPE + MLA_ROPE)
    qup = jnp.pad(qup, ((0, 0), (0, 0), (0, 0), (0, 2 * HEAD - MLA_NOPE - MLA_ROPE)))
    qup = jnp.concatenate([qup[..., :HEAD], qup[..., HEAD:][..., o64]], axis=-1)
    kvup = mla_w_kvup.reshape(DEPTH, MLA_KV_RANK, MLA_HEADS, MLA_NOPE + MLA_V)
    return {
        "w_in": w_in_p,
        "g_cq": mla_g_q.reshape(DEPTH, 1, -1),
        "w_qup": qup.reshape(DEPTH, MLA_Q_RANK, MLA_HEADS * 2 * HEAD).astype(BF16),
        "g_ckv": mla_g_kv.reshape(DEPTH, 1, -1),
        "w_kvk": kvup[..., :MLA_NOPE].reshape(DEPTH, MLA_KV_RANK, -1).astype(BF16),
        "w_kvv": kvup[..., MLA_NOPE:].reshape(DEPTH, MLA_KV_RANK, -1).astype(BF16),
        "g_gq": gqa_g_q[:, o128].reshape(DEPTH, 1, -1),
        "g_gk": gqa_g_k[:, o128].reshape(DEPTH, 1, -1),
    }


def kernel(x, c, ctx, c_ctx, w_mod, b_mod, g_mix, w_in, na_rel_bias, mla_g_q, mla_w_qup, mla_g_kv, mla_w_kvup,
           gqa_g_q, gqa_g_k, diff_lq1, diff_lk1, diff_lq2, diff_lk2, diff_g_sub, w_out, g_ffn,
           ffn_w_gate, ffn_w_up, ffn_w_down, moe_w_router, moe_b_router, moe_w_gate, moe_w_up, moe_w_down,
           g_final):
    tabs = _rope_tables()
    lw = _layer_weights(w_in, mla_g_q, mla_w_qup, mla_g_kv, mla_w_kvup, gqa_g_q, gqa_g_k)
    cvec =jnp.concatenate([c, c_ctx[None, :], jnp.zeros((16 - BATCH - 1, D_MODEL), F32)], axis=0)
    mod_all = _modulation(cvec, w_mod, b_mod)
    stream = (x.reshape(T_LAT, D_MODEL), ctx.reshape(T_CTX, D_MODEL))

    out = None
    for l in range(DEPTH):
        last = l == DEPTH - 1
        with_ctx = not last
        lam_init = 0.8 - 0.6 * math.exp(-0.3 * l)
        mod = mod_all[l].reshape(16, 1, 6 * D_MODEL)

        proj = _in_proj(stream, g_mix[l].reshape(1, -1), mod, lw["w_in"], l)
        q_mla, k_mla, v_mla, q_gqa, k_gqa, q_diff, k_diff = _prep(proj, lw, l, tabs)

        o_na = _na_attention(proj, _na_bias_table(na_rel_bias[l]), with_ctx)
        o_mla = _global_attention(q_mla, 2 * HEAD, k_mla, 2 * HEAD, MLA_HEADS, 0, v_mla, 0, with_ctx)
        o_gqa = _global_attention(q_gqa, HEAD, k_gqa, HEAD, GQA_KV_HEADS, 0,
                                  proj, COL_GQA_V // (GQA_KV_HEADS * HEAD), with_ctx)
        lam_vecs = jnp.stack([diff_lq1[l], diff_lk1[l], diff_lq2[l], diff_lk2[l]])
        o_diff = _global_attention(q_diff, HEAD, k_diff, HEAD, DIFF_HEADS, 0,
                                   proj, COL_DIFF_V // (DIFF_HEADS * HEAD), with_ctx,
                                   diff=(lam_vecs, diff_g_sub[l].reshape(1, -1), lam_init))

        h = _out_proj((o_na, o_mla, o_gqa, o_diff), w_out[l].astype(BF16), stream, mod,
                      ROWS if with_ctx else T_LAT)

        if l % 2 == 0:
            h = _ffn_dense(h, g_ffn[l].reshape(1, -1), mod, ffn_w_gate[l // 2], ffn_w_up[l // 2],
                           ffn_w_down[l // 2])
            stream = (h,)
        else:
            m = l // 2
            w_r = jnp.pad(moe_w_router[m], ((0, 0), (0, META_LANES - N_EXPERTS)))
            b_r = jnp.pad(moe_b_router[m], (0, META_LANES - N_EXPERTS)).reshape(1, -1)
            u, meta, counts = _moe_route(h, g_ffn[l].reshape(1, -1), mod, w_r, b_r)
            src, dst, tile_expert, n_used, n_valid = _moe_plan(meta, counts)
            tok = _moe_ffn(u, src, dst, tile_expert, n_used, n_valid, moe_w_gate[m], moe_w_up[m], moe_w_down[m])
            out = _moe_combine(h, meta, mod, g_final.reshape(1, -1), tok)
    return out.reshape(BATCH, SEQ, D_MODEL)
```

```python
import functools
import math

import numpy as np
import jax
import jax.numpy as jnp
from jax import lax
from jax.experimental import pallas as pl
from jax.experimental.pallas import tpu as pltpu

F32 = jnp.float32
BF16 = jnp.bfloat16

D_MODEL = 2048
BATCH = 8
SEQ = 2048
DEPTH = 2
GRID_W = 64
GRID_H = SEQ // GRID_W
CTX_LEN = 256
ROPE_THETA = 10000.0
EPS = 1e-6
NEG_INF = -1e30

NA_HEADS = 4
NA_KH = 8
NA_KW = 16
MLA_HEADS = 4
MLA_NOPE = 128
MLA_ROPE = 64
MLA_V = 128
MLA_Q_RANK = 384
MLA_KV_RANK = 128
GQA_HEADS = 4
GQA_KV_HEADS = 2
DIFF_HEADS = 4
DIFF_QK = 64
HEAD = 128

LOG2E = math.log2(math.e)
NA_SCALE = HEAD ** -0.5 * LOG2E
MLA_SCALE = (MLA_NOPE + MLA_ROPE) ** -0.5 * LOG2E
GQA_SCALE = HEAD ** -0.5 * LOG2E
DIFF_SCALE = DIFF_QK ** -0.5 * LOG2E

D_FF = 5632
N_EXPERTS = 8
D_FF_EXPERT = 7168

T_LAT = BATCH * SEQ
T_CTX = BATCH * CTX_LEN
ROWS = T_LAT + T_CTX

COL_NA_Q, COL_NA_K, COL_NA_V = 0, 512, 1024
COL_MLA_C = 1536
COL_GQA_Q, COL_GQA_K, COL_GQA_V = 2048, 2560, 2816
COL_DIFF_Q, COL_DIFF_K, COL_DIFF_V = 3072, 3584, 4096
COL_KPE = 4608
IN_COLS_PAD = 5120

V7X_VMEM_LIMIT = 56 * 1024 * 1024

TM = 1024
N_LAT_TILES = T_LAT // TM
N_ALL_TILES = ROWS // TM
TILES_PER_BATCH = SEQ // TM
TQ = 256
NQ = SEQ // TQ
TQ_LATENT_ONLY = 512
TP = 512
TN_IN = 1024
V7X_MXU_WIDTH = 256
IN_LAST_COLS = -(-(COL_KPE + HEAD - (IN_COLS_PAD - TN_IN)) // V7X_MXU_WIDTH) * V7X_MXU_WIDTH
TM_O = 512
TF = 256
NORM_ROWS = 256
MOE_TILES = 2 * T_LAT // TM + N_EXPERTS
MOE_ROWS = MOE_TILES * TM
TC = 512


def _cparams(sem, vmem=V7X_VMEM_LIMIT):
    return pltpu.CompilerParams(dimension_semantics=sem, vmem_limit_bytes=vmem)


def _mod_row(i, tm):
    return jnp.where(i < T_LAT // tm, i // (SEQ // tm), BATCH)


def _dot(a, b):
    return jnp.dot(a, b, preferred_element_type=F32)


def _dot_t(a, b):
    return lax.dot_general(a, b, (((1,), (1,)), ((), ())), preferred_element_type=F32)


def _rms(x):
    return x * lax.rsqrt(jnp.mean(x * x, axis=-1, keepdims=True) + EPS)


def _silu(x):
    return x * jax.nn.sigmoid(x)


def _swiglu_weights(wg_ref, wu_ref, wd_ref):
    return wg_ref[...].astype(BF16), wu_ref[...].astype(BF16), wd_ref[...].astype(BF16)


def _swiglu_accumulate(x, weights, out_ref, assign=False):
    wg, wu, wd = weights
    rows = x.shape[0]
    a = (_silu(_dot(x, wg)) * _dot(x, wu)).astype(BF16)
    for c in range(0, D_MODEL, 512):
        part = _dot(a, wd[:, c:c + 512])
        if assign:
            out_ref[:rows, c:c + 512] = part
        else:
            out_ref[:rows, c:c + 512] += part


def _norm_modulate(x, g_ref, sh_ref, sc_ref):
    y = _rms(x) * g_ref[...]
    return (y * (1.0 + sc_ref[0]) + sh_ref[0]).astype(BF16)


def _mod_kernel(c_ref, w_ref, b_ref, o_ref):
    s = _silu(c_ref[...]).astype(BF16)
    o_ref[0] = _dot(s, w_ref[0].astype(BF16)) + b_ref[0]


def _modulation(cvec, w_mod, b_mod):
    tn = 1024
    return pl.pallas_call(
        _mod_kernel,
        grid=(DEPTH, 6 * D_MODEL // tn),
        in_specs=[pl.BlockSpec((16, D_MODEL), lambda l, j: (0, 0)),
                  pl.BlockSpec((1, D_MODEL, tn), lambda l, j: (l, 0, j)),
                  pl.BlockSpec((1, 1, tn), lambda l, j: (l, 0, j))],
        out_specs=pl.BlockSpec((1, 16, tn), lambda l, j: (l, 0, j)),
        out_shape=jax.ShapeDtypeStruct((DEPTH, 16, 6 * D_MODEL), F32),
        compiler_params=_cparams(("arbitrary", "arbitrary")),
        name="modulation",
    )(cvec, w_mod, b_mod.reshape(DEPTH, 1, 6 * D_MODEL))


def _stream_specs(stream, tm):
    if len(stream) == 1:
        return [pl.BlockSpec((tm, D_MODEL), lambda i, *_: (i, 0))]
    n_lat = T_LAT // tm
    return [pl.BlockSpec((tm, D_MODEL), lambda i, *_: (jnp.minimum(i, n_lat - 1), 0)),
            pl.BlockSpec((tm, D_MODEL), lambda i, *_: (jnp.maximum(i - n_lat, 0), 0))]


def _with_stream_rows(h_refs, tm, fn):
    if len(h_refs) == 1:
        fn(h_refs[0])
        return
    i = pl.program_id(0)
    n_lat = T_LAT // tm
    pl.when(i < n_lat)(lambda: fn(h_refs[0]))
    pl.when(i >= n_lat)(lambda: fn(h_refs[1]))


def _in_proj_kernel(*refs):
    *h_refs, g_ref, sh_ref, sc_ref, w_ref, o_ref, u_scr = refs

    j = pl.program_id(1)

    def first(h_ref):
        w = w_ref[...]
        for r in range(0, TM, NORM_ROWS):
            u = _norm_modulate(h_ref[r:r + NORM_ROWS], g_ref, sh_ref, sc_ref)
            u_scr[r:r + NORM_ROWS] = u
            o_ref[r:r + NORM_ROWS] = _dot(u, w).astype(o_ref.dtype)

    @pl.when(j == 0)
    def _():
        _with_stream_rows(h_refs, TM, first)

    last = pl.num_programs(1) - 1

    @pl.when((j > 0) & (j < last))
    def _():
        o_ref[...] = _dot(u_scr[...], w_ref[...]).astype(o_ref.dtype)

    @pl.when(j == last)
    def _():
        o_ref[:, :IN_LAST_COLS] = _dot(u_scr[...], w_ref[:, :IN_LAST_COLS]).astype(o_ref.dtype)
        o_ref[:, IN_LAST_COLS:] = jnp.zeros((TM, TN_IN - IN_LAST_COLS), o_ref.dtype)


def _in_proj(stream, g, mod, w_in, l):
    return pl.pallas_call(
        _in_proj_kernel,
        grid=(N_ALL_TILES, IN_COLS_PAD // TN_IN),
        in_specs=_stream_specs(stream, TM) + [
                  pl.BlockSpec((1, D_MODEL), lambda i, j: (0, 0)),
                  pl.BlockSpec((1, 1, D_MODEL), lambda i, j: (_mod_row(i, TM), 0, 0)),
                  pl.BlockSpec((1, 1, D_MODEL), lambda i, j: (_mod_row(i, TM), 0, 1)),
                  pl.BlockSpec((None, D_MODEL, TN_IN), lambda i, j: (l, 0, j))],
        out_specs=pl.BlockSpec((TM, TN_IN), lambda i, j: (i, j)),
        out_shape=jax.ShapeDtypeStruct((ROWS, IN_COLS_PAD), BF16),
        scratch_shapes=[pltpu.VMEM((TM, D_MODEL), BF16)],
        compiler_params=_cparams(("arbitrary", "arbitrary")),
        name="in_proj",
    )(*stream, g, mod, mod, w_in)


def _rope_lane_order(dim):
    q, n_vec = dim // 4, HEAD // dim
    order = np.empty(HEAD, np.int64)
    for b in range(2):
        for v in range(n_vec):
            for a in range(2):
                for i in range(q):
                    order[b * (HEAD // 2) + v * 2 * q + a * q + i] = v * dim + a * 2 * q + b * q + i
    return order


def _to_rope_lane_order(w, dim):
    q, n_vec = dim // 4, HEAD // dim
    blocks = w.reshape(w.shape[:-1] + (-1, n_vec, 2, 2, q))
    return jnp.moveaxis(blocks, -2, -4).reshape(w.shape)


def _rope(x, c, s):
    return x * c + pltpu.roll(x, HEAD // 2, 1) * s


def _prep_kernel(mc_ref, gq_ref, gk_ref, dq_ref, dk_ref, kpe_ref,
                 g_cq_ref, w_qup_ref, g_ckv_ref, w_kvk_ref, w_kvv_ref, g_gq_ref, g_gk_ref,
                 c64_ref, s64_ref, c128_ref, s128_ref,
                 qm_ref, km_ref, vm_ref, qg_ref, kg_ref, qd_ref, kd_ref):
    c64, s64 = c64_ref[...], s64_ref[...]
    c128, s128 = c128_ref[...], s128_ref[...]

    mc = mc_ref[...].astype(F32)
    cq = (_rms(mc[:, :MLA_Q_RANK]) * g_cq_ref[...]).astype(BF16)
    ckv = (_rms(mc[:, MLA_Q_RANK:]) * g_ckv_ref[...]).astype(BF16)
    q = _dot(cq, w_qup_ref[...])
    kn = _dot(ckv, w_kvk_ref[...])
    vm_ref[...] = _dot(ckv, w_kvv_ref[...]).astype(BF16)
    kpe = _rope(kpe_ref[...].astype(F32), c64, s64).astype(BF16)
    for h in range(MLA_HEADS):
        lo = 2 * HEAD * h
        qm_ref[:, lo:lo + HEAD] = (q[:, lo:lo + HEAD] * MLA_SCALE).astype(BF16)
        pe = _rope(q[:, lo + HEAD:lo + 2 * HEAD], c64, s64)
        qm_ref[:, lo + HEAD:lo + 2 * HEAD] = (pe * MLA_SCALE).astype(BF16)
        km_ref[:, lo:lo + HEAD] = kn[:, h * HEAD:(h + 1) * HEAD].astype(BF16)
        km_ref[:, lo + HEAD:lo + 2 * HEAD] = kpe

    for h in range(GQA_HEADS):
        sl = slice(h * HEAD, (h + 1) * HEAD)
        z = _rms(gq_ref[:, sl].astype(F32)) * g_gq_ref[...]
        qg_ref[:, sl] = (_rope(z, c128, s128) * GQA_SCALE).astype(BF16)
    for h in range(GQA_KV_HEADS):
        sl = slice(h * HEAD, (h + 1) * HEAD)
        z = _rms(gk_ref[:, sl].astype(F32)) * g_gk_ref[...]
        kg_ref[:, sl] = _rope(z, c128, s128).astype(BF16)

    for h in range(DIFF_HEADS):
        sl = slice(h * HEAD, (h + 1) * HEAD)
        qd_ref[:, sl] = (_rope(dq_ref[:, sl].astype(F32), c64, s64) * DIFF_SCALE).astype(BF16)
        kd_ref[:, sl] = _rope(dk_ref[:, sl].astype(F32), c64, s64).astype(BF16)


def _prep(proj, lw, l, tabs):
    n_tiles = ROWS // TP
    lat_tiles = T_LAT // TP
    per_batch = SEQ // TP

    def col(width, start):
        return pl.BlockSpec((TP, width), lambda i: (i, start // width))

    def full(shape):
        return pl.BlockSpec((None,) + shape, lambda i: (l,) + (0,) * len(shape))

    def tab():
        return pl.BlockSpec((TP, HEAD), lambda i: (jnp.where(i < lat_tiles, i % per_batch, per_batch), 0))

    def out(width):
        return pl.BlockSpec((TP, width), lambda i: (i, 0))

    widths = (1024, 1024, 512, 512, 256, 512, 512)
    return pl.pallas_call(
        _prep_kernel,
        grid=(n_tiles,),
        in_specs=[col(512, COL_MLA_C), col(512, COL_GQA_Q), col(256, COL_GQA_K),
                  col(512, COL_DIFF_Q), col(512, COL_DIFF_K), col(128, COL_KPE),
                  full((1, MLA_Q_RANK)), full((MLA_Q_RANK, 1024)), full((1, MLA_KV_RANK)),
                  full((MLA_KV_RANK, 512)), full((MLA_KV_RANK, 512)), full((1, HEAD)), full((1, HEAD)),
                  tab(), tab(), tab(), tab()],
        out_specs=[out(w) for w in widths],
        out_shape=[jax.ShapeDtypeStruct((ROWS, w), BF16) for w in widths],
        compiler_params=_cparams(("arbitrary",)),
        name="prep",
    )(proj, proj, proj, proj, proj, proj,
      lw["g_cq"], lw["w_qup"], lw["g_ckv"], lw["w_kvk"], lw["w_kvv"], lw["g_gq"], lw["g_gk"],
      tabs["c64"], tabs["s64"], tabs["c128"], tabs["s128"])


N_HEADS = 4
N_KEYS = CTX_LEN + SEQ


def _attend(s_c, vc, s_l=None, vl=None):
    m = jnp.max(s_c, axis=-1, keepdims=True)
    if s_l is not None:
        m = jnp.maximum(m, jnp.max(s_l, axis=-1, keepdims=True))
    p_c = jnp.exp2(s_c - m)
    den = jnp.sum(p_c, axis=-1, keepdims=True)
    o = _dot(p_c.astype(BF16), vc)
    if s_l is not None:
        p_l = jnp.exp2(s_l - m)
        den = den + jnp.sum(p_l, axis=-1, keepdims=True)
        o = o + _dot(p_l.astype(BF16), vl)
    return o / den


def _attend_ones(q, k, v_ones):
    s = _dot_t(q, k)
    p = jnp.exp2(s - jnp.max(s, axis=-1, keepdims=True)).astype(BF16)
    o = _dot(p, v_ones)
    return o[:, :HEAD] / o[:, HEAD:HEAD + 1]


def _on_query_tiles(with_ctx, lat_fn, ctx_fn):
    if not with_ctx:
        lat_fn()
        return
    qt = pl.program_id(1)
    pl.when(qt < NQ)(lat_fn)
    pl.when(qt == NQ)(ctx_fn)


def _global_kernel(q_ref, kc_ref, vc_ref, kl_ref, vl_ref, *rest, with_ctx, n_kv, q_w, k_w, lam_init):
    if lam_init is None:
        o_ref, kcat, vcat = rest
    else:
        lam_ref, g_ref, o_ref, kcat, vcat = rest

    @pl.when(pl.program_id(1) == 0)
    def _():
        lane = lax.broadcasted_iota(jnp.int32, (N_KEYS, HEAD), 1)
        ones = jnp.where(lane == 0, 1.0, 0.0).astype(BF16)
        for kv in range(n_kv):
            kcat[kv, :CTX_LEN] = kc_ref[:, kv * k_w:(kv + 1) * k_w]
            kcat[kv, CTX_LEN:] = kl_ref[:, kv * k_w:(kv + 1) * k_w]
            vcat[kv, :CTX_LEN, :HEAD] = vc_ref[:, kv * HEAD:(kv + 1) * HEAD]
            vcat[kv, CTX_LEN:, :HEAD] = vl_ref[:, kv * HEAD:(kv + 1) * HEAD]
            vcat[kv, :, HEAD:] = ones

    if lam_init is not None:
        lv = lam_ref[...]
        lam = (jnp.exp(jnp.sum(lv[0:1] * lv[1:2], axis=-1, keepdims=True))
               - jnp.exp(jnp.sum(lv[2:3] * lv[3:4], axis=-1, keepdims=True)) + lam_init)

    def run(n_keys):
        for h in range(N_HEADS):
            kv = h // (N_HEADS // n_kv)
            k, v = kcat[kv, :n_keys], vcat[kv, :n_keys]
            q = q_ref[:, h * q_w:(h + 1) * q_w]
            if lam_init is None:
                o = _attend_ones(q, k, v)
            else:
                first = (lax.broadcasted_iota(jnp.int32, q.shape, 1) & (DIFF_QK // 2)) == 0
                zero = jnp.zeros_like(q)
                o1 = _attend_ones(jnp.where(first, q, zero), k, v)
                o2 = _attend_ones(jnp.where(first, zero, q), k, v)
                o = _rms(o1 - lam * o2) * g_ref[...] * (1.0 - lam_init)
            o_ref[:, h * HEAD:(h + 1) * HEAD] = o.astype(o_ref.dtype)

    _on_query_tiles(with_ctx, lambda: run(N_KEYS), lambda: run(CTX_LEN))


def _q_row_block(b, qt):
    return jnp.where(qt < NQ, b * NQ + qt, T_LAT // TQ + b)


def _global_attention(q, q_w, k, k_w, n_kv, k_col, v, v_col, with_ctx, diff=None):
    out_rows = ROWS if with_ctx else T_LAT
    ctx_blk = T_LAT // CTX_LEN
    if with_ctx:
        tq, n_qt, q_blk = TQ, NQ + 1, _q_row_block
    else:
        tq, n_qt, q_blk = TQ_LATENT_ONLY, SEQ // TQ_LATENT_ONLY, lambda b, t: b * (SEQ // TQ_LATENT_ONLY) + t

    in_specs = [
        pl.BlockSpec((tq, N_HEADS * q_w), lambda b, t: (q_blk(b, t), 0)),
        pl.BlockSpec((CTX_LEN, n_kv * k_w), lambda b, t: (ctx_blk + b, k_col)),
        pl.BlockSpec((CTX_LEN, n_kv * HEAD), lambda b, t: (ctx_blk + b, v_col)),
        pl.BlockSpec((SEQ, n_kv * k_w), lambda b, t: (b, k_col)),
        pl.BlockSpec((SEQ, n_kv * HEAD), lambda b, t: (b, v_col)),
    ]
    args = [q, k, v, k, v]
    lam_init = None
    if diff is not None:
        lam_vecs, g_sub, lam_init = diff
        in_specs += [pl.BlockSpec((4, DIFF_QK), lambda b, t: (0, 0)),
                     pl.BlockSpec((1, HEAD), lambda b, t: (0, 0))]
        args += [lam_vecs, g_sub]
    return pl.pallas_call(
        functools.partial(_global_kernel, with_ctx=with_ctx, n_kv=n_kv, q_w=q_w, k_w=k_w, lam_init=lam_init),
        grid=(BATCH, n_qt),
        in_specs=in_specs,
        out_specs=pl.BlockSpec((tq, N_HEADS * HEAD), lambda b, t: (q_blk(b, t), 0)),
        out_shape=jax.ShapeDtypeStruct((out_rows, N_HEADS * HEAD), BF16),
        scratch_shapes=[pltpu.VMEM((n_kv, N_KEYS, k_w), BF16), pltpu.VMEM((n_kv, N_KEYS, 2 * HEAD), BF16)],
        compiler_params=_cparams(("arbitrary",) * 2),
        name="diff_attention" if diff is not None else "global_attention",
    )(*args)


NA_QROWS = TQ // GRID_W
NA_WROWS = 12
NA_WIN = NA_WROWS * GRID_W
NA_MASKED = 2 * NA_KH - 1


def _na_kernel(q_ref, kc_ref, vc_ref, k_ref, v_ref, bias_ref, o_ref, bias_scr, *, with_ctx):
    def head(h):
        return slice(h * HEAD, (h + 1) * HEAD)

    def lat():
        r0 = pl.program_id(1) * NA_QROWS
        ws = jnp.clip(r0 - NA_KH // 2, 0, GRID_H - NA_WROWS)
        start = pl.multiple_of(ws * GRID_W, TQ)
        block = {}
        for i in range(NA_QROWS):
            qr = r0 + i
            rs = jnp.clip(qr - NA_KH // 2, 0, GRID_H - NA_KH)
            for j in range(NA_WROWS):
                kr = ws + j
                block[i, j] = jnp.where((kr >= rs) & (kr < rs + NA_KH), kr - qr + NA_KH - 1, NA_MASKED)
        for h in range(NA_HEADS):
            for (i, j), a in block.items():
                half = (j % 2) * GRID_W
                bias_scr[h, i * GRID_W:(i + 1) * GRID_W, j * GRID_W:(j + 1) * GRID_W] = (
                    bias_ref[h, a][:, half:half + GRID_W])
            q = q_ref[:, head(h)]
            kw = k_ref[pl.ds(start, NA_WIN), head(h)]
            vw = v_ref[pl.ds(start, NA_WIN), head(h)]
            s_l = _dot_t(q, kw) * NA_SCALE + bias_scr[h]
            s_c = _dot_t(q, kc_ref[:, head(h)]) * NA_SCALE
            o_ref[:, head(h)] = _attend(s_c, vc_ref[:, head(h)], s_l, vw).astype(o_ref.dtype)

    def ctx():
        for h in range(NA_HEADS):
            s_c = _dot_t(q_ref[:, head(h)], kc_ref[:, head(h)]) * NA_SCALE
            o_ref[:, head(h)] = _attend(s_c, vc_ref[:, head(h)]).astype(o_ref.dtype)

    _on_query_tiles(with_ctx, lat, ctx)


def _na_attention(proj, bias_tab, with_ctx):
    n_qt = NQ + 1 if with_ctx else NQ
    out_rows = ROWS if with_ctx else T_LAT
    ctx_blk = T_LAT // CTX_LEN
    width = NA_HEADS * HEAD
    kcol, vcol = COL_NA_K // width, COL_NA_V // width
    return pl.pallas_call(
        functools.partial(_na_kernel, with_ctx=with_ctx),
        grid=(BATCH, n_qt),
        in_specs=[pl.BlockSpec((TQ, width), lambda b, t: (_q_row_block(b, t), 0)),
                  pl.BlockSpec((CTX_LEN, width), lambda b, t: (ctx_blk + b, kcol)),
                  pl.BlockSpec((CTX_LEN, width), lambda b, t: (ctx_blk + b, vcol)),
                  pl.BlockSpec((SEQ, width), lambda b, t: (b, kcol)),
                  pl.BlockSpec((SEQ, width), lambda b, t: (b, vcol)),
                  pl.BlockSpec((NA_HEADS, 2 * NA_KH, GRID_W, HEAD), lambda b, t: (0, 0, 0, 0))],
        out_specs=pl.BlockSpec((TQ, width), lambda b, t: (_q_row_block(b, t), 0)),
        out_shape=jax.ShapeDtypeStruct((out_rows, width), BF16),
        scratch_shapes=[pltpu.VMEM((NA_HEADS, TQ, NA_WIN), F32)],
        compiler_params=_cparams(("arbitrary",) * 2),
        name="na_attention",
    )(proj, proj, proj, proj, proj, bias_tab)


def _out_proj_kernel(o0_ref, o1_ref, o2_ref, o3_ref, w_ref, gt_ref, *refs):
    *h_refs, out_ref = refs
    for c in range(0, D_MODEL, 512):
        acc = None
        for t, o_ref in enumerate((o0_ref, o1_ref, o2_ref, o3_ref)):
            part = _dot(o_ref[...], w_ref[t * 512:(t + 1) * 512, c:c + 512])
            acc = part if acc is None else acc + part
        gated = gt_ref[0, :, c:c + 512] * acc
        out_ref[:, c:c + 512] = h_refs[0][:, c:c + 512] + gated if len(h_refs) == 1 else gated

    def residual(h_ref):
        out_ref[...] += h_ref[...]

    if len(h_refs) > 1:
        _with_stream_rows(h_refs, TM_O, residual)


def _out_proj(o_parts, w_out, stream, mod, n_rows):
    o_spec = pl.BlockSpec((TM_O, 512), lambda i: (i, 0))
    return pl.pallas_call(
        _out_proj_kernel,
        grid=(n_rows // TM_O,),
        in_specs=[o_spec, o_spec, o_spec, o_spec,
                  pl.BlockSpec((D_MODEL, D_MODEL), lambda i: (0, 0), pipeline_mode=pl.Buffered(1)),
                  pl.BlockSpec((1, 1, D_MODEL), lambda i: (_mod_row(i, TM_O), 0, 2))]
                 + _stream_specs(stream, TM_O),
        out_specs=pl.BlockSpec((TM_O, D_MODEL), lambda i: (i, 0)),
        out_shape=jax.ShapeDtypeStruct((n_rows, D_MODEL), F32),
        compiler_params=_cparams(("arbitrary",)),
        name="out_proj",
    )(*o_parts, w_out, mod, *stream)


def _ffn_kernel(h_ref, g_ref, sh_ref, sc_ref, gt_ref, wg_ref, wu_ref, wd_ref, out_ref, u_scr):
    j = pl.program_id(1)

    @pl.when(j == 0)
    def _():
        weights = _swiglu_weights(wg_ref, wu_ref, wd_ref)
        for r in range(0, TM, NORM_ROWS):
            u = _norm_modulate(h_ref[r:r + NORM_ROWS], g_ref, sh_ref, sc_ref)
            u_scr[r:r + NORM_ROWS] = u
            _swiglu_accumulate(u, weights, out_ref.at[pl.ds(r, NORM_ROWS)], assign=True)

    @pl.when(j > 0)
    def _():
        _swiglu_accumulate(u_scr[...], _swiglu_weights(wg_ref, wu_ref, wd_ref), out_ref)

    @pl.when(j == pl.num_programs(1) - 1)
    def _():
        out_ref[...] = h_ref[...] + gt_ref[0] * out_ref[...]


def _ffn_dense(h, g, mod, w_gate, w_up, w_down):
    def mod_spec(chunk):
        return pl.BlockSpec((1, 1, D_MODEL), lambda i, j: (_mod_row(i, TM), 0, chunk))

    return pl.pallas_call(
        _ffn_kernel,
        grid=(N_ALL_TILES, D_FF // TF),
        in_specs=[pl.BlockSpec((TM, D_MODEL), lambda i, j: (i, 0)),
                  pl.BlockSpec((1, D_MODEL), lambda i, j: (0, 0)),
                  mod_spec(3), mod_spec(4), mod_spec(5),
                  pl.BlockSpec((D_MODEL, TF), lambda i, j: (0, j)),
                  pl.BlockSpec((D_MODEL, TF), lambda i, j: (0, j)),
                  pl.BlockSpec((TF, D_MODEL), lambda i, j: (j, 0))],
        out_specs=pl.BlockSpec((TM, D_MODEL), lambda i, j: (i, 0)),
        out_shape=jax.ShapeDtypeStruct((ROWS, D_MODEL), F32),
        scratch_shapes=[pltpu.VMEM((TM, D_MODEL), BF16)],
        compiler_params=_cparams(("arbitrary", "arbitrary")),
        name="ffn_dense",
    )(h, g, mod, mod, mod, w_gate, w_up, w_down)


META_LANES = 128


def _route_kernel(h_ref, g_ref, sh_ref, sc_ref, wr_ref, br_ref, u_ref, meta_ref, cnt_ref, carry):
    i = pl.program_id(0)

    @pl.when(i == 0)
    def _():
        carry[...] = jnp.zeros_like(carry)

    y = _rms(h_ref[...]) * g_ref[...]
    u = y * (1.0 + sc_ref[0]) + sh_ref[0]
    u_ref[...] = u

    w = wr_ref[...]
    u_hi = u.astype(BF16)
    u_lo = (u - u_hi.astype(F32)).astype(BF16)
    w_hi = w.astype(BF16)
    w_lo = (w - w_hi.astype(F32)).astype(BF16)
    logits = _dot(u_hi, w_hi) + (_dot(u_hi, w_lo) + _dot(u_lo, w_hi)) + br_ref[...]

    lane = lax.broadcasted_iota(jnp.int32, logits.shape, 1)
    logits = jnp.where(lane < N_EXPERTS, logits, NEG_INF)
    lane_f = lane.astype(F32)
    m1 = jnp.max(logits, axis=-1, keepdims=True)
    i1 = jnp.min(jnp.where(logits == m1, lane_f, float(META_LANES)), axis=-1, keepdims=True)
    pick1 = lane_f == i1
    rest = jnp.where(pick1, NEG_INF, logits)
    m2 = jnp.max(rest, axis=-1, keepdims=True)
    i2 = jnp.min(jnp.where(rest == m2, lane_f, float(META_LANES)), axis=-1, keepdims=True)
    pick2 = lane_f == i2
    e = jnp.exp(m2 - m1)
    g1 = 1.0 / (1.0 + e)
    g2 = e / (1.0 + e)

    onehot = jnp.where(pick1 | pick2, 1.0, 0.0)
    rr = lax.broadcasted_iota(jnp.int32, (TM, TM), 0)
    cc = lax.broadcasted_iota(jnp.int32, (TM, TM), 1)
    earlier = jnp.where(cc < rr, 1.0, 0.0).astype(BF16)
    before = _dot(earlier, onehot.astype(BF16)) + carry[...]
    r1 = jnp.sum(jnp.where(pick1, before, 0.0), axis=-1, keepdims=True)
    r2 = jnp.sum(jnp.where(pick2, before, 0.0), axis=-1, keepdims=True)
    carry[...] += jnp.sum(onehot, axis=0, keepdims=True)

    meta = jnp.zeros(logits.shape, F32)
    for k, val in enumerate((i1, i2, r1, r2, g1, g2)):
        meta = jnp.where(lane == k, val, meta)
    meta_ref[...] = meta
    cnt_ref[...] = jnp.broadcast_to(carry[...], cnt_ref.shape)


def _moe_route(h, g, mod, w_router, b_router):
    def mod_spec(chunk):
        return pl.BlockSpec((1, 1, D_MODEL), lambda i: (_mod_row(i, TM), 0, chunk))

    return pl.pallas_call(
        _route_kernel,
        grid=(N_LAT_TILES,),
        in_specs=[pl.BlockSpec((TM, D_MODEL), lambda i: (i, 0)),
                  pl.BlockSpec((1, D_MODEL), lambda i: (0, 0)),
                  mod_spec(3), mod_spec(4),
                  pl.BlockSpec((D_MODEL, META_LANES), lambda i: (0, 0)),
                  pl.BlockSpec((1, META_LANES), lambda i: (0, 0))],
        out_specs=[pl.BlockSpec((TM, D_MODEL), lambda i: (i, 0)),
                   pl.BlockSpec((TM, META_LANES), lambda i: (i, 0)),
                   pl.BlockSpec((8, META_LANES), lambda i: (0, 0))],
        out_shape=[jax.ShapeDtypeStruct((T_LAT, D_MODEL), F32),
                   jax.ShapeDtypeStruct((T_LAT, META_LANES), F32),
                   jax.ShapeDtypeStruct((8, META_LANES), F32)],
        scratch_shapes=[pltpu.VMEM((1, META_LANES), F32)],
        compiler_params=_cparams(("arbitrary",)),
        name="moe_route",
    )(h, g, mod, mod, w_router, b_router)


def _row_copy(src, s, dst, d, sem):
    return pltpu.make_async_copy(src.at[pl.ds(s, 1)], dst.at[pl.ds(d, 1)], sem)


MOE_NJ = D_FF_EXPERT // TF
GATHER_ROWS_PER_STEP = TM // MOE_NJ


TAIL_ROWS = range(GATHER_ROWS_PER_STEP * MOE_NJ, TM)
TOK_ROWS = 2 * T_LAT + TM
HALF = TM // 2


def _moe_ffn_kernel(te_ref, nu_ref, nv_ref, src_ref, dst_ref, u_ref, wg_ref, wu_ref, wd_ref, tok_ref,
                    xg_scr, xb_scr, acc, sem):
    i, j = pl.program_id(0), pl.program_id(1)
    n_used = nu_ref[0]
    used = i < n_used
    last = n_used - 1
    slot = i % 2
    nxt = jnp.minimum(i + 1, last)
    prev = i - 1

    def get_row(tile, r):
        _row_copy(u_ref, src_ref[tile * TM + r], xg_scr, r, sem.at[0]).start()

    def put_row(tile, s, r):
        _row_copy(acc.at[s], r, tok_ref, dst_ref[(tile + 1) * TM + r], sem.at[1]).start()

    def wait_rows(get):
        def wait(r, c):
            if get:
                _row_copy(u_ref, 0, xg_scr, 0, sem.at[0]).wait()
            else:
                _row_copy(acc.at[0], 0, tok_ref, 0, sem.at[1]).wait()
            return c
        lax.fori_loop(0, TM, wait, 0, unroll=8)

    @pl.when((i == 0) & (j == 0))
    def _():
        lax.fori_loop(0, TM, lambda r, c: (get_row(0, r), c)[1], 0)
        acc[1] = jnp.zeros((TM, D_MODEL), F32)

    @pl.when(used & (j == 0))
    def _():
        wait_rows(True)
        xb_scr[...] = xg_scr[...].astype(BF16)
        acc[slot] = jnp.zeros((TM, D_MODEL), F32)
        for r in TAIL_ROWS:
            get_row(nxt, r)
            put_row(prev, 1 - slot, r)

    def step(rows):
        for r in range(GATHER_ROWS_PER_STEP):
            get_row(nxt, r * MOE_NJ + j)
            put_row(prev, 1 - slot, r * MOE_NJ + j)
        _swiglu_accumulate(xb_scr[:rows], _swiglu_weights(wg_ref, wu_ref, wd_ref), acc.at[slot])

    pl.when(used & (nv_ref[i] > HALF))(lambda: step(TM))
    pl.when(used & (nv_ref[i] <= HALF))(lambda: step(HALF))

    @pl.when(used & (j == MOE_NJ - 1))
    def _():
        wait_rows(False)

    @pl.when((i == last) & (j == MOE_NJ - 1))
    def _():
        wait_rows(True)
        lax.fori_loop(0, TM, lambda r, c: (put_row(i, slot, r), c)[1], 0)
        wait_rows(False)


def _moe_ffn(u, src, dst, tile_expert, n_used, n_valid, w_gate, w_up, w_down):
    def jj(i, j, nu):
        return jnp.where(i < nu[0], j, MOE_NJ - 1)

    grid_spec = pltpu.PrefetchScalarGridSpec(
        num_scalar_prefetch=5,
        grid=(MOE_TILES, MOE_NJ),
        in_specs=[pl.BlockSpec(memory_space=pl.ANY),
                  pl.BlockSpec((None, D_MODEL, TF), lambda i, j, te, nu, *_: (te[i], 0, jj(i, j, nu))),
                  pl.BlockSpec((None, D_MODEL, TF), lambda i, j, te, nu, *_: (te[i], 0, jj(i, j, nu))),
                  pl.BlockSpec((None, TF, D_MODEL), lambda i, j, te, nu, *_: (te[i], jj(i, j, nu), 0))],
        out_specs=pl.BlockSpec(memory_space=pl.ANY),
        scratch_shapes=[pltpu.VMEM((TM, D_MODEL), F32), pltpu.VMEM((TM, D_MODEL), BF16),
                        pltpu.VMEM((2, TM, D_MODEL), F32), pltpu.SemaphoreType.DMA((2,))],
    )
    return pl.pallas_call(
        _moe_ffn_kernel,
        grid_spec=grid_spec,
        out_shape=jax.ShapeDtypeStruct((TOK_ROWS, D_MODEL), F32),
        compiler_params=_cparams(("arbitrary", "arbitrary")),
        name="moe_ffn",
    )(tile_expert, n_used, n_valid, src, dst, u, w_gate, w_up, w_down)


def _combine_kernel(h_ref, meta_ref, gt_ref, gf_ref, y1_ref, y2_ref, out_ref):
    meta = meta_ref[...]
    y = meta[:, 4:5] * y1_ref[...] + meta[:, 5:6] * y2_ref[...]
    hn = h_ref[...] + gt_ref[0] * y
    out_ref[...] = _rms(hn) * gf_ref[...]


def _moe_combine(h, meta, mod, g_final, tok):
    n = T_LAT // TC
    row = pl.BlockSpec((TC, D_MODEL), lambda i: (i, 0))
    return pl.pallas_call(
        _combine_kernel,
        grid=(n,),
        in_specs=[row,
                  pl.BlockSpec((TC, META_LANES), lambda i: (i, 0)),
                  pl.BlockSpec((1, 1, D_MODEL), lambda i: (i // (SEQ // TC), 0, 5)),
                  pl.BlockSpec((1, D_MODEL), lambda i: (0, 0)),
                  row,
                  pl.BlockSpec((TC, D_MODEL), lambda i: (n + i, 0))],
        out_specs=row,
        out_shape=jax.ShapeDtypeStruct((T_LAT, D_MODEL), F32),
        compiler_params=_cparams(("arbitrary",)),
        name="moe_combine",
    )(h, meta, mod, g_final, tok, tok)


def _moe_plan(meta, counts_f):
    i1 = meta[:, 0].astype(jnp.int32)
    i2 = meta[:, 1].astype(jnp.int32)
    r1 = meta[:, 2].astype(jnp.int32)
    r2 = meta[:, 3].astype(jnp.int32)
    counts = counts_f[0, :N_EXPERTS].astype(jnp.int32)
    padded = (counts + TM - 1) // TM * TM
    ends = jnp.cumsum(padded)
    starts = ends - padded
    experts = jnp.arange(N_EXPERTS, dtype=jnp.int32)

    def start_of(idx):
        return jnp.sum(jnp.where(idx[:, None] == experts, starts, 0), axis=1)

    pos = jnp.concatenate([start_of(i1) + r1, start_of(i2) + r2])
    n_used = ends[-1] // TM
    tile_start = jnp.arange(MOE_TILES, dtype=jnp.int32) * TM
    expert = jnp.minimum(jnp.sum(tile_start[:, None] >= ends[None, :], axis=1), N_EXPERTS - 1)
    n_valid = jnp.clip((starts + counts)[expert] - tile_start, 0, TM).astype(jnp.int32)
    expert = expert[jnp.minimum(jnp.arange(MOE_TILES), n_used - 1)].astype(jnp.int32)
    slot = jnp.arange(MOE_ROWS, dtype=jnp.int32)
    owner = jnp.full((MOE_ROWS,), -1, jnp.int32).at[pos].set(jnp.arange(2 * T_LAT, dtype=jnp.int32),
                                                             unique_indices=True, mode="promise_in_bounds")
    spare = 2 * T_LAT + slot % TM
    dst = jnp.concatenate([spare[:TM], jnp.where(owner >= 0, owner, spare)])
    src = jnp.where(owner >= 0, owner % T_LAT, 0)
    return src, dst, expert, n_used.reshape(1).astype(jnp.int32), n_valid


def _rope_tables():
    t = np.arange(SEQ)
    row, col = (t // GRID_W).astype(np.float64), (t % GRID_W).astype(np.float64)
    out = {}
    for dim, name in ((MLA_ROPE, "64"), (HEAD, "128")):
        half = dim // 2
        freqs = ROPE_THETA ** (-np.arange(0, half, 2, dtype=np.float64) / half)
        ar, ac = row[:, None] * freqs, col[:, None] * freqs
        c = np.concatenate([np.cos(ar), np.cos(ar), np.cos(ac), np.cos(ac)], axis=1)
        s = np.concatenate([-np.sin(ar), np.sin(ar), -np.sin(ac), np.sin(ac)], axis=1)
        order = _rope_lane_order(dim)
        c = np.tile(c, (1, HEAD // dim))[:, order]
        s = np.tile(s, (1, HEAD // dim))[:, order]
        out["c" + name] = jnp.asarray(np.concatenate([c, np.ones((TP, HEAD))], axis=0), F32)
        out["s" + name] = jnp.asarray(np.concatenate([s, np.zeros((TP, HEAD))], axis=0), F32)
    return out


def _na_bias_table(rel_bias):
    qc = np.arange(GRID_W)[:, None]
    kc = np.arange(GRID_W)[None, :]
    cs = np.clip(qc - NA_KW // 2, 0, GRID_W - NA_KW)
    col_ok = (kc >= cs) & (kc < cs + NA_KW)
    col_off = np.clip(kc - qc + NA_KW - 1, 0, 2 * NA_KW - 2)
    pick = jnp.asarray(col_off[None] == np.arange(2 * NA_KW - 1)[:, None, None], F32)
    tab = jnp.einsum("had,dqk->haqk", rel_bias * LOG2E, pick, precision=lax.Precision.HIGHEST)
    tab = jnp.where(jnp.asarray(col_ok), tab, NEG_INF)
    tab = jnp.concatenate([tab, jnp.full((NA_HEADS, 1, GRID_W, GRID_W), NEG_INF, F32)], axis=1)
    return jnp.concatenate([tab, tab], axis=-1)


def _layer_weights(w_in, mla_g_q, mla_w_qup, mla_g_kv, mla_w_kvup, gqa_g_q, gqa_g_k):
    cuts = (1536, 2048, 2112, 2880, 3136, 4160)
    na, mla_c, kpe, gqa_qk, gqa_v, diff_qk, diff_v = jnp.split(w_in, cuts, axis=2)
    kpe = _to_rope_lane_order(jnp.pad(kpe, ((0, 0), (0, 0), (0, HEAD - MLA_ROPE))), MLA_ROPE)
    gqa_qk = _to_rope_lane_order(gqa_qk, HEAD)
    diff_qk = _to_rope_lane_order(diff_qk, DIFF_QK)
    pad = jnp.zeros((DEPTH, D_MODEL, IN_COLS_PAD - COL_KPE - HEAD), F32)
    w_in_p = jnp.concatenate([na, mla_c, gqa_qk, gqa_v, diff_qk, diff_v, kpe, pad], axis=2).astype(BF16)

    o64, o128 = _rope_lane_order(MLA_ROPE), _rope_lane_order(HEAD)
    qup = mla_w_qup.reshape(DEPTH, MLA_Q_RANK, MLA_HEADS, MLA_NOPE + MLA_ROPE)
    qup = jnp.pad(qup, ((0, 0), (0, 0), (0, 0), (0, 2 * HEAD - MLA_NOPE - MLA_ROPE)))
    qup = jnp.concatenate([qup[..., :HEAD], qup[..., HEAD:][..., o64]], axis=-1)
    kvup = mla_w_kvup.reshape(DEPTH, MLA_KV_RANK, MLA_HEADS, MLA_NOPE + MLA_V)
    return {
        "w_in": w_in_p,
        "g_cq": mla_g_q.reshape(DEPTH, 1, -1),
        "w_qup": qup.reshape(DEPTH, MLA_Q_RANK, MLA_HEADS * 2 * HEAD).astype(BF16),
        "g_ckv": mla_g_kv.reshape(DEPTH, 1, -1),
        "w_kvk": kvup[..., :MLA_NOPE].reshape(DEPTH, MLA_KV_RANK, -1).astype(BF16),
        "w_kvv": kvup[..., MLA_NOPE:].reshape(DEPTH, MLA_KV_RANK, -1).astype(BF16),
        "g_gq": gqa_g_q[:, o128].reshape(DEPTH, 1, -1),
        "g_gk": gqa_g_k[:, o128].reshape(DEPTH, 1, -1),
    }


def kernel(x, c, ctx, c_ctx, w_mod, b_mod, g_mix, w_in, na_rel_bias, mla_g_q, mla_w_qup, mla_g_kv, mla_w_kvup,
           gqa_g_q, gqa_g_k, diff_lq1, diff_lk1, diff_lq2, diff_lk2, diff_g_sub, w_out, g_ffn,
           ffn_w_gate, ffn_w_up, ffn_w_down, moe_w_router, moe_b_router, moe_w_gate, moe_w_up, moe_w_down,
           g_final):
    tabs = _rope_tables()
    lw = _layer_weights(w_in, mla_g_q, mla_w_qup, mla_g_kv, mla_w_kvup, gqa_g_q, gqa_g_k)
    cvec =jnp.concatenate([c, c_ctx[None, :], jnp.zeros((16 - BATCH - 1, D_MODEL), F32)], axis=0)
    mod_all = _modulation(cvec, w_mod, b_mod)
    stream = (x.reshape(T_LAT, D_MODEL), ctx.reshape(T_CTX, D_MODEL))

    out = None
    for l in range(DEPTH):
        last = l == DEPTH - 1
        with_ctx = not last
        lam_init = 0.8 - 0.6 * math.exp(-0.3 * l)
        mod = mod_all[l].reshape(16, 1, 6 * D_MODEL)

        proj = _in_proj(stream, g_mix[l].reshape(1, -1), mod, lw["w_in"], l)
        q_mla, k_mla, v_mla, q_gqa, k_gqa, q_diff, k_diff = _prep(proj, lw, l, tabs)

        o_na = _na_attention(proj, _na_bias_table(na_rel_bias[l]), with_ctx)
        o_mla = _global_attention(q_mla, 2 * HEAD, k_mla, 2 * HEAD, MLA_HEADS, 0, v_mla, 0, with_ctx)
        o_gqa = _global_attention(q_gqa, HEAD, k_gqa, HEAD, GQA_KV_HEADS, 0,
                                  proj, COL_GQA_V // (GQA_KV_HEADS * HEAD), with_ctx)
        lam_vecs = jnp.stack([diff_lq1[l], diff_lk1[l], diff_lq2[l], diff_lk2[l]])
        o_diff = _global_attention(q_diff, HEAD, k_diff, HEAD, DIFF_HEADS, 0,
                                   proj, COL_DIFF_V // (DIFF_HEADS * HEAD), with_ctx,
                                   diff=(lam_vecs, diff_g_sub[l].reshape(1, -1), lam_init))

        h = _out_proj((o_na, o_mla, o_gqa, o_diff), w_out[l].astype(BF16), stream, mod,
                      ROWS if with_ctx else T_LAT)

        if l % 2 == 0:
            h = _ffn_dense(h, g_ffn[l].reshape(1, -1), mod, ffn_w_gate[l // 2], ffn_w_up[l // 2],
                           ffn_w_down[l // 2])
            stream = (h,)
        else:
            m = l // 2
            w_r = jnp.pad(moe_w_router[m], ((0, 0), (0, META_LANES - N_EXPERTS)))
            b_r = jnp.pad(moe_b_router[m], (0, META_LANES - N_EXPERTS)).reshape(1, -1)
            u, meta, counts = _moe_route(h, g_ffn[l].reshape(1, -1), mod, w_r, b_r)
            src, dst, tile_expert, n_used, n_valid = _moe_plan(meta, counts)
            tok = _moe_ffn(u, src, dst, tile_expert, n_used, n_valid, moe_w_gate[m], moe_w_up[m], moe_w_down[m])
            out = _moe_combine(h, meta, mod, g_final.reshape(1, -1), tok)
    return out.reshape(BATCH, SEQ, D_MODEL)
```

```python
import functools
import math

import numpy as np
import jax
import jax.numpy as jnp
from jax import lax
from jax.experimental import pallas as pl
from jax.experimental.pallas import tpu as pltpu

F32 = jnp.float32
BF16 = jnp.bfloat16

D_MODEL = 2048
BATCH = 8
SEQ = 2048
DEPTH = 2
GRID_W = 64
GRID_H = SEQ // GRID_W
CTX_LEN = 256
ROPE_THETA = 10000.0
EPS = 1e-6
NEG_INF = -1e30

NA_HEADS = 4
NA_KH = 8
NA_KW = 16
MLA_HEADS = 4
MLA_NOPE = 128
MLA_ROPE = 64
MLA_V = 128
MLA_Q_RANK = 384
MLA_KV_RANK = 128
GQA_HEADS = 4
GQA_KV_HEADS = 2
DIFF_HEADS = 4
DIFF_QK = 64
HEAD = 128

LOG2E = math.log2(math.e)
NA_SCALE = HEAD ** -0.5 * LOG2E
MLA_SCALE = (MLA_NOPE + MLA_ROPE) ** -0.5 * LOG2E
GQA_SCALE = HEAD ** -0.5 * LOG2E
DIFF_SCALE = DIFF_QK ** -0.5 * LOG2E

D_FF = 5632
N_EXPERTS = 8
D_FF_EXPERT = 7168

T_LAT = BATCH * SEQ
T_CTX = BATCH * CTX_LEN
ROWS = T_LAT + T_CTX

COL_NA_Q, COL_NA_K, COL_NA_V = 0, 512, 1024
COL_MLA_C = 1536
COL_GQA_Q, COL_GQA_K, COL_GQA_V = 2048, 2560, 2816
COL_DIFF_Q, COL_DIFF_K, COL_DIFF_V = 3072, 3584, 4096
COL_KPE = 4608
IN_COLS_PAD = 5120

V7X_VMEM_LIMIT = 56 * 1024 * 1024

TM = 1024
N_LAT_TILES = T_LAT // TM
N_ALL_TILES = ROWS // TM
TILES_PER_BATCH = SEQ // TM
TQ = 256
NQ = SEQ // TQ
TQ_LATENT_ONLY = 512
TP = 512
TN_IN = 1024
V7X_MXU_WIDTH = 256
IN_LAST_COLS = -(-(COL_KPE + HEAD - (IN_COLS_PAD - TN_IN)) // V7X_MXU_WIDTH) * V7X_MXU_WIDTH
TM_O = 512
TF = 256
NORM_ROWS = 256
MOE_TILES = 2 * T_LAT // TM + N_EXPERTS
MOE_ROWS = MOE_TILES * TM
TC = 512


def _cparams(sem, vmem=V7X_VMEM_LIMIT):
    return pltpu.CompilerParams(dimension_semantics=sem, vmem_limit_bytes=vmem)


def _mod_row(i, tm):
    return jnp.where(i < T_LAT // tm, i // (SEQ // tm), BATCH)


def _dot(a, b):
    return jnp.dot(a, b, preferred_element_type=F32)


def _dot_t(a, b):
    return lax.dot_general(a, b, (((1,), (1,)), ((), ())), preferred_element_type=F32)


def _rms(x):
    return x * lax.rsqrt(jnp.mean(x * x, axis=-1, keepdims=True) + EPS)


def _silu(x):
    return x * jax.nn.sigmoid(x)


def _swiglu_weights(wg_ref, wu_ref, wd_ref):
    return wg_ref[...].astype(BF16), wu_ref[...].astype(BF16), wd_ref[...].astype(BF16)


def _swiglu_accumulate(x, weights, out_ref, assign=False):
    wg, wu, wd = weights
    rows = x.shape[0]
    a = (_silu(_dot(x, wg)) * _dot(x, wu)).astype(BF16)
    for c in range(0, D_MODEL, 512):
        part = _dot(a, wd[:, c:c + 512])
        if assign:
            out_ref[:rows, c:c + 512] = part
        else:
            out_ref[:rows, c:c + 512] += part


def _norm_modulate(x, g_ref, sh_ref, sc_ref):
    y = _rms(x) * g_ref[...]
    return (y * (1.0 + sc_ref[0]) + sh_ref[0]).astype(BF16)


def _mod_kernel(c_ref, w_ref, b_ref, o_ref):
    s = _silu(c_ref[...]).astype(BF16)
    o_ref[0] = _dot(s, w_ref[0].astype(BF16)) + b_ref[0]


def _modulation(cvec, w_mod, b_mod):
    tn = 1024
    return pl.pallas_call(
        _mod_kernel,
        grid=(DEPTH, 6 * D_MODEL // tn),
        in_specs=[pl.BlockSpec((16, D_MODEL), lambda l, j: (0, 0)),
                  pl.BlockSpec((1, D_MODEL, tn), lambda l, j: (l, 0, j)),
                  pl.BlockSpec((1, 1, tn), lambda l, j: (l, 0, j))],
        out_specs=pl.BlockSpec((1, 16, tn), lambda l, j: (l, 0, j)),
        out_shape=jax.ShapeDtypeStruct((DEPTH, 16, 6 * D_MODEL), F32),
        compiler_params=_cparams(("arbitrary", "arbitrary")),
        name="modulation",
    )(cvec, w_mod, b_mod.reshape(DEPTH, 1, 6 * D_MODEL))


def _stream_specs(stream, tm):
    if len(stream) == 1:
        return [pl.BlockSpec((tm, D_MODEL), lambda i, *_: (i, 0))]
    n_lat = T_LAT // tm
    return [pl.BlockSpec((tm, D_MODEL), lambda i, *_: (jnp.minimum(i, n_lat - 1), 0)),
            pl.BlockSpec((tm, D_MODEL), lambda i, *_: (jnp.maximum(i - n_lat, 0), 0))]


def _with_stream_rows(h_refs, tm, fn):
    if len(h_refs) == 1:
        fn(h_refs[0])
        return
    i = pl.program_id(0)
    n_lat = T_LAT // tm
    pl.when(i < n_lat)(lambda: fn(h_refs[0]))
    pl.when(i >= n_lat)(lambda: fn(h_refs[1]))


def _in_proj_kernel(*refs):
    *h_refs, g_ref, sh_ref, sc_ref, w_ref, o_ref, u_scr = refs

    j = pl.program_id(1)

    def first(h_ref):
        w = w_ref[...]
        for r in range(0, TM, NORM_ROWS):
            u = _norm_modulate(h_ref[r:r + NORM_ROWS], g_ref, sh_ref, sc_ref)
            u_scr[r:r + NORM_ROWS] = u
            o_ref[r:r + NORM_ROWS] = _dot(u, w).astype(o_ref.dtype)

    @pl.when(j == 0)
    def _():
        _with_stream_rows(h_refs, TM, first)

    last = pl.num_programs(1) - 1

    @pl.when((j > 0) & (j < last))
    def _():
        o_ref[...] = _dot(u_scr[...], w_ref[...]).astype(o_ref.dtype)

    @pl.when(j == last)
    def _():
        o_ref[:, :IN_LAST_COLS] = _dot(u_scr[...], w_ref[:, :IN_LAST_COLS]).astype(o_ref.dtype)
        o_ref[:, IN_LAST_COLS:] = jnp.zeros((TM, TN_IN - IN_LAST_COLS), o_ref.dtype)


def _in_proj(stream, g, mod, w_in, l):
    return pl.pallas_call(
        _in_proj_kernel,
        grid=(N_ALL_TILES, IN_COLS_PAD // TN_IN),
        in_specs=_stream_specs(stream, TM) + [
                  pl.BlockSpec((1, D_MODEL), lambda i, j: (0, 0)),
                  pl.BlockSpec((1, 1, D_MODEL), lambda i, j: (_mod_row(i, TM), 0, 0)),
                  pl.BlockSpec((1, 1, D_MODEL), lambda i, j: (_mod_row(i, TM), 0, 1)),
                  pl.BlockSpec((None, D_MODEL, TN_IN), lambda i, j: (l, 0, j))],
        out_specs=pl.BlockSpec((TM, TN_IN), lambda i, j: (i, j)),
        out_shape=jax.ShapeDtypeStruct((ROWS, IN_COLS_PAD), BF16),
        scratch_shapes=[pltpu.VMEM((TM, D_MODEL), BF16)],
        compiler_params=_cparams(("arbitrary", "arbitrary")),
        name="in_proj",
    )(*stream, g, mod, mod, w_in)


def _rope_lane_order(dim):
    q, n_vec = dim // 4, HEAD // dim
    order = np.empty(HEAD, np.int64)
    for b in range(2):
        for v in range(n_vec):
            for a in range(2):
                for i in range(q):
                    order[b * (HEAD // 2) + v * 2 * q + a * q + i] = v * dim + a * 2 * q + b * q + i
    return order


def _to_rope_lane_order(w, dim):
    q, n_vec = dim // 4, HEAD // dim
    blocks = w.reshape(w.shape[:-1] + (-1, n_vec, 2, 2, q))
    return jnp.moveaxis(blocks, -2, -4).reshape(w.shape)


def _rope(x, c, s):
    return x * c + pltpu.roll(x, HEAD // 2, 1) * s


def _prep_kernel(mc_ref, gq_ref, gk_ref, dq_ref, dk_ref, kpe_ref,
                 g_cq_ref, w_qup_ref, g_ckv_ref, w_kvk_ref, w_kvv_ref, g_gq_ref, g_gk_ref,
                 c64_ref, s64_ref, c128_ref, s128_ref,
                 qm_ref, km_ref, vm_ref, qg_ref, kg_ref, qd_ref, kd_ref):
    c64, s64 = c64_ref[...], s64_ref[...]
    c128, s128 = c128_ref[...], s128_ref[...]

    mc = mc_ref[...].astype(F32)
    cq = (_rms(mc[:, :MLA_Q_RANK]) * g_cq_ref[...]).astype(BF16)
    ckv = (_rms(mc[:, MLA_Q_RANK:]) * g_ckv_ref[...]).astype(BF16)
    q = _dot(cq, w_qup_ref[...])
    kn = _dot(ckv, w_kvk_ref[...])
    vm_ref[...] = _dot(ckv, w_kvv_ref[...]).astype(BF16)
    kpe = _rope(kpe_ref[...].astype(F32), c64, s64).astype(BF16)
    for h in range(MLA_HEADS):
        lo = 2 * HEAD * h
        qm_ref[:, lo:lo + HEAD] = (q[:, lo:lo + HEAD] * MLA_SCALE).astype(BF16)
        pe = _rope(q[:, lo + HEAD:lo + 2 * HEAD], c64, s64)
        qm_ref[:, lo + HEAD:lo + 2 * HEAD] = (pe * MLA_SCALE).astype(BF16)
        km_ref[:, lo:lo + HEAD] = kn[:, h * HEAD:(h + 1) * HEAD].astype(BF16)
        km_ref[:, lo + HEAD:lo + 2 * HEAD] = kpe

    for h in range(GQA_HEADS):
        sl = slice(h * HEAD, (h + 1) * HEAD)
        z = _rms(gq_ref[:, sl].astype(F32)) * g_gq_ref[...]
        qg_ref[:, sl] = (_rope(z, c128, s128) * GQA_SCALE).astype(BF16)
    for h in range(GQA_KV_HEADS):
        sl = slice(h * HEAD, (h + 1) * HEAD)
        z = _rms(gk_ref[:, sl].astype(F32)) * g_gk_ref[...]
        kg_ref[:, sl] = _rope(z, c128, s128).astype(BF16)

    for h in range(DIFF_HEADS):
        sl = slice(h * HEAD, (h + 1) * HEAD)
        qd_ref[:, sl] = (_rope(dq_ref[:, sl].astype(F32), c64, s64) * DIFF_SCALE).astype(BF16)
        kd_ref[:, sl] = _rope(dk_ref[:, sl].astype(F32), c64, s64).astype(BF16)


def _prep(proj, lw, l, tabs):
    n_tiles = ROWS // TP
    lat_tiles = T_LAT // TP
    per_batch = SEQ // TP

    def col(width, start):
        return pl.BlockSpec((TP, width), lambda i: (i, start // width))

    def full(shape):
        return pl.BlockSpec((None,) + shape, lambda i: (l,) + (0,) * len(shape))

    def tab():
        return pl.BlockSpec((TP, HEAD), lambda i: (jnp.where(i < lat_tiles, i % per_batch, per_batch), 0))

    def out(width):
        return pl.BlockSpec((TP, width), lambda i: (i, 0))

    widths = (1024, 1024, 512, 512, 256, 512, 512)
    return pl.pallas_call(
        _prep_kernel,
        grid=(n_tiles,),
        in_specs=[col(512, COL_MLA_C), col(512, COL_GQA_Q), col(256, COL_GQA_K),
                  col(512, COL_DIFF_Q), col(512, COL_DIFF_K), col(128, COL_KPE),
                  full((1, MLA_Q_RANK)), full((MLA_Q_RANK, 1024)), full((1, MLA_KV_RANK)),
                  full((MLA_KV_RANK, 512)), full((MLA_KV_RANK, 512)), full((1, HEAD)), full((1, HEAD)),
                  tab(), tab(), tab(), tab()],
        out_specs=[out(w) for w in widths],
        out_shape=[jax.ShapeDtypeStruct((ROWS, w), BF16) for w in widths],
        compiler_params=_cparams(("arbitrary",)),
        name="prep",
    )(proj, proj, proj, proj, proj, proj,
      lw["g_cq"], lw["w_qup"], lw["g_ckv"], lw["w_kvk"], lw["w_kvv"], lw["g_gq"], lw["g_gk"],
      tabs["c64"], tabs["s64"], tabs["c128"], tabs["s128"])


N_HEADS = 4
N_KEYS = CTX_LEN + SEQ


def _attend(s_c, vc, s_l=None, vl=None):
    m = jnp.max(s_c, axis=-1, keepdims=True)
    if s_l is not None:
        m = jnp.maximum(m, jnp.max(s_l, axis=-1, keepdims=True))
    p_c = jnp.exp2(s_c - m)
    den = jnp.sum(p_c, axis=-1, keepdims=True)
    o = _dot(p_c.astype(BF16), vc)
    if s_l is not None:
        p_l = jnp.exp2(s_l - m)
        den = den + jnp.sum(p_l, axis=-1, keepdims=True)
        o = o + _dot(p_l.astype(BF16), vl)
    return o / den


def _attend_ones(q, k, v_ones):
    s = _dot_t(q, k)
    p = jnp.exp2(s - jnp.max(s, axis=-1, keepdims=True)).astype(BF16)
    o = _dot(p, v_ones)
    return o[:, :HEAD] / o[:, HEAD:HEAD + 1]


def _on_query_tiles(with_ctx, lat_fn, ctx_fn):
    if not with_ctx:
        lat_fn()
        return
    qt = pl.program_id(1)
    pl.when(qt < NQ)(lat_fn)
    pl.when(qt == NQ)(ctx_fn)


def _global_kernel(q_ref, kc_ref, vc_ref, kl_ref, vl_ref, *rest, with_ctx, n_kv, q_w, k_w, lam_init):
    if lam_init is None:
        o_ref, kcat, vcat = rest
    else:
        lam_ref, g_ref, o_ref, kcat, vcat = rest

    @pl.when(pl.program_id(1) == 0)
    def _():
        lane = lax.broadcasted_iota(jnp.int32, (N_KEYS, HEAD), 1)
        ones = jnp.where(lane == 0, 1.0, 0.0).astype(BF16)
        for kv in range(n_kv):
            kcat[kv, :CTX_LEN] = kc_ref[:, kv * k_w:(kv + 1) * k_w]
            kcat[kv, CTX_LEN:] = kl_ref[:, kv * k_w:(kv + 1) * k_w]
            vcat[kv, :CTX_LEN, :HEAD] = vc_ref[:, kv * HEAD:(kv + 1) * HEAD]
            vcat[kv, CTX_LEN:, :HEAD] = vl_ref[:, kv * HEAD:(kv + 1) * HEAD]
            vcat[kv, :, HEAD:] = ones

    if lam_init is not None:
        lv = lam_ref[...]
        lam = (jnp.exp(jnp.sum(lv[0:1] * lv[1:2], axis=-1, keepdims=True))
               - jnp.exp(jnp.sum(lv[2:3] * lv[3:4], axis=-1, keepdims=True)) + lam_init)

    def run(n_keys):
        for h in range(N_HEADS):
            kv = h // (N_HEADS // n_kv)
            k, v = kcat[kv, :n_keys], vcat[kv, :n_keys]
            q = q_ref[:, h * q_w:(h + 1) * q_w]
            if lam_init is None:
                o = _attend_ones(q, k, v)
            else:
                first = (lax.broadcasted_iota(jnp.int32, q.shape, 1) & (DIFF_QK // 2)) == 0
                zero = jnp.zeros_like(q)
                o1 = _attend_ones(jnp.where(first, q, zero), k, v)
                o2 = _attend_ones(jnp.where(first, zero, q), k, v)
                o = _rms(o1 - lam * o2) * g_ref[...] * (1.0 - lam_init)
            o_ref[:, h * HEAD:(h + 1) * HEAD] = o.astype(o_ref.dtype)

    _on_query_tiles(with_ctx, lambda: run(N_KEYS), lambda: run(CTX_LEN))


def _q_row_block(b, qt):
    return jnp.where(qt < NQ, b * NQ + qt, T_LAT // TQ + b)


def _global_attention(q, q_w, k, k_w, n_kv, k_col, v, v_col, with_ctx, diff=None):
    out_rows = ROWS if with_ctx else T_LAT
    ctx_blk = T_LAT // CTX_LEN
    if with_ctx:
        tq, n_qt, q_blk = TQ, NQ + 1, _q_row_block
    else:
        tq, n_qt, q_blk = TQ_LATENT_ONLY, SEQ // TQ_LATENT_ONLY, lambda b, t: b * (SEQ // TQ_LATENT_ONLY) + t

    in_specs = [
        pl.BlockSpec((tq, N_HEADS * q_w), lambda b, t: (q_blk(b, t), 0)),
        pl.BlockSpec((CTX_LEN, n_kv * k_w), lambda b, t: (ctx_blk + b, k_col)),
        pl.BlockSpec((CTX_LEN, n_kv * HEAD), lambda b, t: (ctx_blk + b, v_col)),
        pl.BlockSpec((SEQ, n_kv * k_w), lambda b, t: (b, k_col)),
        pl.BlockSpec((SEQ, n_kv * HEAD), lambda b, t: (b, v_col)),
    ]
    args = [q, k, v, k, v]
    lam_init = None
    if diff is not None:
        lam_vecs, g_sub, lam_init = diff
        in_specs += [pl.BlockSpec((4, DIFF_QK), lambda b, t: (0, 0)),
                     pl.BlockSpec((1, HEAD), lambda b, t: (0, 0))]
        args += [lam_vecs, g_sub]
    return pl.pallas_call(
        functools.partial(_global_kernel, with_ctx=with_ctx, n_kv=n_kv, q_w=q_w, k_w=k_w, lam_init=lam_init),
        grid=(BATCH, n_qt),
        in_specs=in_specs,
        out_specs=pl.BlockSpec((tq, N_HEADS * HEAD), lambda b, t: (q_blk(b, t), 0)),
        out_shape=jax.ShapeDtypeStruct((out_rows, N_HEADS * HEAD), BF16),
        scratch_shapes=[pltpu.VMEM((n_kv, N_KEYS, k_w), BF16), pltpu.VMEM((n_kv, N_KEYS, 2 * HEAD), BF16)],
        compiler_params=_cparams(("arbitrary",) * 2),
        name="diff_attention" if diff is not None else "global_attention",
    )(*args)


NA_QROWS = TQ // GRID_W
NA_WROWS = 12
NA_WIN = NA_WROWS * GRID_W
NA_MASKED = 2 * NA_KH - 1


def _na_kernel(q_ref, kc_ref, vc_ref, k_ref, v_ref, bias_ref, o_ref, bias_scr, *, with_ctx):
    def head(h):
        return slice(h * HEAD, (h + 1) * HEAD)

    def lat():
        r0 = pl.program_id(1) * NA_QROWS
        ws = jnp.clip(r0 - NA_KH // 2, 0, GRID_H - NA_WROWS)
        start = pl.multiple_of(ws * GRID_W, TQ)
        block = {}
        for i in range(NA_QROWS):
            qr = r0 + i
            rs = jnp.clip(qr - NA_KH // 2, 0, GRID_H - NA_KH)
            for j in range(NA_WROWS):
                kr = ws + j
                block[i, j] = jnp.where((kr >= rs) & (kr < rs + NA_KH), kr - qr + NA_KH - 1, NA_MASKED)
        for h in range(NA_HEADS):
            for (i, j), a in block.items():
                half = (j % 2) * GRID_W
                bias_scr[h, i * GRID_W:(i + 1) * GRID_W, j * GRID_W:(j + 1) * GRID_W] = (
                    bias_ref[h, a][:, half:half + GRID_W])
            q = q_ref[:, head(h)]
            kw = k_ref[pl.ds(start, NA_WIN), head(h)]
            vw = v_ref[pl.ds(start, NA_WIN), head(h)]
            s_l = _dot_t(q, kw) * NA_SCALE + bias_scr[h]
            s_c = _dot_t(q, kc_ref[:, head(h)]) * NA_SCALE
            o_ref[:, head(h)] = _attend(s_c, vc_ref[:, head(h)], s_l, vw).astype(o_ref.dtype)

    def ctx():
        for h in range(NA_HEADS):
            s_c = _dot_t(q_ref[:, head(h)], kc_ref[:, head(h)]) * NA_SCALE
            o_ref[:, head(h)] = _attend(s_c, vc_ref[:, head(h)]).astype(o_ref.dtype)

    _on_query_tiles(with_ctx, lat, ctx)


def _na_attention(proj, bias_tab, with_ctx):
    n_qt = NQ + 1 if with_ctx else NQ
    out_rows = ROWS if with_ctx else T_LAT
    ctx_blk = T_LAT // CTX_LEN
    width = NA_HEADS * HEAD
    kcol, vcol = COL_NA_K // width, COL_NA_V // width
    return pl.pallas_call(
        functools.partial(_na_kernel, with_ctx=with_ctx),
        grid=(BATCH, n_qt),
        in_specs=[pl.BlockSpec((TQ, width), lambda b, t: (_q_row_block(b, t), 0)),
                  pl.BlockSpec((CTX_LEN, width), lambda b, t: (ctx_blk + b, kcol)),
                  pl.BlockSpec((CTX_LEN, width), lambda b, t: (ctx_blk + b, vcol)),
                  pl.BlockSpec((SEQ, width), lambda b, t: (b, kcol)),
                  pl.BlockSpec((SEQ, width), lambda b, t: (b, vcol)),
                  pl.BlockSpec((NA_HEADS, 2 * NA_KH, GRID_W, HEAD), lambda b, t: (0, 0, 0, 0))],
        out_specs=pl.BlockSpec((TQ, width), lambda b, t: (_q_row_block(b, t), 0)),
        out_shape=jax.ShapeDtypeStruct((out_rows, width), BF16),
        scratch_shapes=[pltpu.VMEM((NA_HEADS, TQ, NA_WIN), F32)],
        compiler_params=_cparams(("arbitrary",) * 2),
        name="na_attention",
    )(proj, proj, proj, proj, proj, bias_tab)


def _out_proj_kernel(o0_ref, o1_ref, o2_ref, o3_ref, w_ref, gt_ref, *refs):
    *h_refs, out_ref = refs
    for c in range(0, D_MODEL, 512):
        acc = None
        for t, o_ref in enumerate((o0_ref, o1_ref, o2_ref, o3_ref)):
            part = _dot(o_ref[...], w_ref[t * 512:(t + 1) * 512, c:c + 512])
            acc = part if acc is None else acc + part
        gated = gt_ref[0, :, c:c + 512] * acc
        out_ref[:, c:c + 512] = h_refs[0][:, c:c + 512] + gated if len(h_refs) == 1 else gated

    def residual(h_ref):
        out_ref[...] += h_ref[...]

    if len(h_refs) > 1:
        _with_stream_rows(h_refs, TM_O, residual)


def _out_proj(o_parts, w_out, stream, mod, n_rows):
    o_spec = pl.BlockSpec((TM_O, 512), lambda i: (i, 0))
    return pl.pallas_call(
        _out_proj_kernel,
        grid=(n_rows // TM_O,),
        in_specs=[o_spec, o_spec, o_spec, o_spec,
                  pl.BlockSpec((D_MODEL, D_MODEL), lambda i: (0, 0), pipeline_mode=pl.Buffered(1)),
                  pl.BlockSpec((1, 1, D_MODEL), lambda i: (_mod_row(i, TM_O), 0, 2))]
                 + _stream_specs(stream, TM_O),
        out_specs=pl.BlockSpec((TM_O, D_MODEL), lambda i: (i, 0)),
        out_shape=jax.ShapeDtypeStruct((n_rows, D_MODEL), F32),
        compiler_params=_cparams(("arbitrary",)),
        name="out_proj",
    )(*o_parts, w_out, mod, *stream)


def _ffn_kernel(h_ref, g_ref, sh_ref, sc_ref, gt_ref, wg_ref, wu_ref, wd_ref, out_ref, u_scr):
    j = pl.program_id(1)

    @pl.when(j == 0)
    def _():
        weights = _swiglu_weights(wg_ref, wu_ref, wd_ref)
        for r in range(0, TM, NORM_ROWS):
            u = _norm_modulate(h_ref[r:r + NORM_ROWS], g_ref, sh_ref, sc_ref)
            u_scr[r:r + NORM_ROWS] = u
            _swiglu_accumulate(u, weights, out_ref.at[pl.ds(r, NORM_ROWS)], assign=True)

    @pl.when(j > 0)
    def _():
        _swiglu_accumulate(u_scr[...], _swiglu_weights(wg_ref, wu_ref, wd_ref), out_ref)

    @pl.when(j == pl.num_programs(1) - 1)
    def _():
        out_ref[...] = h_ref[...] + gt_ref[0] * out_ref[...]


def _ffn_dense(h, g, mod, w_gate, w_up, w_down):
    def mod_spec(chunk):
        return pl.BlockSpec((1, 1, D_MODEL), lambda i, j: (_mod_row(i, TM), 0, chunk))

    return pl.pallas_call(
        _ffn_kernel,
        grid=(N_ALL_TILES, D_FF // TF),
        in_specs=[pl.BlockSpec((TM, D_MODEL), lambda i, j: (i, 0)),
                  pl.BlockSpec((1, D_MODEL), lambda i, j: (0, 0)),
                  mod_spec(3), mod_spec(4), mod_spec(5),
                  pl.BlockSpec((D_MODEL, TF), lambda i, j: (0, j)),
                  pl.BlockSpec((D_MODEL, TF), lambda i, j: (0, j)),
                  pl.BlockSpec((TF, D_MODEL), lambda i, j: (j, 0))],
        out_specs=pl.BlockSpec((TM, D_MODEL), lambda i, j: (i, 0)),
        out_shape=jax.ShapeDtypeStruct((ROWS, D_MODEL), F32),
        scratch_shapes=[pltpu.VMEM((TM, D_MODEL), BF16)],
        compiler_params=_cparams(("arbitrary", "arbitrary")),
        name="ffn_dense",
    )(h, g, mod, mod, mod, w_gate, w_up, w_down)


META_LANES = 128


def _route_kernel(h_ref, g_ref, sh_ref, sc_ref, wr_ref, br_ref, u_ref, meta_ref, cnt_ref, carry):
    i = pl.program_id(0)

    @pl.when(i == 0)
    def _():
        carry[...] = jnp.zeros_like(carry)

    y = _rms(h_ref[...]) * g_ref[...]
    u = y * (1.0 + sc_ref[0]) + sh_ref[0]
    u_ref[...] = u

    w = wr_ref[...]
    u_hi = u.astype(BF16)
    u_lo = (u - u_hi.astype(F32)).astype(BF16)
    w_hi = w.astype(BF16)
    w_lo = (w - w_hi.astype(F32)).astype(BF16)
    logits = _dot(u_hi, w_hi) + (_dot(u_hi, w_lo) + _dot(u_lo, w_hi)) + br_ref[...]

    lane = lax.broadcasted_iota(jnp.int32, logits.shape, 1)
    logits = jnp.where(lane < N_EXPERTS, logits, NEG_INF)
    lane_f = lane.astype(F32)
    m1 = jnp.max(logits, axis=-1, keepdims=True)
    i1 = jnp.min(jnp.where(logits == m1, lane_f, float(META_LANES)), axis=-1, keepdims=True)
    pick1 = lane_f == i1
    rest = jnp.where(pick1, NEG_INF, logits)
    m2 = jnp.max(rest, axis=-1, keepdims=True)
    i2 = jnp.min(jnp.where(rest == m2, lane_f, float(META_LANES)), axis=-1, keepdims=True)
    pick2 = lane_f == i2
    e = jnp.exp(m2 - m1)
    g1 = 1.0 / (1.0 + e)
    g2 = e / (1.0 + e)

    onehot = jnp.where(pick1 | pick2, 1.0, 0.0)
    rr = lax.broadcasted_iota(jnp.int32, (TM, TM), 0)
    cc = lax.broadcasted_iota(jnp.int32, (TM, TM), 1)
    earlier = jnp.where(cc < rr, 1.0, 0.0).astype(BF16)
    before = _dot(earlier, onehot.astype(BF16)) + carry[...]
    r1 = jnp.sum(jnp.where(pick1, before, 0.0), axis=-1, keepdims=True)
    r2 = jnp.sum(jnp.where(pick2, before, 0.0), axis=-1, keepdims=True)
    carry[...] += jnp.sum(onehot, axis=0, keepdims=True)

    meta = jnp.zeros(logits.shape, F32)
    for k, val in enumerate((i1, i2, r1, r2, g1, g2)):
        meta = jnp.where(lane == k, val, meta)
    meta_ref[...] = meta
    cnt_ref[...] = jnp.broadcast_to(carry[...], cnt_ref.shape)


def _moe_route(h, g, mod, w_router, b_router):
    def mod_spec(chunk):
        return pl.BlockSpec((1, 1, D_MODEL), lambda i: (_mod_row(i, TM), 0, chunk))

    return pl.pallas_call(
        _route_kernel,
        grid=(N_LAT_TILES,),
        in_specs=[pl.BlockSpec((TM, D_MODEL), lambda i: (i, 0)),
                  pl.BlockSpec((1, D_MODEL), lambda i: (0, 0)),
                  mod_spec(3), mod_spec(4),
                  pl.BlockSpec((D_MODEL, META_LANES), lambda i: (0, 0)),
                  pl.BlockSpec((1, META_LANES), lambda i: (0, 0))],
        out_specs=[pl.BlockSpec((TM, D_MODEL), lambda i: (i, 0)),
                   pl.BlockSpec((TM, META_LANES), lambda i: (i, 0)),
                   pl.BlockSpec((8, META_LANES), lambda i: (0, 0))],
        out_shape=[jax.ShapeDtypeStruct((T_LAT, D_MODEL), F32),
                   jax.ShapeDtypeStruct((T_LAT, META_LANES), F32),
                   jax.ShapeDtypeStruct((8, META_LANES), F32)],
        scratch_shapes=[pltpu.VMEM((1, META_LANES), F32)],
        compiler_params=_cparams(("arbitrary",)),
        name="moe_route",
    )(h, g, mod, mod, w_router, b_router)


def _row_copy(src, s, dst, d, sem):
    return pltpu.make_async_copy(src.at[pl.ds(s, 1)], dst.at[pl.ds(d, 1)], sem)


MOE_NJ = D_FF_EXPERT // TF
GATHER_ROWS_PER_STEP = TM // MOE_NJ


TAIL_ROWS = range(GATHER_ROWS_PER_STEP * MOE_NJ, TM)
TOK_ROWS = 2 * T_LAT + TM
HALF = TM // 2
ROW_COPY_PRIORITY = 1


def _moe_ffn_kernel(te_ref, nu_ref, nv_ref, src_ref, dst_ref, u_ref, wg_ref, wu_ref, wd_ref, tok_ref,
                    xg_scr, xb_scr, acc, sem):
    i, j = pl.program_id(0), pl.program_id(1)
    n_used = nu_ref[0]
    used = i < n_used
    last = n_used - 1
    slot = i % 2
    nxt = jnp.minimum(i + 1, last)
    prev = i - 1

    def get_row(tile, r):
        _row_copy(u_ref, src_ref[tile * TM + r], xg_scr, r, sem.at[0]).start(priority=ROW_COPY_PRIORITY)

    def put_row(tile, s, r):
        _row_copy(acc.at[s], r, tok_ref, dst_ref[(tile + 1) * TM + r], sem.at[1]).start(priority=ROW_COPY_PRIORITY)

    def wait_rows(get):
        def wait(r, c):
            if get:
                _row_copy(u_ref, 0, xg_scr, 0, sem.at[0]).wait()
            else:
                _row_copy(acc.at[0], 0, tok_ref, 0, sem.at[1]).wait()
            return c
        lax.fori_loop(0, TM, wait, 0, unroll=8)

    @pl.when((i == 0) & (j == 0))
    def _():
        lax.fori_loop(0, TM, lambda r, c: (get_row(0, r), c)[1], 0)
        acc[1] = jnp.zeros((TM, D_MODEL), F32)

    @pl.when(used & (j == 0))
    def _():
        wait_rows(True)
        xb_scr[...] = xg_scr[...].astype(BF16)
        acc[slot] = jnp.zeros((TM, D_MODEL), F32)
        for r in TAIL_ROWS:
            get_row(nxt, r)
            put_row(prev, 1 - slot, r)

    def step(rows):
        for r in range(GATHER_ROWS_PER_STEP):
            get_row(nxt, r * MOE_NJ + j)
            put_row(prev, 1 - slot, r * MOE_NJ + j)
        _swiglu_accumulate(xb_scr[:rows], _swiglu_weights(wg_ref, wu_ref, wd_ref), acc.at[slot])

    pl.when(used & (nv_ref[i] > HALF))(lambda: step(TM))
    pl.when(used & (nv_ref[i] <= HALF))(lambda: step(HALF))

    @pl.when(used & (j == MOE_NJ - 1))
    def _():
        wait_rows(False)

    @pl.when((i == last) & (j == MOE_NJ - 1))
    def _():
        wait_rows(True)
        lax.fori_loop(0, TM, lambda r, c: (put_row(i, slot, r), c)[1], 0)
        wait_rows(False)


def _moe_ffn(u, src, dst, tile_expert, n_used, n_valid, w_gate, w_up, w_down):
    def jj(i, j, nu):
        return jnp.where(i < nu[0], j, MOE_NJ - 1)

    grid_spec = pltpu.PrefetchScalarGridSpec(
        num_scalar_prefetch=5,
        grid=(MOE_TILES, MOE_NJ),
        in_specs=[pl.BlockSpec(memory_space=pl.ANY),
                  pl.BlockSpec((None, D_MODEL, TF), lambda i, j, te, nu, *_: (te[i], 0, jj(i, j, nu))),
                  pl.BlockSpec((None, D_MODEL, TF), lambda i, j, te, nu, *_: (te[i], 0, jj(i, j, nu))),
                  pl.BlockSpec((None, TF, D_MODEL), lambda i, j, te, nu, *_: (te[i], jj(i, j, nu), 0))],
        out_specs=pl.BlockSpec(memory_space=pl.ANY),
        scratch_shapes=[pltpu.VMEM((TM, D_MODEL), F32), pltpu.VMEM((TM, D_MODEL), BF16),
                        pltpu.VMEM((2, TM, D_MODEL), F32), pltpu.SemaphoreType.DMA((2,))],
    )
    return pl.pallas_call(
        _moe_ffn_kernel,
        grid_spec=grid_spec,
        out_shape=jax.ShapeDtypeStruct((TOK_ROWS, D_MODEL), F32),
        compiler_params=_cparams(("arbitrary", "arbitrary")),
        name="moe_ffn",
    )(tile_expert, n_used, n_valid, src, dst, u, w_gate, w_up, w_down)


def _combine_kernel(h_ref, meta_ref, gt_ref, gf_ref, y1_ref, y2_ref, out_ref):
    meta = meta_ref[...]
    y = meta[:, 4:5] * y1_ref[...] + meta[:, 5:6] * y2_ref[...]
    hn = h_ref[...] + gt_ref[0] * y
    out_ref[...] = _rms(hn) * gf_ref[...]


def _moe_combine(h, meta, mod, g_final, tok):
    n = T_LAT // TC
    row = pl.BlockSpec((TC, D_MODEL), lambda i: (i, 0))
    return pl.pallas_call(
        _combine_kernel,
        grid=(n,),
        in_specs=[row,
                  pl.BlockSpec((TC, META_LANES), lambda i: (i, 0)),
                  pl.BlockSpec((1, 1, D_MODEL), lambda i: (i // (SEQ // TC), 0, 5)),
                  pl.BlockSpec((1, D_MODEL), lambda i: (0, 0)),
                  row,
                  pl.BlockSpec((TC, D_MODEL), lambda i: (n + i, 0))],
        out_specs=row,
        out_shape=jax.ShapeDtypeStruct((T_LAT, D_MODEL), F32),
        compiler_params=_cparams(("arbitrary",)),
        name="moe_combine",
    )(h, meta, mod, g_final, tok, tok)


def _moe_plan(meta, counts_f):
    i1 = meta[:, 0].astype(jnp.int32)
    i2 = meta[:, 1].astype(jnp.int32)
    r1 = meta[:, 2].astype(jnp.int32)
    r2 = meta[:, 3].astype(jnp.int32)
    counts = counts_f[0, :N_EXPERTS].astype(jnp.int32)
    padded = (counts + TM - 1) // TM * TM
    ends = jnp.cumsum(padded)
    starts = ends - padded
    experts = jnp.arange(N_EXPERTS, dtype=jnp.int32)

    def start_of(idx):
        return jnp.sum(jnp.where(idx[:, None] == experts, starts, 0), axis=1)

    pos = jnp.concatenate([start_of(i1) + r1, start_of(i2) + r2])
    n_used = ends[-1] // TM
    tile_start = jnp.arange(MOE_TILES, dtype=jnp.int32) * TM
    expert = jnp.minimum(jnp.sum(tile_start[:, None] >= ends[None, :], axis=1), N_EXPERTS - 1)
    n_valid = jnp.clip((starts + counts)[expert] - tile_start, 0, TM).astype(jnp.int32)
    expert = expert[jnp.minimum(jnp.arange(MOE_TILES), n_used - 1)].astype(jnp.int32)
    slot = jnp.arange(MOE_ROWS, dtype=jnp.int32)
    owner = jnp.full((MOE_ROWS,), -1, jnp.int32).at[pos].set(jnp.arange(2 * T_LAT, dtype=jnp.int32),
                                                             unique_indices=True, mode="promise_in_bounds")
    spare = 2 * T_LAT + slot % TM
    dst = jnp.concatenate([spare[:TM], jnp.where(owner >= 0, owner, spare)])
    src = jnp.where(owner >= 0, owner % T_LAT, 0)
    return src, dst, expert, n_used.reshape(1).astype(jnp.int32), n_valid


def _rope_tables():
    t = np.arange(SEQ)
    row, col = (t // GRID_W).astype(np.float64), (t % GRID_W).astype(np.float64)
    out = {}
    for dim, name in ((MLA_ROPE, "64"), (HEAD, "128")):
        half = dim // 2
        freqs = ROPE_THETA ** (-np.arange(0, half, 2, dtype=np.float64) / half)
        ar, ac = row[:, None] * freqs, col[:, None] * freqs
        c = np.concatenate([np.cos(ar), np.cos(ar), np.cos(ac), np.cos(ac)], axis=1)
        s = np.concatenate([-np.sin(ar), np.sin(ar), -np.sin(ac), np.sin(ac)], axis=1)
        order = _rope_lane_order(dim)
        c = np.tile(c, (1, HEAD // dim))[:, order]
        s = np.tile(s, (1, HEAD // dim))[:, order]
        out["c" + name] = jnp.asarray(np.concatenate([c, np.ones((TP, HEAD))], axis=0), F32)
        out["s" + name] = jnp.asarray(np.concatenate([s, np.zeros((TP, HEAD))], axis=0), F32)
    return out


def _na_bias_table(rel_bias):
    qc = np.arange(GRID_W)[:, None]
    kc = np.arange(GRID_W)[None, :]
    cs = np.clip(qc - NA_KW // 2, 0, GRID_W - NA_KW)
    col_ok = (kc >= cs) & (kc < cs + NA_KW)
    col_off = np.clip(kc - qc + NA_KW - 1, 0, 2 * NA_KW - 2)
    pick = jnp.asarray(col_off[None] == np.arange(2 * NA_KW - 1)[:, None, None], F32)
    tab = jnp.einsum("had,dqk->haqk", rel_bias * LOG2E, pick, precision=lax.Precision.HIGHEST)
    tab = jnp.where(jnp.asarray(col_ok), tab, NEG_INF)
    tab = jnp.concatenate([tab, jnp.full((NA_HEADS, 1, GRID_W, GRID_W), NEG_INF, F32)], axis=1)
    return jnp.concatenate([tab, tab], axis=-1)


def _layer_weights(w_in, mla_g_q, mla_w_qup, mla_g_kv, mla_w_kvup, gqa_g_q, gqa_g_k):
    cuts = (1536, 2048, 2112, 2880, 3136, 4160)
    na, mla_c, kpe, gqa_qk, gqa_v, diff_qk, diff_v = jnp.split(w_in, cuts, axis=2)
    kpe = _to_rope_lane_order(jnp.pad(kpe, ((0, 0), (0, 0), (0, HEAD - MLA_ROPE))), MLA_ROPE)
    gqa_qk = _to_rope_lane_order(gqa_qk, HEAD)
    diff_qk = _to_rope_lane_order(diff_qk, DIFF_QK)
    pad = jnp.zeros((DEPTH, D_MODEL, IN_COLS_PAD - COL_KPE - HEAD), F32)
    w_in_p = jnp.concatenate([na, mla_c, gqa_qk, gqa_v, diff_qk, diff_v, kpe, pad], axis=2).astype(BF16)

    o64, o128 = _rope_lane_order(MLA_ROPE), _rope_lane_order(HEAD)
    qup = mla_w_qup.reshape(DEPTH, MLA_Q_RANK, MLA_HEADS, MLA_NOPE + MLA_ROPE)
    qup = jnp.pad(qup, ((0, 0), (0, 0), (0, 0), (0, 2 * HEAD - MLA_NOPE - MLA_ROPE)))
    qup = jnp.concatenate([qup[..., :HEAD], qup[..., HEAD:][..., o64]], axis=-1)
    kvup = mla_w_kvup.reshape(DEPTH, MLA_KV_RANK, MLA_HEADS, MLA_NOPE + MLA_V)
    return {
        "w_in": w_in_p,
        "g_cq": mla_g_q.reshape(DEPTH, 1, -1),
        "w_qup": qup.reshape(DEPTH, MLA_Q_RANK, MLA_HEADS * 2 * HEAD).astype(BF16),
        "g_ckv": mla_g_kv.reshape(DEPTH, 1, -1),
        "w_kvk": kvup[..., :MLA_NOPE].reshape(DEPTH, MLA_KV_RANK, -1).astype(BF16),
        "w_kvv": kvup[..., MLA_NOPE:].reshape(DEPTH, MLA_KV_RANK, -1).astype(BF16),
        "g_gq": gqa_g_q[:, o128].reshape(DEPTH, 1, -1),
        "g_gk": gqa_g_k[:, o128].reshape(DEPTH, 1, -1),
    }


def kernel(x, c, ctx, c_ctx, w_mod, b_mod, g_mix, w_in, na_rel_bias, mla_g_q, mla_w_qup, mla_g_kv, mla_w_kvup,
           gqa_g_q, gqa_g_k, diff_lq1, diff_lk1, diff_lq2, diff_lk2, diff_g_sub, w_out, g_ffn,
           ffn_w_gate, ffn_w_up, ffn_w_down, moe_w_router, moe_b_router, moe_w_gate, moe_w_up, moe_w_down,
           g_final):
    tabs = _rope_tables()
    lw = _layer_weights(w_in, mla_g_q, mla_w_qup, mla_g_kv, mla_w_kvup, gqa_g_q, gqa_g_k)
    cvec =jnp.concatenate([c, c_ctx[None, :], jnp.zeros((16 - BATCH - 1, D_MODEL), F32)], axis=0)
    mod_all = _modulation(cvec, w_mod, b_mod)
    stream = (x.reshape(T_LAT, D_MODEL), ctx.reshape(T_CTX, D_MODEL))

    out = None
    for l in range(DEPTH):
        last = l == DEPTH - 1
        with_ctx = not last
        lam_init = 0.8 - 0.6 * math.exp(-0.3 * l)
        mod = mod_all[l].reshape(16, 1, 6 * D_MODEL)

        proj = _in_proj(stream, g_mix[l].reshape(1, -1), mod, lw["w_in"], l)
        q_mla, k_mla, v_mla, q_gqa, k_gqa, q_diff, k_diff = _prep(proj, lw, l, tabs)

        o_na = _na_attention(proj, _na_bias_table(na_rel_bias[l]), with_ctx)
        o_mla = _global_attention(q_mla, 2 * HEAD, k_mla, 2 * HEAD, MLA_HEADS, 0, v_mla, 0, with_ctx)
        o_gqa = _global_attention(q_gqa, HEAD, k_gqa, HEAD, GQA_KV_HEADS, 0,
                                  proj, COL_GQA_V // (GQA_KV_HEADS * HEAD), with_ctx)
        lam_vecs = jnp.stack([diff_lq1[l], diff_lk1[l], diff_lq2[l], diff_lk2[l]])
        o_diff = _global_attention(q_diff, HEAD, k_diff, HEAD, DIFF_HEADS, 0,
                                   proj, COL_DIFF_V // (DIFF_HEADS * HEAD), with_ctx,
                                   diff=(lam_vecs, diff_g_sub[l].reshape(1, -1), lam_init))

        h = _out_proj((o_na, o_mla, o_gqa, o_diff), w_out[l].astype(BF16), stream, mod,
                      ROWS if with_ctx else T_LAT)

        if l % 2 == 0:
            h = _ffn_dense(h, g_ffn[l].reshape(1, -1), mod, ffn_w_gate[l // 2], ffn_w_up[l // 2],
                           ffn_w_down[l // 2])
            stream = (h,)
        else:
            m = l // 2
            w_r = jnp.pad(moe_w_router[m], ((0, 0), (0, META_LANES - N_EXPERTS)))
            b_r = jnp.pad(moe_b_router[m], (0, META_LANES - N_EXPERTS)).reshape(1, -1)
            u, meta, counts = _moe_route(h, g_ffn[l].reshape(1, -1), mod, w_r, b_r)
            src, dst, tile_expert, n_used, n_valid = _moe_plan(meta, counts)
            tok = _moe_ffn(u, src, dst, tile_expert, n_used, n_valid, moe_w_gate[m], moe_w_up[m], moe_w_down[m])
            out = _moe_combine(h, meta, mod, g_final.reshape(1, -1), tok)
    return out.reshape(BATCH, SEQ, D_MODEL)
```
